```python
import math
import jax, jax.numpy as jnp
from jax import lax
import numpy as np

D_MODEL = 1024
BATCH = 8
SEQ = 2048
DEPTH = 2

MIX_WIDTH = D_MODEL
A_HEADS = 4
A_HEAD_DIM = MIX_WIDTH // 2 // A_HEADS
A_WIDTH = A_HEADS * A_HEAD_DIM
B_HEADS = 4
B_HEAD_DIM = MIX_WIDTH // 2 // B_HEADS
B_WIDTH = B_HEADS * B_HEAD_DIM
IN_AB = A_WIDTH + 2 * B_WIDTH
CONV_WIDTH = 31
POOL_WINDOWS = (2, 4, 8, 16)
C_GROUPS = len(POOL_WINDOWS)
C_GROUP_DIM = D_MODEL // C_GROUPS
D_FF = int(math.ceil((8 * D_MODEL / 3) / 256) * 256)
N_EVEN = (DEPTH + 1) // 2
N_ODD = DEPTH // 2
RMS_EPS = 1e-6
LN_EPS = 1e-5

kernel_name = "hybrid_fnet_conformer_poolformer_encoder"


def rmsnorm(x, g):
    xf = x.astype(jnp.float32)
    inv = lax.rsqrt(jnp.mean(xf * xf, axis=-1, keepdims=True) + RMS_EPS)
    return (xf * inv).astype(x.dtype) * g


def layernorm(x, g, b):
    xf = x.astype(jnp.float32)
    mu = jnp.mean(xf, axis=-1, keepdims=True)
    var = jnp.mean(jnp.square(xf - mu), axis=-1, keepdims=True)
    return ((xf - mu) * lax.rsqrt(var + LN_EPS)).astype(x.dtype) * g + b


def swiglu_ffn(h, w_gate, w_up, w_down):
    return (jax.nn.silu(h @ w_gate) * (h @ w_up)) @ w_down


def fnet_heads(a, fnet_map):
    bsz, seq, _ = a.shape
    a4 = a.reshape(bsz, seq, A_HEADS, A_HEAD_DIM).astype(jnp.float32)
    f = jnp.fft.fft2(a4, axes=(1, 3), norm="ortho").real.astype(a.dtype)
    y = jnp.einsum("bshd,hde->bshe", f, fnet_map)
    return y.reshape(bsz, seq, A_WIDTH)


def conformer_conv_heads(v, gate, conv_w, conv_b, ln_g, ln_b):
    u = v * jax.nn.sigmoid(gate)
    pad = CONV_WIDTH // 2
    u = lax.conv_general_dilated(
        u, conv_w[:, None, :].astype(u.dtype),
        window_strides=(1,), padding=[(pad, pad)],
        dimension_numbers=("NWC", "WIO", "NWC"),
        feature_group_count=B_WIDTH) + conv_b
    bsz, seq, _ = u.shape
    u = u.reshape(bsz, seq, B_HEADS, B_HEAD_DIM)
    u = layernorm(u, ln_g.reshape(B_HEADS, B_HEAD_DIM), ln_b.reshape(B_HEADS, B_HEAD_DIM))
    return jax.nn.silu(u).reshape(bsz, seq, B_WIDTH)


def centred_pool_minus_self(u, window):
    seq = u.shape[1]
    uf = u.astype(jnp.float32)
    cs = jnp.concatenate([jnp.zeros_like(uf[:, :1]), lax.cumsum(uf, axis=1)], axis=1)
    pos = jnp.arange(seq, dtype=jnp.int32)
    lo = jnp.clip(pos - window // 2, 0, seq)
    hi = jnp.clip(pos - window // 2 + window, 0, seq)
    win_sum = jnp.take(cs, hi, axis=1) - jnp.take(cs, lo, axis=1)
    cnt = (hi - lo).astype(jnp.float32)[None, :, None]
    return (win_sum / cnt - uf).astype(u.dtype)


def pool_mixer(h, pool_map, pool_scale):
    outs = []
    for gi, w in enumerate(POOL_WINDOWS):
        hg = h[..., gi * C_GROUP_DIM:(gi + 1) * C_GROUP_DIM]
        pg = centred_pool_minus_self(hg, w)
        outs.append(pg @ pool_map[gi])
    return jnp.concatenate(outs, axis=-1) * pool_scale


def setup_inputs(seed: int = 0) -> dict:
    key = jax.random.key(seed)
    ks = jax.random.split(key, 20)
    f32 = jnp.float32
    nrm = lambda k, shape, fan_in: jax.random.normal(k, shape, f32) * (fan_in ** -0.5)
    gain = lambda k, shape: 1.0 + 0.05 * jax.random.normal(k, shape, f32)
    return {
        "x": jax.random.normal(ks[0], (BATCH, SEQ, D_MODEL), f32),
        "norm_mix_g": gain(ks[1], (DEPTH, D_MODEL)),
        "norm_ffn_g": gain(ks[2], (DEPTH, D_MODEL)),
        "w_in_ab": nrm(ks[3], (N_EVEN, D_MODEL, IN_AB), D_MODEL),
        "fnet_map": nrm(ks[4], (N_EVEN, A_HEADS, A_HEAD_DIM, A_HEAD_DIM), A_HEAD_DIM),
        "conv_w": nrm(ks[5], (N_EVEN, CONV_WIDTH, B_WIDTH), CONV_WIDTH),
        "conv_b": 0.02 * jax.random.normal(ks[6], (N_EVEN, B_WIDTH), f32),
        "conv_ln_g": gain(ks[7], (N_EVEN, B_WIDTH)),
        "conv_ln_b": 0.02 * jax.random.normal(ks[8], (N_EVEN, B_WIDTH), f32),
        "w_out_ab": nrm(ks[9], (N_EVEN, MIX_WIDTH, D_MODEL), MIX_WIDTH),
        "pool_map": nrm(ks[10], (N_ODD, C_GROUPS, C_GROUP_DIM, C_GROUP_DIM), C_GROUP_DIM),
        "pool_scale": 1.0 + 0.1 * jax.random.normal(ks[11], (N_ODD, D_MODEL), f32),
        "ffn_w_gate": nrm(ks[12], (DEPTH, D_MODEL, D_FF), D_MODEL),
        "ffn_w_up": nrm(ks[13], (DEPTH, D_MODEL, D_FF), D_MODEL),
        "ffn_w_down": nrm(ks[14], (DEPTH, D_FF, D_MODEL), D_FF),
        "final_g": gain(ks[15], (D_MODEL,)),
    }


def reference(x, norm_mix_g, norm_ffn_g, w_in_ab, fnet_map, conv_w, conv_b, conv_ln_g,
              conv_ln_b, w_out_ab, pool_map, pool_scale, ffn_w_gate, ffn_w_up, ffn_w_down,
              final_g):
    for layer in range(DEPTH):
        h = rmsnorm(x, norm_mix_g[layer])
        if layer % 2 == 0:
            e = layer // 2
            p = h @ w_in_ab[e]
            ya = fnet_heads(p[..., :A_WIDTH], fnet_map[e])
            yb = conformer_conv_heads(p[..., A_WIDTH:A_WIDTH + B_WIDTH],
                                      p[..., A_WIDTH + B_WIDTH:],
                                      conv_w[e], conv_b[e], conv_ln_g[e], conv_ln_b[e])
            y = jnp.concatenate([ya, yb], axis=-1) @ w_out_ab[e]
        else:
            o = layer // 2
            y = pool_mixer(h, pool_map[o], pool_scale[o])
        x = x + y
        h = rmsnorm(x, norm_ffn_g[layer])
        x = x + swiglu_ffn(h, ffn_w_gate[layer], ffn_w_up[layer], ffn_w_down[layer])
    return rmsnorm(x, final_g)
```

```python
import functools
import math

import jax
import jax.numpy as jnp
import numpy as np
from jax.experimental import pallas as pl
from jax.experimental.pallas import tpu as pltpu

RMS_EPS = 1e-6
LN_EPS = 1e-5

A_HEADS = 4
HEAD_DIM = 128
A_WIDTH = A_HEADS * HEAD_DIM
B_WIDTH = 512
CONV_WIDTH = 31
CONV_PAD = CONV_WIDTH // 2
POOL_WINDOWS = (2, 4, 8, 16)
POOL_HALO = 8

DFT_RADIX = 8

V7X_VMEM_LIMIT_BYTES = 56 * 1024 * 1024

TOKEN_TILE = 512
FF_CHUNKS = (1024, 1024, 768)
CONV_ROW_TILE = 32
BFLY_ROW_TILE = 16

_F32 = jnp.float32
_BF16 = jnp.bfloat16


def _resident(shape):
    nd = len(shape)
    return pl.BlockSpec(shape, lambda *_: (0,) * nd, pipeline_mode=pl.Buffered(1))


def _rmsnorm(xv, g):
    inv = jax.lax.rsqrt(jnp.mean(xv * xv, axis=-1, keepdims=True) + RMS_EPS)
    return (xv * inv) * g


def _silu(v):
    return v * jax.nn.sigmoid(v)


def _in_proj_kernel(x_ref, g_ref, w_ref, a_ref, u_ref, a_scr):
    h = _rmsnorm(x_ref[...], g_ref[...]).astype(_BF16)
    p = jnp.dot(h, w_ref[...], preferred_element_type=_F32)
    rows = a_scr.shape[1] // DFT_RADIX
    for lt in range(A_WIDTH // 128):
        lanes = slice(lt * 128, (lt + 1) * 128)
        a_scr[lt] = p[:, lanes]
        for jr in range(DFT_RADIX):
            a_ref[jr, :, lanes] = a_scr[lt, pl.ds(jr, rows, stride=DFT_RADIX), :].astype(_BF16)
    v = p[:, A_WIDTH:A_WIDTH + B_WIDTH]
    gate = p[:, A_WIDTH + B_WIDTH:]
    u_ref[...] = v * jax.nn.sigmoid(gate)


def _in_proj(x, g, w_in):
    bsz, seq, d = x.shape
    tm = TOKEN_TILE
    inner = seq // DFT_RADIX
    return pl.pallas_call(
        _in_proj_kernel,
        grid=(bsz, seq // tm),
        in_specs=[
            pl.BlockSpec((None, tm, d), lambda b, i: (b, i, 0)),
            _resident((1, d)),
            _resident(w_in.shape),
        ],
        out_specs=[
            pl.BlockSpec((None, DFT_RADIX, tm // DFT_RADIX, A_WIDTH), lambda b, i: (b, 0, i, 0)),
            pl.BlockSpec((None, tm, B_WIDTH), lambda b, i: (b, i, 0)),
        ],
        out_shape=[
            jax.ShapeDtypeStruct((bsz, DFT_RADIX, inner, A_WIDTH), _BF16),
            jax.ShapeDtypeStruct((bsz, seq, B_WIDTH), _F32),
        ],
        scratch_shapes=[pltpu.VMEM((A_WIDTH // 128, tm, 128), _F32)],
        compiler_params=pltpu.CompilerParams(
            dimension_semantics=("arbitrary", "arbitrary"),
            vmem_limit_bytes=V7X_VMEM_LIMIT_BYTES),
        name="in_proj",
    )(x, g, w_in)


def _cadd(a, b):
    return (a[0] + b[0], a[1] + b[1])


def _csub(a, b):
    return (a[0] - b[0], a[1] - b[1])


def _dft4(a0, a1, a2, a3):
    s0, s1 = _cadd(a0, a2), _csub(a0, a2)
    s2, s3 = _cadd(a1, a3), _csub(a1, a3)
    return (_cadd(s0, s2), (s1[0] + s3[1], s1[1] - s3[0]),
            _csub(s0, s2), (s1[0] - s3[1], s1[1] + s3[0]))


def _mul_w8(k, z):
    r, i = z
    h = math.sqrt(0.5)
    if k == 0:
        return z
    if k == 1:
        return (h * (r + i), h * (i - r))
    if k == 2:
        return (i, -r)
    return (h * (i - r), -h * (r + i))


def _mixers_kernel(a_ref, u_ref, cs_ref, twc_ref, tws_ref, cdsd_ref, map_ref,
                   cw_ref, cb_ref, lg_ref, lb_ref, y_ref, yr_scr, yi_scr, upad_scr):
    seq = u_ref.shape[0]
    inner = seq // DFT_RADIX

    for jr in range(DFT_RADIX):
        yy = jnp.dot(cs_ref[...], a_ref[jr], preferred_element_type=_F32)
        yr_scr[jr * inner:(jr + 1) * inner, :] = yy[:inner]
        yi_scr[jr * inner:(jr + 1) * inner, :] = yy[inner:]

    rt = BFLY_ROW_TILE

    def bfly(c, carry):
        r0 = pl.multiple_of(c * rt, rt)
        for lt in range(A_WIDTH // 128):
            lanes = slice(lt * 128, (lt + 1) * 128)
            z = []
            for jr in range(DFT_RADIX):
                rows = pl.ds(jr * inner + r0, rt)
                yr = yr_scr[rows, lanes]
                yi = yi_scr[rows, lanes]
                if jr == 0:
                    z.append((yr, yi))
                else:
                    tc = twc_ref[rows, :]
                    ts = tws_ref[rows, :]
                    z.append((yr * tc + yi * ts, yi * tc - yr * ts))
            ev = _dft4(z[0], z[2], z[4], z[6])
            od = _dft4(z[1], z[3], z[5], z[7])
            for k in range(4):
                w = _mul_w8(k, od[k])
                lo = _cadd(ev[k], w)
                hi = _csub(ev[k], w)
                rows_lo = pl.ds(k * inner + r0, rt)
                rows_hi = pl.ds((k + 4) * inner + r0, rt)
                yr_scr[rows_lo, lanes] = lo[0]
                yi_scr[rows_lo, lanes] = lo[1]
                yr_scr[rows_hi, lanes] = hi[0]
                yi_scr[rows_hi, lanes] = hi[1]
        return carry

    jax.lax.fori_loop(0, inner // rt, bfly, 0)

    for hd in range(A_HEADS):
        lanes = slice(hd * HEAD_DIM, (hd + 1) * HEAD_DIM)
        lhs = jnp.concatenate([yr_scr[:, lanes].astype(_BF16),
                               yi_scr[:, lanes].astype(_BF16)], axis=1)
        f = jnp.dot(lhs, cdsd_ref[...], preferred_element_type=_F32)
        ya = jnp.dot(f.astype(_BF16), map_ref[hd], preferred_element_type=_F32)
        y_ref[:, lanes] = ya.astype(_BF16)

    halo = 16
    upad_scr[0:halo, :] = jnp.zeros((halo, B_WIDTH), _F32)
    upad_scr[halo + seq:halo + seq + halo, :] = jnp.zeros((halo, B_WIDTH), _F32)
    upad_scr[halo:halo + seq, :] = u_ref[...]
    ct = CONV_ROW_TILE

    def conv(c, carry):
        r0 = pl.multiple_of(c * ct, ct)
        for lt in range(B_WIDTH // 128):
            lanes = slice(lt * 128, (lt + 1) * 128)
            win = upad_scr[pl.ds(r0, ct + 2 * halo), lanes]
            acc = jnp.zeros((ct, 128), _F32)
            for k in range(CONV_WIDTH):
                off = halo - CONV_PAD + k
                acc = acc + win[off:off + ct] * cw_ref[k:k + 1, lanes]
            cv = acc + cb_ref[:, lanes]
            mu = jnp.mean(cv, axis=-1, keepdims=True)
            dv = cv - mu
            var = jnp.mean(dv * dv, axis=-1, keepdims=True)
            yn = (dv * jax.lax.rsqrt(var + LN_EPS)) * lg_ref[:, lanes] + lb_ref[:, lanes]
            y_ref[pl.ds(r0, ct), A_WIDTH + lt * 128:A_WIDTH + (lt + 1) * 128] = _silu(yn).astype(_BF16)
        return carry

    jax.lax.fori_loop(0, seq // ct, conv, 0)


def _mixers(a_perm, u, consts, fmap, conv_w, conv_b, ln_g, ln_b):
    bsz, seq, _ = u.shape
    inner = seq // DFT_RADIX
    cs, twc, tws, cdsd = consts
    return pl.pallas_call(
        _mixers_kernel,
        grid=(bsz,),
        in_specs=[
            pl.BlockSpec((None, DFT_RADIX, inner, A_WIDTH), lambda b: (b, 0, 0, 0)),
            pl.BlockSpec((None, seq, B_WIDTH), lambda b: (b, 0, 0)),
            _resident(cs.shape), _resident(twc.shape), _resident(tws.shape),
            _resident(cdsd.shape), _resident(fmap.shape),
            _resident(conv_w.shape), _resident(conv_b.shape),
            _resident(ln_g.shape), _resident(ln_b.shape),
        ],
        out_specs=pl.BlockSpec((None, seq, A_WIDTH + B_WIDTH), lambda b: (b, 0, 0)),
        out_shape=jax.ShapeDtypeStruct((bsz, seq, A_WIDTH + B_WIDTH), _BF16),
        scratch_shapes=[
            pltpu.VMEM((seq, A_WIDTH), _F32),
            pltpu.VMEM((seq, A_WIDTH), _F32),
            pltpu.VMEM((seq + 32, B_WIDTH), _F32),
        ],
        compiler_params=pltpu.CompilerParams(
            dimension_semantics=("arbitrary",),
            vmem_limit_bytes=V7X_VMEM_LIMIT_BYTES),
        name="mixers",
    )(a_perm, u, cs, twc, tws, cdsd, fmap, conv_w, conv_b, ln_g, ln_b)


def _dft_constants(seq):
    inner = seq // DFT_RADIX
    k = np.arange(inner, dtype=np.float64)
    ang = 2.0 * np.pi * np.outer(k, k) / inner
    cs = np.concatenate([np.cos(ang), -np.sin(ang)], axis=0)
    jr = np.arange(DFT_RADIX, dtype=np.float64)[:, None]
    tw = 2.0 * np.pi * (jr * k[None, :]) / seq
    twc = np.repeat(np.cos(tw).reshape(seq, 1), 128, axis=1)
    tws = np.repeat(np.sin(tw).reshape(seq, 1), 128, axis=1)
    d = np.arange(HEAD_DIM, dtype=np.float64)
    angd = 2.0 * np.pi * np.outer(d, d) / HEAD_DIM
    scale = 1.0 / math.sqrt(seq * HEAD_DIM)
    cdsd = np.concatenate([np.cos(angd), np.sin(angd)], axis=0) * scale
    return (jnp.asarray(cs, _F32).astype(_BF16), jnp.asarray(twc, _F32), jnp.asarray(tws, _F32),
            jnp.asarray(cdsd, _F32).astype(_BF16))


def _ffn_kernel(*refs, has_out_proj, has_final):
    it = iter(refs)
    x_ref = next(it)
    if has_out_proj:
        y_ref = next(it)
        wo_ref = next(it)
    g_ref = next(it)
    wg_ref = next(it)
    wu_ref = next(it)
    wd_ref = next(it)
    if has_final:
        fg_ref = next(it)
    o_ref = next(it)

    xv = x_ref[...]
    if has_out_proj:
        xv = xv + jnp.dot(y_ref[...], wo_ref[...], preferred_element_type=_F32)
    h = _rmsnorm(xv, g_ref[...]).astype(_BF16)
    acc = xv
    c0 = 0
    for cw in FF_CHUNKS:
        gt = jnp.dot(h, wg_ref[:, c0:c0 + cw], preferred_element_type=_F32)
        up = jnp.dot(h, wu_ref[:, c0:c0 + cw], preferred_element_type=_F32)
        act = (_silu(gt) * up).astype(_BF16)
        acc = acc + jnp.dot(act, wd_ref[c0:c0 + cw, :], preferred_element_type=_F32)
        c0 += cw
    if has_final:
        acc = _rmsnorm(acc, fg_ref[...])
    o_ref[...] = acc


def _ffn(x2d, g, wg, wu, wd, y2d=None, w_out=None, final_g=None):
    tokens, d = x2d.shape
    tm = TOKEN_TILE
    has_out_proj = y2d is not None
    has_final = final_g is not None
    assert sum(FF_CHUNKS) == wg.shape[1]
    args = [x2d]
    specs = [pl.BlockSpec((tm, d), lambda i: (i, 0))]
    if has_out_proj:
        args += [y2d, w_out]
        specs += [pl.BlockSpec((tm, y2d.shape[1]), lambda i: (i, 0)), _resident(w_out.shape)]
    args += [g, wg, wu, wd]
    specs += [_resident(g.shape), _resident(wg.shape), _resident(wu.shape), _resident(wd.shape)]
    if has_final:
        args.append(final_g)
        specs.append(_resident(final_g.shape))
    return pl.pallas_call(
        functools.partial(_ffn_kernel, has_out_proj=has_out_proj, has_final=has_final),
        grid=(tokens // tm,),
        in_specs=specs,
        out_specs=pl.BlockSpec((tm, d), lambda i: (i, 0)),
        out_shape=jax.ShapeDtypeStruct((tokens, d), _F32),
        compiler_params=pltpu.CompilerParams(
            dimension_semantics=("arbitrary",),
            vmem_limit_bytes=V7X_VMEM_LIMIT_BYTES),
        name="ffn_out_proj" if has_out_proj else "ffn_final",
    )(*args)


def _pool_kernel(x_ref, prev_ref, next_ref, g_ref, pm_ref, ps_ref, o_ref, hp_scr):
    i = pl.program_id(1)
    n_tiles = pl.num_programs(1)
    ts, d = x_ref.shape
    seq = ts * n_tiles
    g = g_ref[...]
    xv = x_ref[...]
    hm = _rmsnorm(xv, g)
    halo_iota = jax.lax.broadcasted_iota(jnp.int32, (POOL_HALO, 1), 0)
    prev_ok = (i * ts - POOL_HALO + halo_iota) >= 0
    next_ok = ((i + 1) * ts + halo_iota) < seq
    hp_scr[0:POOL_HALO, :] = jnp.where(prev_ok, _rmsnorm(prev_ref[...], g), 0.0)
    hp_scr[POOL_HALO:POOL_HALO + ts, :] = hm
    hp_scr[POOL_HALO + ts:, :] = jnp.where(next_ok, _rmsnorm(next_ref[...], g), 0.0)

    pos = i * ts + jax.lax.broadcasted_iota(jnp.int32, (ts, 1), 0)
    gd = d // len(POOL_WINDOWS)
    for gi, w in enumerate(POOL_WINDOWS):
        lanes = slice(gi * gd, (gi + 1) * gd)
        win = None
        for off in range(-(w // 2), w - w // 2):
            sl = hp_scr[POOL_HALO + off:POOL_HALO + off + ts, lanes]
            win = sl if win is None else win + sl
        lo = jnp.clip(pos - w // 2, 0, seq)
        hi = jnp.clip(pos - w // 2 + w, 0, seq)
        cnt = (hi - lo).astype(_F32)
        pg = win / cnt - hm[:, lanes]
        yg = jnp.dot(pg.astype(_BF16), pm_ref[gi], preferred_element_type=_F32)
        o_ref[:, lanes] = xv[:, lanes] + yg * ps_ref[:, lanes]


def _pool(x, g, pool_map, pool_scale):
    bsz, seq, d = x.shape
    ts = TOKEN_TILE
    hb = ts // POOL_HALO
    n_halo_blocks = seq // POOL_HALO
    return pl.pallas_call(
        _pool_kernel,
        grid=(bsz, seq // ts),
        in_specs=[
            pl.BlockSpec((None, ts, d), lambda b, i: (b, i, 0)),
            pl.BlockSpec((None, POOL_HALO, d), lambda b, i: (b, jnp.maximum(i * hb - 1, 0), 0)),
            pl.BlockSpec((None, POOL_HALO, d),
                         lambda b, i: (b, jnp.minimum((i + 1) * hb, n_halo_blocks - 1), 0)),
            _resident(g.shape), _resident(pool_map.shape), _resident(pool_scale.shape),
        ],
        out_specs=pl.BlockSpec((None, ts, d), lambda b, i: (b, i, 0)),
        out_shape=jax.ShapeDtypeStruct((bsz, seq, d), _F32),
        scratch_shapes=[pltpu.VMEM((ts + 2 * POOL_HALO, d), _F32)],
        compiler_params=pltpu.CompilerParams(
            dimension_semantics=("arbitrary", "arbitrary"),
            vmem_limit_bytes=V7X_VMEM_LIMIT_BYTES),
        name="pool_mixer",
    )(x, x, x, g, pool_map, pool_scale)


def kernel(x, norm_mix_g, norm_ffn_g, w_in_ab, fnet_map, conv_w, conv_b, conv_ln_g, conv_ln_b,
           w_out_ab, pool_map, pool_scale, ffn_w_gate, ffn_w_up, ffn_w_down, final_g):
    bsz, seq, d = x.shape
    tokens = bsz * seq
    row = lambda v: v.reshape(1, -1)
    bf = lambda v: v.astype(_BF16)

    a_perm, u = _in_proj(x, row(norm_mix_g[0]), bf(w_in_ab[0]))
    yab = _mixers(a_perm, u, _dft_constants(seq), bf(fnet_map[0]), conv_w[0], row(conv_b[0]),
                  row(conv_ln_g[0]), row(conv_ln_b[0]))
    x2 = _ffn(x.reshape(tokens, d), row(norm_ffn_g[0]), bf(ffn_w_gate[0]), bf(ffn_w_up[0]),
              bf(ffn_w_down[0]), y2d=yab.reshape(tokens, d), w_out=bf(w_out_ab[0]))

    x3 = _pool(x2.reshape(bsz, seq, d), row(norm_mix_g[1]), bf(pool_map[0]), row(pool_scale[0]))
    out = _ffn(x3.reshape(tokens, d), row(norm_ffn_g[1]), bf(ffn_w_gate[1]), bf(ffn_w_up[1]),
               bf(ffn_w_down[1]), final_g=row(final_g))
    return out.reshape(bsz, seq, d)
```

```python
import functools
import math

import jax
import jax.numpy as jnp
import numpy as np
from jax.experimental import pallas as pl
from jax.experimental.pallas import tpu as pltpu

RMS_EPS = 1e-6
LN_EPS = 1e-5

A_HEADS = 4
HEAD_DIM = 128
A_WIDTH = A_HEADS * HEAD_DIM
B_WIDTH = 512
CONV_WIDTH = 31
CONV_PAD = CONV_WIDTH // 2
POOL_WINDOWS = (2, 4, 8, 16)
POOL_HALO = 8

LANES = 128
SUBLANES = 8

DFT_RADIX = 8

V7X_VMEM_LIMIT_BYTES = 56 * 1024 * 1024

TOKEN_TILE = 512
GU_CHUNKS = (1024, 1024, 1024, 1024, 1024, 512)
BFLY_ROW_TILE = 16
CONV_HALO = 16
CONV_ROW_TILE = 64
CONV_UNITS_BEFORE_OUT_PROJ = 8
CONV_UNITS_PER_DOT = 4
BF16_SUBLANES = 16

_F32 = jnp.float32
_BF16 = jnp.bfloat16


def _resident(shape):
    nd = len(shape)
    return pl.BlockSpec(shape, lambda *_: (0,) * nd, pipeline_mode=pl.Buffered(1))


def _rmsnorm(xv, g):
    inv = jax.lax.rsqrt(jnp.mean(xv * xv, axis=-1, keepdims=True) + RMS_EPS)
    return (xv * inv) * g


def _silu(v):
    return v * jax.nn.sigmoid(v)


def _in_proj_kernel(x_ref, g_ref, w_ref, a_ref, u_ref, a_scr):
    h = _rmsnorm(x_ref[...], g_ref[...]).astype(_BF16)
    p = jnp.dot(h, w_ref[...], preferred_element_type=_F32)
    rows = a_scr.shape[1] // DFT_RADIX
    for lt in range(A_WIDTH // LANES):
        lanes = slice(lt * LANES, (lt + 1) * LANES)
        a_scr[lt] = p[:, lanes]
        for jr in range(DFT_RADIX):
            a_ref[jr, :, lanes] = a_scr[lt, pl.ds(jr, rows, stride=DFT_RADIX), :].astype(_BF16)
    v = p[:, A_WIDTH:A_WIDTH + B_WIDTH]
    gate = p[:, A_WIDTH + B_WIDTH:]
    u_ref[...] = v * jax.nn.sigmoid(gate)


def _in_proj(x, g, w_in):
    bsz, seq, d = x.shape
    tm = TOKEN_TILE
    inner = seq // DFT_RADIX
    return pl.pallas_call(
        _in_proj_kernel,
        grid=(bsz, seq // tm),
        in_specs=[
            pl.BlockSpec((None, tm, d), lambda b, i: (b, i, 0)),
            _resident((1, d)),
            _resident(w_in.shape),
        ],
        out_specs=[
            pl.BlockSpec((None, DFT_RADIX, tm // DFT_RADIX, A_WIDTH), lambda b, i: (b, 0, i, 0)),
            pl.BlockSpec((None, tm, B_WIDTH), lambda b, i: (b, i, 0)),
        ],
        out_shape=[
            jax.ShapeDtypeStruct((bsz, DFT_RADIX, inner, A_WIDTH), _BF16),
            jax.ShapeDtypeStruct((bsz, seq, B_WIDTH), _F32),
        ],
        scratch_shapes=[pltpu.VMEM((A_WIDTH // LANES, tm, LANES), _F32)],
        compiler_params=pltpu.CompilerParams(
            dimension_semantics=("arbitrary", "arbitrary"),
            vmem_limit_bytes=V7X_VMEM_LIMIT_BYTES),
        name="in_proj",
    )(x, g, w_in)


def _cadd(a, b):
    return (a[0] + b[0], a[1] + b[1])


def _csub(a, b):
    return (a[0] - b[0], a[1] - b[1])


def _dft4(a0, a1, a2, a3):
    s0, s1 = _cadd(a0, a2), _csub(a0, a2)
    s2, s3 = _cadd(a1, a3), _csub(a1, a3)
    return (_cadd(s0, s2), (s1[0] + s3[1], s1[1] - s3[0]),
            _csub(s0, s2), (s1[0] - s3[1], s1[1] + s3[0]))


def _mul_w8(k, z):
    r, i = z
    h = math.sqrt(0.5)
    if k == 0:
        return z
    if k == 1:
        return (h * (r + i), h * (i - r))
    if k == 2:
        return (i, -r)
    return (h * (i - r), -h * (r + i))


def _fnet_kernel(a_ref, cs_ref, twc_ref, tws_ref, cdsd_ref, map_ref, y_ref, yr_scr, yi_scr):
    seq = y_ref.shape[0]
    inner = seq // DFT_RADIX

    for jr in range(DFT_RADIX):
        yy = jnp.dot(cs_ref[...], a_ref[jr], preferred_element_type=_F32)
        yr_scr[jr * inner:(jr + 1) * inner, :] = yy[:inner]
        yi_scr[jr * inner:(jr + 1) * inner, :] = yy[inner:]

    rt = BFLY_ROW_TILE

    def bfly(c, carry):
        r0 = pl.multiple_of(c * rt, rt)
        for lt in range(A_WIDTH // LANES):
            lanes = slice(lt * LANES, (lt + 1) * LANES)
            z = []
            for jr in range(DFT_RADIX):
                rows = pl.ds(jr * inner + r0, rt)
                yr = yr_scr[rows, lanes]
                yi = yi_scr[rows, lanes]
                if jr == 0:
                    z.append((yr, yi))
                else:
                    tc = twc_ref[rows, :]
                    ts = tws_ref[rows, :]
                    z.append((yr * tc + yi * ts, yi * tc - yr * ts))
            ev = _dft4(z[0], z[2], z[4], z[6])
            od = _dft4(z[1], z[3], z[5], z[7])
            for k in range(4):
                w = _mul_w8(k, od[k])
                lo = _cadd(ev[k], w)
                hi = _csub(ev[k], w)
                rows_lo = pl.ds(k * inner + r0, rt)
                rows_hi = pl.ds((k + 4) * inner + r0, rt)
                yr_scr[rows_lo, lanes] = lo[0]
                yi_scr[rows_lo, lanes] = lo[1]
                yr_scr[rows_hi, lanes] = hi[0]
                yi_scr[rows_hi, lanes] = hi[1]
        return carry

    jax.lax.fori_loop(0, inner // rt, bfly, 0)

    for hd in range(A_HEADS):
        lanes = slice(hd * HEAD_DIM, (hd + 1) * HEAD_DIM)
        lhs = jnp.concatenate([yr_scr[:, lanes].astype(_BF16),
                               yi_scr[:, lanes].astype(_BF16)], axis=1)
        f = jnp.dot(lhs, cdsd_ref[...], preferred_element_type=_F32)
        ya = jnp.dot(f.astype(_BF16), map_ref[hd], preferred_element_type=_F32)
        y_ref[:, lanes] = ya.astype(_BF16)


def _fnet(a_perm, consts, fmap):
    bsz, _, inner, _ = a_perm.shape
    seq = inner * DFT_RADIX
    cs, twc, tws, cdsd = consts
    return pl.pallas_call(
        _fnet_kernel,
        grid=(bsz,),
        in_specs=[
            pl.BlockSpec((None, DFT_RADIX, inner, A_WIDTH), lambda b: (b, 0, 0, 0)),
            _resident(cs.shape), _resident(twc.shape), _resident(tws.shape),
            _resident(cdsd.shape), _resident(fmap.shape),
        ],
        out_specs=pl.BlockSpec((None, seq, A_WIDTH), lambda b: (b, 0, 0)),
        out_shape=jax.ShapeDtypeStruct((bsz, seq, A_WIDTH), _BF16),
        scratch_shapes=[
            pltpu.VMEM((seq, A_WIDTH), _F32),
            pltpu.VMEM((seq, A_WIDTH), _F32),
        ],
        compiler_params=pltpu.CompilerParams(
            dimension_semantics=("arbitrary",),
            vmem_limit_bytes=V7X_VMEM_LIMIT_BYTES),
        name="fnet",
    )(a_perm, cs, twc, tws, cdsd, fmap)


def _dft_constants(seq):
    inner = seq // DFT_RADIX
    k = np.arange(inner, dtype=np.float64)
    ang = 2.0 * np.pi * np.outer(k, k) / inner
    cs = np.concatenate([np.cos(ang), -np.sin(ang)], axis=0)
    jr = np.arange(DFT_RADIX, dtype=np.float64)[:, None]
    tw = 2.0 * np.pi * (jr * k[None, :]) / seq
    twc = np.repeat(np.cos(tw).reshape(seq, 1), LANES, axis=1)
    tws = np.repeat(np.sin(tw).reshape(seq, 1), LANES, axis=1)
    d = np.arange(HEAD_DIM, dtype=np.float64)
    angd = 2.0 * np.pi * np.outer(d, d) / HEAD_DIM
    scale = 1.0 / math.sqrt(seq * HEAD_DIM)
    cdsd = np.concatenate([np.cos(angd), np.sin(angd)], axis=0) * scale
    return (jnp.asarray(cs, _F32).astype(_BF16), jnp.asarray(twc, _F32), jnp.asarray(tws, _F32),
            jnp.asarray(cdsd, _F32).astype(_BF16))


def _conv_fill_window(tile, tiles_per_seq, main_ref, prev_ref, next_ref, win_scr):
    tm = main_ref.shape[0]
    halo = CONV_HALO
    pos = jnp.zeros((halo, 1), jnp.int32) + tile % tiles_per_seq
    win_scr[0:halo, :] = jnp.where(pos == 0, 0.0, prev_ref[...])
    win_scr[halo:halo + tm, :] = main_ref[...]
    win_scr[halo + tm:, :] = jnp.where(pos == tiles_per_seq - 1, 0.0, next_ref[...])


def _conv_unit(rc, lt, win_scr, cw_ref, cb_ref, lg_ref, lb_ref, out_ref):
    ct = CONV_ROW_TILE
    halo = CONV_HALO
    r0 = rc * ct
    lanes = slice(lt * LANES, (lt + 1) * LANES)
    first = halo - CONV_PAD
    span = ct + 2 * halo
    win = win_scr[r0:r0 + span, lanes]
    acc = None
    for s in range(SUBLANES):
        rot = win if s == 0 else pltpu.roll(win, span - s, axis=0)
        for q in range((first + CONV_WIDTH - 1) // SUBLANES + 1):
            k = SUBLANES * q + s - first
            if 0 <= k < CONV_WIDTH:
                term = rot[SUBLANES * q:SUBLANES * q + ct] * cw_ref[k:k + 1, lanes]
                acc = term if acc is None else acc + term
    cv = acc + cb_ref[:, lanes]
    mu = jnp.mean(cv, axis=-1, keepdims=True)
    dv = cv - mu
    var = jnp.mean(dv * dv, axis=-1, keepdims=True)
    yn = (dv * jax.lax.rsqrt(var + LN_EPS)) * lg_ref[:, lanes] + lb_ref[:, lanes]
    y = _silu(yn)
    out_ref[r0:r0 + ct, lanes] = y.astype(_BF16)
    return y[ct - SUBLANES:, :]


def _tied(tile, dep):
    zero = pltpu.bitcast(dep, jnp.uint32)
    zero = jax.lax.shift_right_logical(jax.lax.shift_right_logical(zero, jnp.uint32(16)), jnp.uint32(16))
    zero = pltpu.bitcast(zero, _F32)
    return tile + jnp.concatenate([zero, zero], axis=0)


def _conv_units(tm):
    return [(rc, lt) for rc in range(tm // CONV_ROW_TILE) for lt in range(B_WIDTH // LANES)]


def _conv_scratch(tm):
    return [pltpu.VMEM((tm + 2 * CONV_HALO, B_WIDTH), _F32)]


def _conv_first_kernel(main_ref, next_ref, cw_ref, cb_ref, lg_ref, lb_ref, out_ref, win_scr):
    _conv_fill_window(0, 2, main_ref, next_ref, next_ref, win_scr)
    for rc, lt in _conv_units(main_ref.shape[0]):
        _conv_unit(rc, lt, win_scr, cw_ref, cb_ref, lg_ref, lb_ref, out_ref)


def _conv_first(u2d, conv_w, conv_b, ln_g, ln_b):
    tm = TOKEN_TILE
    return pl.pallas_call(
        _conv_first_kernel,
        grid=(1,),
        in_specs=[
            pl.BlockSpec((tm, B_WIDTH), lambda i: (0, 0)),
            pl.BlockSpec((CONV_HALO, B_WIDTH), lambda i: (tm // CONV_HALO, 0)),
            _resident(conv_w.shape), _resident(conv_b.shape), _resident(ln_g.shape),
            _resident(ln_b.shape),
        ],
        out_specs=pl.BlockSpec((tm, B_WIDTH), lambda i: (0, 0)),
        out_shape=jax.ShapeDtypeStruct((tm, B_WIDTH), _BF16),
        scratch_shapes=_conv_scratch(tm),
        compiler_params=pltpu.CompilerParams(
            dimension_semantics=("arbitrary",),
            vmem_limit_bytes=V7X_VMEM_LIMIT_BYTES),
        name="conv_first",
    )(u2d, u2d, conv_w, conv_b, ln_g, ln_b)


def _ffn_kernel(*refs, has_mixer, has_final, tiles_per_seq):
    it = iter(refs)
    x_ref = next(it)
    if has_mixer:
        ya_ref = next(it)
        yb0_ref = next(it)
        u_refs = (next(it), next(it), next(it))
        conv_refs = (next(it), next(it), next(it), next(it))
        wo_ref = next(it)
    g_ref = next(it)
    wgu_ref = next(it)
    wd_ref = next(it)
    if has_final:
        fg_ref = next(it)
    o_ref = next(it)
    h_scr = next(it)
    act_scr = next(it)
    if has_mixer:
        yb_scr = next(it)
        win_scr = next(it)

    tm = x_ref.shape[0]
    xv = x_ref[...]
    conv_some = lambda n_units: None
    if has_mixer:
        i = pl.program_id(0)
        n = pl.num_programs(0)
        slot = i % 2

        @pl.when(i == 0)
        def _():
            yb_scr[0] = yb0_ref[...]

        yb = yb_scr[slot]
        _conv_fill_window(jnp.minimum(i + 1, n - 1), tiles_per_seq, *u_refs, win_scr)
        pending = iter(_conv_units(tm))

        def conv_some(n_units):
            dep = None
            for _ in range(n_units):
                unit = next(pending, None)
                if unit is not None:
                    dep = _conv_unit(*unit, win_scr, *conv_refs, yb_scr.at[1 - slot])
            return dep

        conv_some(CONV_UNITS_BEFORE_OUT_PROJ)
        yab = jnp.concatenate([ya_ref[...], yb], axis=1)
        xv = xv + jnp.dot(yab, wo_ref[...], preferred_element_type=_F32)
    h = _rmsnorm(xv, g_ref[...])
    h_scr[...] = h.astype(_BF16)
    h_tile = h[:BF16_SUBLANES, :LANES]

    def load_h(dep):
        if dep is not None:
            h_scr[:BF16_SUBLANES, :LANES] = _tied(h_tile, dep).astype(_BF16)
        return h_scr[...]

    c0 = 0
    for cw in GU_CHUNKS:
        dep = conv_some(CONV_UNITS_PER_DOT)
        gu = jnp.dot(load_h(dep), wgu_ref[:, c0:c0 + cw], preferred_element_type=_F32)
        for b0 in range(0, cw, 2 * LANES):
            act = _silu(gu[:, b0:b0 + LANES]) * gu[:, b0 + LANES:b0 + 2 * LANES]
            f0 = (c0 + b0) // 2
            act_scr[:, f0:f0 + LANES] = act.astype(_BF16)
        c0 += cw
    if has_mixer:
        assert next(pending, None) is None
    acc = xv + jnp.dot(act_scr[...], wd_ref[...], preferred_element_type=_F32)
    if has_final:
        acc = _rmsnorm(acc, fg_ref[...])
    o_ref[...] = acc


def _interleave_gate_up(wg, wu):
    d, ff = wg.shape
    blocks = jnp.stack([wg.reshape(d, ff // LANES, LANES), wu.reshape(d, ff // LANES, LANES)], axis=2)
    return blocks.reshape(d, 2 * ff).astype(_BF16)


def _ffn(x2d, g, wgu, wd, mixer=None, final_g=None, tiles_per_seq=None):
    tokens, d = x2d.shape
    tm = TOKEN_TILE
    n_tiles = tokens // tm
    has_mixer = mixer is not None
    has_final = final_g is not None
    assert sum(GU_CHUNKS) == wgu.shape[1]
    assert len(GU_CHUNKS) * CONV_UNITS_PER_DOT + CONV_UNITS_BEFORE_OUT_PROJ == len(_conv_units(tm))
    args = [x2d]
    specs = [pl.BlockSpec((tm, d), lambda i: (i, 0))]
    scratch = [pltpu.VMEM((tm, d), _BF16), pltpu.VMEM((tm, wd.shape[0]), _BF16)]
    if has_mixer:
        ya2d, yb0, u2d, conv_w, conv_b, ln_g, ln_b, w_out = mixer
        hb = tm // CONV_HALO
        n_halo = tokens // CONV_HALO
        nxt = lambda i: jnp.minimum(i + 1, n_tiles - 1)
        args += [ya2d, yb0, u2d, u2d, u2d, conv_w, conv_b, ln_g, ln_b, w_out]
        specs += [
            pl.BlockSpec((tm, A_WIDTH), lambda i: (i, 0)),
            _resident(yb0.shape),
            pl.BlockSpec((tm, B_WIDTH), lambda i: (nxt(i), 0)),
            pl.BlockSpec((CONV_HALO, B_WIDTH), lambda i: (jnp.maximum(nxt(i) * hb - 1, 0), 0)),
            pl.BlockSpec((CONV_HALO, B_WIDTH),
                         lambda i: (jnp.minimum((nxt(i) + 1) * hb, n_halo - 1), 0)),
            _resident(conv_w.shape), _resident(conv_b.shape), _resident(ln_g.shape),
            _resident(ln_b.shape), _resident(w_out.shape),
        ]
        scratch += [pltpu.VMEM((2, tm, B_WIDTH), _BF16)] + _conv_scratch(tm)
    args += [g, wgu, wd]
    specs += [_resident(g.shape), _resident(wgu.shape), _resident(wd.shape)]
    if has_final:
        args.append(final_g)
        specs.append(_resident(final_g.shape))
    return pl.pallas_call(
        functools.partial(_ffn_kernel, has_mixer=has_mixer, has_final=has_final,
                          tiles_per_seq=tiles_per_seq),
        grid=(n_tiles,),
        in_specs=specs,
        out_specs=pl.BlockSpec((tm, d), lambda i: (i, 0)),
        out_shape=jax.ShapeDtypeStruct((tokens, d), _F32),
        scratch_shapes=scratch,
        compiler_params=pltpu.CompilerParams(
            dimension_semantics=("arbitrary",),
            vmem_limit_bytes=V7X_VMEM_LIMIT_BYTES),
        name="ffn_mixer" if has_mixer else "ffn_final",
    )(*args)


def _pool_kernel(x_ref, prev_ref, next_ref, g_ref, pm_ref, ps_ref, o_ref, hp_scr):
    i = pl.program_id(1)
    n_tiles = pl.num_programs(1)
    ts, d = x_ref.shape
    seq = ts * n_tiles
    g = g_ref[...]
    xv = x_ref[...]
    hm = _rmsnorm(xv, g)
    halo_iota = jax.lax.broadcasted_iota(jnp.int32, (POOL_HALO, 1), 0)
    prev_ok = (i * ts - POOL_HALO + halo_iota) >= 0
    next_ok = ((i + 1) * ts + halo_iota) < seq
    hp_scr[0:POOL_HALO, :] = jnp.where(prev_ok, _rmsnorm(prev_ref[...], g), 0.0)
    hp_scr[POOL_HALO:POOL_HALO + ts, :] = hm
    hp_scr[POOL_HALO + ts:, :] = jnp.where(next_ok, _rmsnorm(next_ref[...], g), 0.0)

    pos = i * ts + jax.lax.broadcasted_iota(jnp.int32, (ts, 1), 0)
    gd = d // len(POOL_WINDOWS)
    for gi, w in enumerate(POOL_WINDOWS):
        lanes = slice(gi * gd, (gi + 1) * gd)
        win = None
        for off in range(-(w // 2), w - w // 2):
            sl = hp_scr[POOL_HALO + off:POOL_HALO + off + ts, lanes]
            win = sl if win is None else win + sl
        lo = jnp.clip(pos - w // 2, 0, seq)
        hi = jnp.clip(pos - w // 2 + w, 0, seq)
        cnt = (hi - lo).astype(_F32)
        pg = win / cnt - hm[:, lanes]
        yg = jnp.dot(pg.astype(_BF16), pm_ref[gi], preferred_element_type=_F32)
        o_ref[:, lanes] = xv[:, lanes] + yg * ps_ref[:, lanes]


def _pool(x, g, pool_map, pool_scale):
    bsz, seq, d = x.shape
    ts = TOKEN_TILE
    hb = ts // POOL_HALO
    n_halo_blocks = seq // POOL_HALO
    return pl.pallas_call(
        _pool_kernel,
        grid=(bsz, seq // ts),
        in_specs=[
            pl.BlockSpec((None, ts, d), lambda b, i: (b, i, 0)),
            pl.BlockSpec((None, POOL_HALO, d), lambda b, i: (b, jnp.maximum(i * hb - 1, 0), 0)),
            pl.BlockSpec((None, POOL_HALO, d),
                         lambda b, i: (b, jnp.minimum((i + 1) * hb, n_halo_blocks - 1), 0)),
            _resident(g.shape), _resident(pool_map.shape), _resident(pool_scale.shape),
        ],
        out_specs=pl.BlockSpec((None, ts, d), lambda b, i: (b, i, 0)),
        out_shape=jax.ShapeDtypeStruct((bsz, seq, d), _F32),
        scratch_shapes=[pltpu.VMEM((ts + 2 * POOL_HALO, d), _F32)],
        compiler_params=pltpu.CompilerParams(
            dimension_semantics=("arbitrary", "arbitrary"),
            vmem_limit_bytes=V7X_VMEM_LIMIT_BYTES),
        name="pool_mixer",
    )(x, x, x, g, pool_map, pool_scale)


def kernel(x, norm_mix_g, norm_ffn_g, w_in_ab, fnet_map, conv_w, conv_b, conv_ln_g, conv_ln_b,
           w_out_ab, pool_map, pool_scale, ffn_w_gate, ffn_w_up, ffn_w_down, final_g):
    bsz, seq, d = x.shape
    tokens = bsz * seq
    tiles_per_seq = seq // TOKEN_TILE
    row = lambda v: v.reshape(1, -1)
    bf = lambda v: v.astype(_BF16)

    a_perm, u = _in_proj(x, row(norm_mix_g[0]), bf(w_in_ab[0]))
    ya = _fnet(a_perm, _dft_constants(seq), bf(fnet_map[0]))
    u2d = u.reshape(tokens, B_WIDTH)
    conv_p = (conv_w[0], row(conv_b[0]), row(conv_ln_g[0]), row(conv_ln_b[0]))
    mixer = (ya.reshape(tokens, A_WIDTH), _conv_first(u2d, *conv_p), u2d, *conv_p, bf(w_out_ab[0]))
    x2 = _ffn(x.reshape(tokens, d), row(norm_ffn_g[0]),
              _interleave_gate_up(ffn_w_gate[0], ffn_w_up[0]), bf(ffn_w_down[0]),
              mixer=mixer, tiles_per_seq=tiles_per_seq)

    x3 = _pool(x2.reshape(bsz, seq, d), row(norm_mix_g[1]), bf(pool_map[0]), row(pool_scale[0]))
    out = _ffn(x3.reshape(tokens, d), row(norm_ffn_g[1]),
               _interleave_gate_up(ffn_w_gate[1], ffn_w_up[1]), bf(ffn_w_down[1]),
               final_g=row(final_g))
    return out.reshape(bsz, seq, d)
```

```python
import functools
import math

import jax
import jax.numpy as jnp
import numpy as np
from jax.experimental import pallas as pl
from jax.experimental.pallas import tpu as pltpu

RMS_EPS = 1e-6
LN_EPS = 1e-5

A_HEADS = 4
HEAD_DIM = 128
A_WIDTH = A_HEADS * HEAD_DIM
B_WIDTH = 512
CONV_WIDTH = 31
CONV_PAD = CONV_WIDTH // 2
POOL_WINDOWS = (2, 4, 8, 16)
POOL_HALO = 8

LANES = 128
SUBLANES = 8

DFT_RADIX = 8

V7X_VMEM_LIMIT_BYTES = 56 * 1024 * 1024

TOKEN_TILE = 512
FF_PREP_CHUNK = 256
BFLY_ROW_TILE = 16
CONV_HALO = 16
CONV_ROW_TILE = 64
CONV_UNITS_PER_DOT = 2
BF16_SUBLANES = 16

_F32 = jnp.float32
_BF16 = jnp.bfloat16


def _resident(shape):
    nd = len(shape)
    return pl.BlockSpec(shape, lambda *_: (0,) * nd, pipeline_mode=pl.Buffered(1))


def _rmsnorm(xv, g):
    inv = jax.lax.rsqrt(jnp.mean(xv * xv, axis=-1, keepdims=True) + RMS_EPS)
    return (xv * inv) * g


def _silu(v):
    return v * jax.nn.sigmoid(v)


def _in_proj_kernel(x_ref, g_ref, w_ref, a_ref, u_ref, a_scr):
    h = _rmsnorm(x_ref[...], g_ref[...]).astype(_BF16)
    p = jnp.dot(h, w_ref[...], preferred_element_type=_F32)
    rows = a_scr.shape[1] // DFT_RADIX
    for lt in range(A_WIDTH // LANES):
        lanes = slice(lt * LANES, (lt + 1) * LANES)
        a_scr[lt] = p[:, lanes]
        for jr in range(DFT_RADIX):
            a_ref[jr, :, lanes] = a_scr[lt, pl.ds(jr, rows, stride=DFT_RADIX), :].astype(_BF16)
    v = p[:, A_WIDTH:A_WIDTH + B_WIDTH]
    gate = p[:, A_WIDTH + B_WIDTH:]
    u_ref[...] = v * jax.nn.sigmoid(gate)


def _in_proj(x, g, w_in):
    bsz, seq, d = x.shape
    tm = TOKEN_TILE
    inner = seq // DFT_RADIX
    return pl.pallas_call(
        _in_proj_kernel,
        grid=(bsz, seq // tm),
        in_specs=[
            pl.BlockSpec((None, tm, d), lambda b, i: (b, i, 0)),
            _resident((1, d)),
            _resident(w_in.shape),
        ],
        out_specs=[
            pl.BlockSpec((None, DFT_RADIX, tm // DFT_RADIX, A_WIDTH), lambda b, i: (b, 0, i, 0)),
            pl.BlockSpec((None, tm, B_WIDTH), lambda b, i: (b, i, 0)),
        ],
        out_shape=[
            jax.ShapeDtypeStruct((bsz, DFT_RADIX, inner, A_WIDTH), _BF16),
            jax.ShapeDtypeStruct((bsz, seq, B_WIDTH), _F32),
        ],
        scratch_shapes=[pltpu.VMEM((A_WIDTH // LANES, tm, LANES), _F32)],
        compiler_params=pltpu.CompilerParams(
            dimension_semantics=("arbitrary", "arbitrary"),
            vmem_limit_bytes=V7X_VMEM_LIMIT_BYTES),
        name="in_proj",
    )(x, g, w_in)


def _cadd(a, b):
    return (a[0] + b[0], a[1] + b[1])


def _csub(a, b):
    return (a[0] - b[0], a[1] - b[1])


def _dft4(a0, a1, a2, a3):
    s0, s1 = _cadd(a0, a2), _csub(a0, a2)
    s2, s3 = _cadd(a1, a3), _csub(a1, a3)
    return (_cadd(s0, s2), (s1[0] + s3[1], s1[1] - s3[0]),
            _csub(s0, s2), (s1[0] - s3[1], s1[1] + s3[0]))


def _mul_w8(k, z):
    r, i = z
    h = math.sqrt(0.5)
    if k == 0:
        return z
    if k == 1:
        return (h * (r + i), h * (i - r))
    if k == 2:
        return (i, -r)
    return (h * (i - r), -h * (r + i))


def _fnet_kernel(a_ref, cs_ref, twc_ref, tws_ref, cdsd_ref, map_ref, y_ref, yr_scr, yi_scr):
    seq = y_ref.shape[0]
    inner = seq // DFT_RADIX

    for jr in range(DFT_RADIX):
        yy = jnp.dot(cs_ref[...], a_ref[jr], preferred_element_type=_F32)
        yr_scr[jr * inner:(jr + 1) * inner, :] = yy[:inner]
        yi_scr[jr * inner:(jr + 1) * inner, :] = yy[inner:]

    rt = BFLY_ROW_TILE

    def bfly(c, carry):
        r0 = pl.multiple_of(c * rt, rt)
        for lt in range(A_WIDTH // LANES):
            lanes = slice(lt * LANES, (lt + 1) * LANES)
            z = []
            for jr in range(DFT_RADIX):
                rows = pl.ds(jr * inner + r0, rt)
                yr = yr_scr[rows, lanes]
                yi = yi_scr[rows, lanes]
                if jr == 0:
                    z.append((yr, yi))
                else:
                    tc = twc_ref[rows, :]
                    ts = tws_ref[rows, :]
                    z.append((yr * tc + yi * ts, yi * tc - yr * ts))
            ev = _dft4(z[0], z[2], z[4], z[6])
            od = _dft4(z[1], z[3], z[5], z[7])
            for k in range(4):
                w = _mul_w8(k, od[k])
                lo = _cadd(ev[k], w)
                hi = _csub(ev[k], w)
                rows_lo = pl.ds(k * inner + r0, rt)
                rows_hi = pl.ds((k + 4) * inner + r0, rt)
                yr_scr[rows_lo, lanes] = lo[0]
                yi_scr[rows_lo, lanes] = lo[1]
                yr_scr[rows_hi, lanes] = hi[0]
                yi_scr[rows_hi, lanes] = hi[1]
        return carry

    jax.lax.fori_loop(0, inner // rt, bfly, 0)

    for hd in range(A_HEADS):
        lanes = slice(hd * HEAD_DIM, (hd + 1) * HEAD_DIM)
        lhs = jnp.concatenate([yr_scr[:, lanes].astype(_BF16),
                               yi_scr[:, lanes].astype(_BF16)], axis=1)
        f = jnp.dot(lhs, cdsd_ref[...], preferred_element_type=_F32)
        ya = jnp.dot(f.astype(_BF16), map_ref[hd], preferred_element_type=_F32)
        y_ref[:, lanes] = ya.astype(_BF16)


def _fnet(a_perm, consts, fmap):
    bsz, _, inner, _ = a_perm.shape
    seq = inner * DFT_RADIX
    cs, twc, tws, cdsd = consts
    return pl.pallas_call(
        _fnet_kernel,
        grid=(bsz,),
        in_specs=[
            pl.BlockSpec((None, DFT_RADIX, inner, A_WIDTH), lambda b: (b, 0, 0, 0)),
            _resident(cs.shape), _resident(twc.shape), _resident(tws.shape),
            _resident(cdsd.shape), _resident(fmap.shape),
        ],
        out_specs=pl.BlockSpec((None, seq, A_WIDTH), lambda b: (b, 0, 0)),
        out_shape=jax.ShapeDtypeStruct((bsz, seq, A_WIDTH), _BF16),
        scratch_shapes=[
            pltpu.VMEM((seq, A_WIDTH), _F32),
            pltpu.VMEM((seq, A_WIDTH), _F32),
        ],
        compiler_params=pltpu.CompilerParams(
            dimension_semantics=("arbitrary",),
            vmem_limit_bytes=V7X_VMEM_LIMIT_BYTES),
        name="fnet",
    )(a_perm, cs, twc, tws, cdsd, fmap)


def _dft_constants(seq):
    inner = seq // DFT_RADIX
    k = np.arange(inner, dtype=np.float64)
    ang = 2.0 * np.pi * np.outer(k, k) / inner
    cs = np.concatenate([np.cos(ang), -np.sin(ang)], axis=0)
    jr = np.arange(DFT_RADIX, dtype=np.float64)[:, None]
    tw = 2.0 * np.pi * (jr * k[None, :]) / seq
    twc = np.repeat(np.cos(tw).reshape(seq, 1), LANES, axis=1)
    tws = np.repeat(np.sin(tw).reshape(seq, 1), LANES, axis=1)
    d = np.arange(HEAD_DIM, dtype=np.float64)
    angd = 2.0 * np.pi * np.outer(d, d) / HEAD_DIM
    scale = 1.0 / math.sqrt(seq * HEAD_DIM)
    cdsd = np.concatenate([np.cos(angd), np.sin(angd)], axis=0) * scale
    return (jnp.asarray(cs, _F32).astype(_BF16), jnp.asarray(twc, _F32), jnp.asarray(tws, _F32),
            jnp.asarray(cdsd, _F32).astype(_BF16))


def _conv_fill_window(tile, tiles_per_seq, main_ref, prev_ref, next_ref, win_scr):
    tm = main_ref.shape[0]
    halo = CONV_HALO
    pos = jnp.zeros((halo, 1), jnp.int32) + tile % tiles_per_seq
    win_scr[0:halo, :] = jnp.where(pos == 0, 0.0, prev_ref[...])
    win_scr[halo:halo + tm, :] = main_ref[...]
    win_scr[halo + tm:, :] = jnp.where(pos == tiles_per_seq - 1, 0.0, next_ref[...])


def _conv_unit(rc, lt, win_scr, cw_ref, cb_ref, lg_ref, lb_ref, out_ref):
    ct = CONV_ROW_TILE
    halo = CONV_HALO
    r0 = rc * ct
    lanes = slice(lt * LANES, (lt + 1) * LANES)
    first = halo - CONV_PAD
    span = ct + 2 * halo
    win = win_scr[r0:r0 + span, lanes]
    acc = None
    for s in range(SUBLANES):
        rot = win if s == 0 else pltpu.roll(win, span - s, axis=0)
        for q in range((first + CONV_WIDTH - 1) // SUBLANES + 1):
            k = SUBLANES * q + s - first
            if 0 <= k < CONV_WIDTH:
                term = rot[SUBLANES * q:SUBLANES * q + ct] * cw_ref[k:k + 1, lanes]
                acc = term if acc is None else acc + term
    cv = acc + cb_ref[:, lanes]
    mu = jnp.mean(cv, axis=-1, keepdims=True)
    dv = cv - mu
    var = jnp.mean(dv * dv, axis=-1, keepdims=True)
    yn = (dv * jax.lax.rsqrt(var + LN_EPS)) * lg_ref[:, lanes] + lb_ref[:, lanes]
    y = _silu(yn)
    out_ref[r0:r0 + ct, lanes] = y.astype(_BF16)
    return y[ct - SUBLANES:, :]


def _tied(tile, dep):
    zero = pltpu.bitcast(dep, jnp.uint32)
    zero = jax.lax.shift_right_logical(jax.lax.shift_right_logical(zero, jnp.uint32(16)), jnp.uint32(16))
    zero = pltpu.bitcast(zero, _F32)
    return tile + jnp.concatenate([zero, zero], axis=0)


def _conv_units(tm):
    return [(rc, lt) for rc in range(tm // CONV_ROW_TILE) for lt in range(B_WIDTH // LANES)]


def _conv_scratch(tm):
    return [pltpu.VMEM((tm + 2 * CONV_HALO, B_WIDTH), _F32)]


def _conv_first_kernel(main_ref, next_ref, cw_ref, cb_ref, lg_ref, lb_ref, out_ref, win_scr):
    _conv_fill_window(0, 2, main_ref, next_ref, next_ref, win_scr)
    for rc, lt in _conv_units(main_ref.shape[0]):
        _conv_unit(rc, lt, win_scr, cw_ref, cb_ref, lg_ref, lb_ref, out_ref)


def _conv_first(u2d, conv_w, conv_b, ln_g, ln_b):
    tm = TOKEN_TILE
    return pl.pallas_call(
        _conv_first_kernel,
        grid=(1,),
        in_specs=[
            pl.BlockSpec((tm, B_WIDTH), lambda i: (0, 0)),
            pl.BlockSpec((CONV_HALO, B_WIDTH), lambda i: (tm // CONV_HALO, 0)),
            _resident(conv_w.shape), _resident(conv_b.shape), _resident(ln_g.shape),
            _resident(ln_b.shape),
        ],
        out_specs=pl.BlockSpec((tm, B_WIDTH), lambda i: (0, 0)),
        out_shape=jax.ShapeDtypeStruct((tm, B_WIDTH), _BF16),
        scratch_shapes=_conv_scratch(tm),
        compiler_params=pltpu.CompilerParams(
            dimension_semantics=("arbitrary",),
            vmem_limit_bytes=V7X_VMEM_LIMIT_BYTES),
        name="conv_first",
    )(u2d, u2d, conv_w, conv_b, ln_g, ln_b)


def _prep_ffn_weights(j, wg_ref, wu_ref, wd_ref, wgu_scr, wd_scr):
    cw = wg_ref.shape[1]
    for b in range(cw // LANES):
        src = slice(b * LANES, (b + 1) * LANES)
        wgu_scr[j, :, 2 * b * LANES:(2 * b + 1) * LANES] = wg_ref[:, src].astype(_BF16)
        wgu_scr[j, :, (2 * b + 1) * LANES:(2 * b + 2) * LANES] = wu_ref[:, src].astype(_BF16)
    wd_scr[pl.ds(pl.multiple_of(j * cw, cw), cw), :] = wd_ref[...].astype(_BF16)


def _ffn_kernel(*refs, has_mixer, has_final, tiles_per_seq, n_prep):
    it = iter(refs)
    x_ref = next(it)
    if has_mixer:
        ya_ref = next(it)
        yb0_ref = next(it)
        u_refs = (next(it), next(it), next(it))
        conv_refs = (next(it), next(it), next(it), next(it))
        wo_ref = next(it)
    g_ref = next(it)
    wg_ref = next(it)
    wu_ref = next(it)
    wd_ref = next(it)
    if has_final:
        fg_ref = next(it)
    o_ref = next(it)
    wgu_scr = next(it)
    wd_scr = next(it)
    h_scr = next(it)
    act_scr = next(it)
    if has_mixer:
        yb_scr = next(it)
        win_scr = next(it)

    step = pl.program_id(0)

    @pl.when(step < n_prep)
    def _():
        _prep_ffn_weights(step, wg_ref, wu_ref, wd_ref, wgu_scr, wd_scr)

    @pl.when(step >= n_prep)
    def _():
        tm = x_ref.shape[0]
        xv = x_ref[...]
        conv_some = lambda n_units: None
        if has_mixer:
            i = step - n_prep
            n = pl.num_programs(0) - n_prep
            slot = i % 2

            @pl.when(i == 0)
            def _():
                yb_scr[0] = yb0_ref[...]

            yb = yb_scr[slot]
            _conv_fill_window(jnp.minimum(i + 1, n - 1), tiles_per_seq, *u_refs, win_scr)
            pending = iter(_conv_units(tm))

            def conv_some(n_units):
                dep = None
                for _ in range(n_units):
                    unit = next(pending, None)
                    if unit is not None:
                        dep = _conv_unit(*unit, win_scr, *conv_refs, yb_scr.at[1 - slot])
                return dep

            conv_some(len(_conv_units(tm)) - n_prep * CONV_UNITS_PER_DOT)
            yab = jnp.concatenate([ya_ref[...], yb], axis=1)
            xv = xv + jnp.dot(yab, wo_ref[...], preferred_element_type=_F32)
        h = _rmsnorm(xv, g_ref[...])
        h_scr[...] = h.astype(_BF16)
        h_tile = h[:BF16_SUBLANES, :LANES]

        def load_h(dep):
            if dep is not None:
                h_scr[:BF16_SUBLANES, :LANES] = _tied(h_tile, dep).astype(_BF16)
            return h_scr[...]

        gu_width = wgu_scr.shape[2]
        for c in range(n_prep):
            dep = conv_some(CONV_UNITS_PER_DOT)
            gu = jnp.dot(load_h(dep), wgu_scr[c], preferred_element_type=_F32)
            for b0 in range(0, gu_width, 2 * LANES):
                act = _silu(gu[:, b0:b0 + LANES]) * gu[:, b0 + LANES:b0 + 2 * LANES]
                f0 = (c * gu_width + b0) // 2
                act_scr[:, f0:f0 + LANES] = act.astype(_BF16)
        acc = xv + jnp.dot(act_scr[...], wd_scr[...], preferred_element_type=_F32)
        if has_final:
            acc = _rmsnorm(acc, fg_ref[...])
        o_ref[...] = acc


def _ffn(x2d, g, wg_all, wu_all, wd_all, layer, mixer=None, final_g=None, tiles_per_seq=None):
    tokens, d = x2d.shape
    ff = wg_all.shape[2]
    tm = TOKEN_TILE
    n_tiles = tokens // tm
    n_prep = ff // FF_PREP_CHUNK
    last_prep = n_prep - 1
    has_mixer = mixer is not None
    has_final = final_g is not None
    tile = lambda i: jnp.maximum(i - n_prep, 0)
    args = [x2d]
    specs = [pl.BlockSpec((tm, d), lambda i: (tile(i), 0))]
    scratch = [pltpu.VMEM((n_prep, d, 2 * FF_PREP_CHUNK), _BF16), pltpu.VMEM((ff, d), _BF16),
               pltpu.VMEM((tm, d), _BF16), pltpu.VMEM((tm, ff), _BF16)]
    if has_mixer:
        ya2d, yb0, u2d, conv_w, conv_b, ln_g, ln_b, w_out = mixer
        hb = tm // CONV_HALO
        n_halo = tokens // CONV_HALO
        nxt = lambda i: jnp.minimum(tile(i) + 1, n_tiles - 1)
        args += [ya2d, yb0, u2d, u2d, u2d, conv_w, conv_b, ln_g, ln_b, w_out]
        specs += [
            pl.BlockSpec((tm, A_WIDTH), lambda i: (tile(i), 0)),
            _resident(yb0.shape),
            pl.BlockSpec((tm, B_WIDTH), lambda i: (nxt(i), 0)),
            pl.BlockSpec((CONV_HALO, B_WIDTH), lambda i: (jnp.maximum(nxt(i) * hb - 1, 0), 0)),
            pl.BlockSpec((CONV_HALO, B_WIDTH),
                         lambda i: (jnp.minimum((nxt(i) + 1) * hb, n_halo - 1), 0)),
            _resident(conv_w.shape), _resident(conv_b.shape), _resident(ln_g.shape),
            _resident(ln_b.shape), _resident(w_out.shape),
        ]
        scratch += [pltpu.VMEM((2, tm, B_WIDTH), _BF16)] + _conv_scratch(tm)
    args += [g, wg_all, wu_all, wd_all]
    specs += [
        _resident(g.shape),
        pl.BlockSpec((None, d, FF_PREP_CHUNK), lambda i: (layer, 0, jnp.minimum(i, last_prep))),
        pl.BlockSpec((None, d, FF_PREP_CHUNK), lambda i: (layer, 0, jnp.minimum(i, last_prep))),
        pl.BlockSpec((None, FF_PREP_CHUNK, d), lambda i: (layer, jnp.minimum(i, last_prep), 0)),
    ]
    if has_final:
        args.append(final_g)
        specs.append(_resident(final_g.shape))
    return pl.pallas_call(
        functools.partial(_ffn_kernel, has_mixer=has_mixer, has_final=has_final,
                          tiles_per_seq=tiles_per_seq, n_prep=n_prep),
        grid=(n_prep + n_tiles,),
        in_specs=specs,
        out_specs=pl.BlockSpec((tm, d), lambda i: (tile(i), 0)),
        out_shape=jax.ShapeDtypeStruct((tokens, d), _F32),
        scratch_shapes=scratch,
        compiler_params=pltpu.CompilerParams(
            dimension_semantics=("arbitrary",),
            vmem_limit_bytes=V7X_VMEM_LIMIT_BYTES),
        name="ffn_mixer" if has_mixer else "ffn_final",
    )(*args)


def _pool_kernel(x_ref, prev_ref, next_ref, g_ref, pm_ref, ps_ref, o_ref, hp_scr):
    i = pl.program_id(1)
    n_tiles = pl.num_programs(1)
    ts, d = x_ref.shape
    seq = ts * n_tiles
    g = g_ref[...]
    xv = x_ref[...]
    hm = _rmsnorm(xv, g)
    halo_iota = jax.lax.broadcasted_iota(jnp.int32, (POOL_HALO, 1), 0)
    prev_ok = (i * ts - POOL_HALO + halo_iota) >= 0
    next_ok = ((i + 1) * ts + halo_iota) < seq
    hp_scr[0:POOL_HALO, :] = jnp.where(prev_ok, _rmsnorm(prev_ref[...], g), 0.0)
    hp_scr[POOL_HALO:POOL_HALO + ts, :] = hm
    hp_scr[POOL_HALO + ts:, :] = jnp.where(next_ok, _rmsnorm(next_ref[...], g), 0.0)

    pos = i * ts + jax.lax.broadcasted_iota(jnp.int32, (ts, 1), 0)
    gd = d // len(POOL_WINDOWS)
    for gi, w in enumerate(POOL_WINDOWS):
        lanes = slice(gi * gd, (gi + 1) * gd)
        win = None
        for off in range(-(w // 2), w - w // 2):
            sl = hp_scr[POOL_HALO + off:POOL_HALO + off + ts, lanes]
            win = sl if win is None else win + sl
        lo = jnp.clip(pos - w // 2, 0, seq)
        hi = jnp.clip(pos - w // 2 + w, 0, seq)
        cnt = (hi - lo).astype(_F32)
        pg = win / cnt - hm[:, lanes]
        yg = jnp.dot(pg.astype(_BF16), pm_ref[gi], preferred_element_type=_F32)
        o_ref[:, lanes] = xv[:, lanes] + yg * ps_ref[:, lanes]


def _pool(x, g, pool_map, pool_scale):
    bsz, seq, d = x.shape
    ts = TOKEN_TILE
    hb = ts // POOL_HALO
    n_halo_blocks = seq // POOL_HALO
    return pl.pallas_call(
        _pool_kernel,
        grid=(bsz, seq // ts),
        in_specs=[
            pl.BlockSpec((None, ts, d), lambda b, i: (b, i, 0)),
            pl.BlockSpec((None, POOL_HALO, d), lambda b, i: (b, jnp.maximum(i * hb - 1, 0), 0)),
            pl.BlockSpec((None, POOL_HALO, d),
                         lambda b, i: (b, jnp.minimum((i + 1) * hb, n_halo_blocks - 1), 0)),
            _resident(g.shape), _resident(pool_map.shape), _resident(pool_scale.shape),
        ],
        out_specs=pl.BlockSpec((None, ts, d), lambda b, i: (b, i, 0)),
        out_shape=jax.ShapeDtypeStruct((bsz, seq, d), _F32),
        scratch_shapes=[pltpu.VMEM((ts + 2 * POOL_HALO, d), _F32)],
        compiler_params=pltpu.CompilerParams(
            dimension_semantics=("arbitrary", "arbitrary"),
            vmem_limit_bytes=V7X_VMEM_LIMIT_BYTES),
        name="pool_mixer",
    )(x, x, x, g, pool_map, pool_scale)


def kernel(x, norm_mix_g, norm_ffn_g, w_in_ab, fnet_map, conv_w, conv_b, conv_ln_g, conv_ln_b,
           w_out_ab, pool_map, pool_scale, ffn_w_gate, ffn_w_up, ffn_w_down, final_g):
    bsz, seq, d = x.shape
    tokens = bsz * seq
    tiles_per_seq = seq // TOKEN_TILE
    row = lambda v: v.reshape(1, -1)
    bf = lambda v: v.astype(_BF16)

    a_perm, u = _in_proj(x, row(norm_mix_g[0]), bf(w_in_ab[0]))
    ya = _fnet(a_perm, _dft_constants(seq), bf(fnet_map[0]))
    u2d = u.reshape(tokens, B_WIDTH)
    conv_p = (conv_w[0], row(conv_b[0]), row(conv_ln_g[0]), row(conv_ln_b[0]))
    mixer = (ya.reshape(tokens, A_WIDTH), _conv_first(u2d, *conv_p), u2d, *conv_p, bf(w_out_ab[0]))
    x2 = _ffn(x.reshape(tokens, d), row(norm_ffn_g[0]), ffn_w_gate, ffn_w_up, ffn_w_down, 0,
              mixer=mixer, tiles_per_seq=tiles_per_seq)

    x3 = _pool(x2.reshape(bsz, seq, d), row(norm_mix_g[1]), bf(pool_map[0]), row(pool_scale[0]))
    out = _ffn(x3.reshape(tokens, d), row(norm_ffn_g[1]), ffn_w_gate, ffn_w_up, ffn_w_down, 1,
               final_g=row(final_g))
    return out.reshape(bsz, seq, d)
```

```python
import functools
import math

import jax
import jax.numpy as jnp
import numpy as np
from jax.experimental import pallas as pl
from jax.experimental.pallas import tpu as pltpu

RMS_EPS = 1e-6
LN_EPS = 1e-5

A_HEADS = 4
HEAD_DIM = 128
A_WIDTH = A_HEADS * HEAD_DIM
B_WIDTH = 512
CONV_WIDTH = 31
CONV_PAD = CONV_WIDTH // 2
POOL_WINDOWS = (2, 4, 8, 16)
POOL_HALO = 8

LANES = 128
SUBLANES = 8

DFT_RADIX = 8

V7X_VMEM_LIMIT_BYTES = 56 * 1024 * 1024

TOKEN_TILE = 512
IN_PROJ_TILE = 1024
FF_PREP_CHUNK = 256
BFLY_ROW_TILE = 16
CONV_HALO = 16
CONV_ROW_TILE = 64
CONV_UNITS_PER_DOT = 2
BF16_SUBLANES = 16

_F32 = jnp.float32
_BF16 = jnp.bfloat16


def _resident(shape):
    nd = len(shape)
    return pl.BlockSpec(shape, lambda *_: (0,) * nd, pipeline_mode=pl.Buffered(1))


def _rmsnorm(xv, g):
    inv = jax.lax.rsqrt(jnp.mean(xv * xv, axis=-1, keepdims=True) + RMS_EPS)
    return (xv * inv) * g


def _silu(v):
    return v * jax.nn.sigmoid(v)


def _in_proj_kernel(x_ref, g_ref, w_ref, a_ref, u_ref, a_scr):
    h = _rmsnorm(x_ref[...], g_ref[...]).astype(_BF16)
    p = jnp.dot(h, w_ref[...], preferred_element_type=_F32)
    rows = a_scr.shape[1] // DFT_RADIX
    for lt in range(A_WIDTH // LANES):
        lanes = slice(lt * LANES, (lt + 1) * LANES)
        a_scr[lt] = p[:, lanes]
        for jr in range(DFT_RADIX):
            a_ref[jr, :, lanes] = a_scr[lt, pl.ds(jr, rows, stride=DFT_RADIX), :].astype(_BF16)
    v = p[:, A_WIDTH:A_WIDTH + B_WIDTH]
    gate = p[:, A_WIDTH + B_WIDTH:]
    u_ref[...] = v * jax.nn.sigmoid(gate)


def _in_proj(x, g, w_in):
    bsz, seq, d = x.shape
    tm = IN_PROJ_TILE
    inner = seq // DFT_RADIX
    return pl.pallas_call(
        _in_proj_kernel,
        grid=(bsz, seq // tm),
        in_specs=[
            pl.BlockSpec((None, tm, d), lambda b, i: (b, i, 0)),
            _resident((1, d)),
            _resident(w_in.shape),
        ],
        out_specs=[
            pl.BlockSpec((None, DFT_RADIX, tm // DFT_RADIX, A_WIDTH), lambda b, i: (b, 0, i, 0)),
            pl.BlockSpec((None, tm, B_WIDTH), lambda b, i: (b, i, 0)),
        ],
        out_shape=[
            jax.ShapeDtypeStruct((bsz, DFT_RADIX, inner, A_WIDTH), _BF16),
            jax.ShapeDtypeStruct((bsz, seq, B_WIDTH), _F32),
        ],
        scratch_shapes=[pltpu.VMEM((A_WIDTH // LANES, tm, LANES), _F32)],
        compiler_params=pltpu.CompilerParams(
            dimension_semantics=("arbitrary", "arbitrary"),
            vmem_limit_bytes=V7X_VMEM_LIMIT_BYTES),
        name="in_proj",
    )(x, g, w_in)


def _cadd(a, b):
    return (a[0] + b[0], a[1] + b[1])


def _csub(a, b):
    return (a[0] - b[0], a[1] - b[1])


def _dft4(a0, a1, a2, a3):
    s0, s1 = _cadd(a0, a2), _csub(a0, a2)
    s2, s3 = _cadd(a1, a3), _csub(a1, a3)
    return (_cadd(s0, s2), (s1[0] + s3[1], s1[1] - s3[0]),
            _csub(s0, s2), (s1[0] - s3[1], s1[1] + s3[0]))


def _mul_w8(k, z):
    r, i = z
    h = math.sqrt(0.5)
    if k == 0:
        return z
    if k == 1:
        return (h * (r + i), h * (i - r))
    if k == 2:
        return (i, -r)
    return (h * (i - r), -h * (r + i))


def _fnet_kernel(a_ref, cs_ref, twc_ref, tws_ref, cdsd_ref, map_ref, y_ref, yr_scr, yi_scr):
    seq = y_ref.shape[0]
    inner = seq // DFT_RADIX

    for jr in range(DFT_RADIX):
        yy = jnp.dot(cs_ref[...], a_ref[jr], preferred_element_type=_F32)
        yr_scr[jr * inner:(jr + 1) * inner, :] = yy[:inner]
        yi_scr[jr * inner:(jr + 1) * inner, :] = yy[inner:]

    rt = BFLY_ROW_TILE

    def bfly(c, carry):
        r0 = pl.multiple_of(c * rt, rt)
        for lt in range(A_WIDTH // LANES):
            lanes = slice(lt * LANES, (lt + 1) * LANES)
            z = []
            for jr in range(DFT_RADIX):
                rows = pl.ds(jr * inner + r0, rt)
                yr = yr_scr[rows, lanes]
                yi = yi_scr[rows, lanes]
                if jr == 0:
                    z.append((yr, yi))
                else:
                    tc = twc_ref[rows, :]
                    ts = tws_ref[rows, :]
                    z.append((yr * tc + yi * ts, yi * tc - yr * ts))
            ev = _dft4(z[0], z[2], z[4], z[6])
            od = _dft4(z[1], z[3], z[5], z[7])
            for k in range(4):
                w = _mul_w8(k, od[k])
                lo = _cadd(ev[k], w)
                hi = _csub(ev[k], w)
                rows_lo = pl.ds(k * inner + r0, rt)
                rows_hi = pl.ds((k + 4) * inner + r0, rt)
                yr_scr[rows_lo, lanes] = lo[0]
                yi_scr[rows_lo, lanes] = lo[1]
                yr_scr[rows_hi, lanes] = hi[0]
                yi_scr[rows_hi, lanes] = hi[1]
        return carry

    jax.lax.fori_loop(0, inner // rt, bfly, 0)

    for hd in range(A_HEADS):
        lanes = slice(hd * HEAD_DIM, (hd + 1) * HEAD_DIM)
        lhs = jnp.concatenate([yr_scr[:, lanes].astype(_BF16),
                               yi_scr[:, lanes].astype(_BF16)], axis=1)
        f = jnp.dot(lhs, cdsd_ref[...], preferred_element_type=_F32)
        ya = jnp.dot(f.astype(_BF16), map_ref[hd], preferred_element_type=_F32)
        y_ref[:, lanes] = ya.astype(_BF16)


def _fnet(a_perm, consts, fmap):
    bsz, _, inner, _ = a_perm.shape
    seq = inner * DFT_RADIX
    cs, twc, tws, cdsd = consts
    return pl.pallas_call(
        _fnet_kernel,
        grid=(bsz,),
        in_specs=[
            pl.BlockSpec((None, DFT_RADIX, inner, A_WIDTH), lambda b: (b, 0, 0, 0)),
            _resident(cs.shape), _resident(twc.shape), _resident(tws.shape),
            _resident(cdsd.shape), _resident(fmap.shape),
        ],
        out_specs=pl.BlockSpec((None, seq, A_WIDTH), lambda b: (b, 0, 0)),
        out_shape=jax.ShapeDtypeStruct((bsz, seq, A_WIDTH), _BF16),
        scratch_shapes=[
            pltpu.VMEM((seq, A_WIDTH), _F32),
            pltpu.VMEM((seq, A_WIDTH), _F32),
        ],
        compiler_params=pltpu.CompilerParams(
            dimension_semantics=("arbitrary",),
            vmem_limit_bytes=V7X_VMEM_LIMIT_BYTES),
        name="fnet",
    )(a_perm, cs, twc, tws, cdsd, fmap)


def _dft_constants(seq):
    inner = seq // DFT_RADIX
    k = np.arange(inner, dtype=np.float64)
    ang = 2.0 * np.pi * np.outer(k, k) / inner
    cs = np.concatenate([np.cos(ang), -np.sin(ang)], axis=0)
    jr = np.arange(DFT_RADIX, dtype=np.float64)[:, None]
    tw = 2.0 * np.pi * (jr * k[None, :]) / seq
    twc = np.repeat(np.cos(tw).reshape(seq, 1), LANES, axis=1)
    tws = np.repeat(np.sin(tw).reshape(seq, 1), LANES, axis=1)
    d = np.arange(HEAD_DIM, dtype=np.float64)
    angd = 2.0 * np.pi * np.outer(d, d) / HEAD_DIM
    scale = 1.0 / math.sqrt(seq * HEAD_DIM)
    cdsd = np.concatenate([np.cos(angd), np.sin(angd)], axis=0) * scale
    return (jnp.asarray(cs, _F32).astype(_BF16), jnp.asarray(twc, _F32), jnp.asarray(tws, _F32),
            jnp.asarray(cdsd, _F32).astype(_BF16))


def _conv_fill_window(tile, tiles_per_seq, main_ref, prev_ref, next_ref, win_scr):
    tm = main_ref.shape[0]
    halo = CONV_HALO
    pos = jnp.zeros((halo, 1), jnp.int32) + tile % tiles_per_seq
    win_scr[0:halo, :] = jnp.where(pos == 0, 0.0, prev_ref[...])
    win_scr[halo:halo + tm, :] = main_ref[...]
    win_scr[halo + tm:, :] = jnp.where(pos == tiles_per_seq - 1, 0.0, next_ref[...])


def _conv_unit(rc, lt, win_scr, cw_ref, cb_ref, lg_ref, lb_ref, out_ref):
    ct = CONV_ROW_TILE
    halo = CONV_HALO
    r0 = rc * ct
    lanes = slice(lt * LANES, (lt + 1) * LANES)
    first = halo - CONV_PAD
    span = ct + 2 * halo
    win = win_scr[r0:r0 + span, lanes]
    acc = None
    for s in range(SUBLANES):
        rot = win if s == 0 else pltpu.roll(win, span - s, axis=0)
        for q in range((first + CONV_WIDTH - 1) // SUBLANES + 1):
            k = SUBLANES * q + s - first
            if 0 <= k < CONV_WIDTH:
                term = rot[SUBLANES * q:SUBLANES * q + ct] * cw_ref[k:k + 1, lanes]
                acc = term if acc is None else acc + term
    cv = acc + cb_ref[:, lanes]
    mu = jnp.mean(cv, axis=-1, keepdims=True)
    dv = cv - mu
    var = jnp.mean(dv * dv, axis=-1, keepdims=True)
    yn = (dv * jax.lax.rsqrt(var + LN_EPS)) * lg_ref[:, lanes] + lb_ref[:, lanes]
    y = _silu(yn)
    out_ref[r0:r0 + ct, lanes] = y.astype(_BF16)
    return y[ct - SUBLANES:, :]


def _tied(tile, dep):
    zero = pltpu.bitcast(dep, jnp.uint32)
    zero = jax.lax.shift_right_logical(jax.lax.shift_right_logical(zero, jnp.uint32(16)), jnp.uint32(16))
    zero = pltpu.bitcast(zero, _F32)
    return tile + jnp.concatenate([zero, zero], axis=0)


def _conv_units(tm):
    return [(rc, lt) for rc in range(tm // CONV_ROW_TILE) for lt in range(B_WIDTH // LANES)]


def _conv_scratch(tm):
    return [pltpu.VMEM((tm + 2 * CONV_HALO, B_WIDTH), _F32)]


def _conv_first_kernel(main_ref, next_ref, cw_ref, cb_ref, lg_ref, lb_ref, out_ref, win_scr):
    _conv_fill_window(0, 2, main_ref, next_ref, next_ref, win_scr)
    for rc, lt in _conv_units(main_ref.shape[0]):
        _conv_unit(rc, lt, win_scr, cw_ref, cb_ref, lg_ref, lb_ref, out_ref)


def _conv_first(u2d, conv_w, conv_b, ln_g, ln_b):
    tm = TOKEN_TILE
    return pl.pallas_call(
        _conv_first_kernel,
        grid=(1,),
        in_specs=[
            pl.BlockSpec((tm, B_WIDTH), lambda i: (0, 0)),
            pl.BlockSpec((CONV_HALO, B_WIDTH), lambda i: (tm // CONV_HALO, 0)),
            _resident(conv_w.shape), _resident(conv_b.shape), _resident(ln_g.shape),
            _resident(ln_b.shape),
        ],
        out_specs=pl.BlockSpec((tm, B_WIDTH), lambda i: (0, 0)),
        out_shape=jax.ShapeDtypeStruct((tm, B_WIDTH), _BF16),
        scratch_shapes=_conv_scratch(tm),
        compiler_params=pltpu.CompilerParams(
            dimension_semantics=("arbitrary",),
            vmem_limit_bytes=V7X_VMEM_LIMIT_BYTES),
        name="conv_first",
    )(u2d, u2d, conv_w, conv_b, ln_g, ln_b)


def _prep_ffn_weights(j, wg_ref, wu_ref, wd_ref, wgu_scr, wd_scr):
    cw = wg_ref.shape[1]
    for b in range(cw // LANES):
        src = slice(b * LANES, (b + 1) * LANES)
        wgu_scr[j, :, 2 * b * LANES:(2 * b + 1) * LANES] = wg_ref[:, src].astype(_BF16)
        wgu_scr[j, :, (2 * b + 1) * LANES:(2 * b + 2) * LANES] = wu_ref[:, src].astype(_BF16)
    wd_scr[pl.ds(pl.multiple_of(j * cw, cw), cw), :] = wd_ref[...].astype(_BF16)


def _ffn_kernel(*refs, has_mixer, has_pool, has_final, tiles_per_seq, n_prep):
    it = iter(refs)
    x_ref = next(it)
    if has_mixer:
        ya_ref = next(it)
        yb0_ref = next(it)
        u_refs = (next(it), next(it), next(it))
        conv_refs = (next(it), next(it), next(it), next(it))
        wo_ref = next(it)
    if has_pool:
        pool_refs = (next(it), next(it), next(it), next(it), next(it))
    g_ref = next(it)
    wg_ref = next(it)
    wu_ref = next(it)
    wd_ref = next(it)
    if has_final:
        fg_ref = next(it)
    o_ref = next(it)
    wgu_scr = next(it)
    wd_scr = next(it)
    h_scr = next(it)
    act_scr = next(it)
    if has_mixer:
        yb_scr = next(it)
        win_scr = next(it)
    if has_pool:
        hp_scr = next(it)

    step = pl.program_id(0)

    @pl.when(step < n_prep)
    def _():
        _prep_ffn_weights(step, wg_ref, wu_ref, wd_ref, wgu_scr, wd_scr)

    @pl.when(step >= n_prep)
    def _():
        tm = x_ref.shape[0]
        if has_pool:
            xv = _pool_tile(step - n_prep, tiles_per_seq, x_ref, *pool_refs, hp_scr)
        else:
            xv = x_ref[...]
        conv_some = lambda n_units: None
        if has_mixer:
            i = step - n_prep
            n = pl.num_programs(0) - n_prep
            slot = i % 2

            @pl.when(i == 0)
            def _():
                yb_scr[0] = yb0_ref[...]

            yb = yb_scr[slot]
            _conv_fill_window(jnp.minimum(i + 1, n - 1), tiles_per_seq, *u_refs, win_scr)
            pending = iter(_conv_units(tm))

            def conv_some(n_units):
                dep = None
                for _ in range(n_units):
                    unit = next(pending, None)
                    if unit is not None:
                        dep = _conv_unit(*unit, win_scr, *conv_refs, yb_scr.at[1 - slot])
                return dep

            conv_some(len(_conv_units(tm)) - n_prep * CONV_UNITS_PER_DOT)
            yab = jnp.concatenate([ya_ref[...], yb], axis=1)
            xv = xv + jnp.dot(yab, wo_ref[...], preferred_element_type=_F32)
        h = _rmsnorm(xv, g_ref[...])
        h_scr[...] = h.astype(_BF16)
        h_tile = h[:BF16_SUBLANES, :LANES]

        def load_h(dep):
            if dep is not None:
                h_scr[:BF16_SUBLANES, :LANES] = _tied(h_tile, dep).astype(_BF16)
            return h_scr[...]

        gu_width = wgu_scr.shape[2]
        for c in range(n_prep):
            dep = conv_some(CONV_UNITS_PER_DOT)
            gu = jnp.dot(load_h(dep), wgu_scr[c], preferred_element_type=_F32)
            for b0 in range(0, gu_width, 2 * LANES):
                gate = gu[:, b0:b0 + LANES].astype(_BF16)
                up = gu[:, b0 + LANES:b0 + 2 * LANES].astype(_BF16)
                f0 = (c * gu_width + b0) // 2
                act_scr[:, f0:f0 + LANES] = _silu(gate) * up
        acc = xv + jnp.dot(act_scr[...], wd_scr[...], preferred_element_type=_F32)
        if has_final:
            acc = _rmsnorm(acc, fg_ref[...])
        o_ref[...] = acc


def _ffn(x2d, g, wg_all, wu_all, wd_all, layer, mixer=None, pool=None, final_g=None,
         tiles_per_seq=None):
    tokens, d = x2d.shape
    ff = wg_all.shape[2]
    tm = TOKEN_TILE
    n_tiles = tokens // tm
    n_prep = ff // FF_PREP_CHUNK
    last_prep = n_prep - 1
    has_mixer = mixer is not None
    has_pool = pool is not None
    has_final = final_g is not None
    tile = lambda i: jnp.maximum(i - n_prep, 0)
    args = [x2d]
    specs = [pl.BlockSpec((tm, d), lambda i: (tile(i), 0))]
    scratch = [pltpu.VMEM((n_prep, d, 2 * FF_PREP_CHUNK), _BF16), pltpu.VMEM((ff, d), _BF16),
               pltpu.VMEM((tm, d), _BF16), pltpu.VMEM((tm, ff), _BF16)]
    if has_mixer:
        ya2d, yb0, u2d, conv_w, conv_b, ln_g, ln_b, w_out = mixer
        hb = tm // CONV_HALO
        n_halo = tokens // CONV_HALO
        nxt = lambda i: jnp.minimum(tile(i) + 1, n_tiles - 1)
        args += [ya2d, yb0, u2d, u2d, u2d, conv_w, conv_b, ln_g, ln_b, w_out]
        specs += [
            pl.BlockSpec((tm, A_WIDTH), lambda i: (tile(i), 0)),
            _resident(yb0.shape),
            pl.BlockSpec((tm, B_WIDTH), lambda i: (nxt(i), 0)),
            pl.BlockSpec((CONV_HALO, B_WIDTH), lambda i: (jnp.maximum(nxt(i) * hb - 1, 0), 0)),
            pl.BlockSpec((CONV_HALO, B_WIDTH),
                         lambda i: (jnp.minimum((nxt(i) + 1) * hb, n_halo - 1), 0)),
            _resident(conv_w.shape), _resident(conv_b.shape), _resident(ln_g.shape),
            _resident(ln_b.shape), _resident(w_out.shape),
        ]
        scratch += [pltpu.VMEM((2, tm, B_WIDTH), _BF16)] + _conv_scratch(tm)
    if has_pool:
        g_mix, pool_map, pool_scale = pool
        hb = tm // POOL_HALO
        n_halo = tokens // POOL_HALO
        args += [x2d, x2d, g_mix, pool_map, pool_scale]
        specs += [
            pl.BlockSpec((POOL_HALO, d), lambda i: (jnp.maximum(tile(i) * hb - 1, 0), 0)),
            pl.BlockSpec((POOL_HALO, d), lambda i: (jnp.minimum((tile(i) + 1) * hb, n_halo - 1), 0)),
            _resident(g_mix.shape), _resident(pool_map.shape), _resident(pool_scale.shape),
        ]
        scratch += [pltpu.VMEM((tm + 2 * POOL_HALO, d), _F32)]
    args += [g, wg_all, wu_all, wd_all]
    specs += [
        _resident(g.shape),
        pl.BlockSpec((None, d, FF_PREP_CHUNK), lambda i: (layer, 0, jnp.minimum(i, last_prep))),
        pl.BlockSpec((None, d, FF_PREP_CHUNK), lambda i: (layer, 0, jnp.minimum(i, last_prep))),
        pl.BlockSpec((None, FF_PREP_CHUNK, d), lambda i: (layer, jnp.minimum(i, last_prep), 0)),
    ]
    if has_final:
        args.append(final_g)
        specs.append(_resident(final_g.shape))
    return pl.pallas_call(
        functools.partial(_ffn_kernel, has_mixer=has_mixer, has_pool=has_pool, has_final=has_final,
                          tiles_per_seq=tiles_per_seq, n_prep=n_prep),
        grid=(n_prep + n_tiles,),
        in_specs=specs,
        out_specs=pl.BlockSpec((tm, d), lambda i: (tile(i), 0)),
        out_shape=jax.ShapeDtypeStruct((tokens, d), _F32),
        scratch_shapes=scratch,
        compiler_params=pltpu.CompilerParams(
            dimension_semantics=("arbitrary",),
            vmem_limit_bytes=V7X_VMEM_LIMIT_BYTES),
        name="ffn_mixer" if has_mixer else "ffn_final",
    )(*args)


def _pool_tile(tile, tiles_per_seq, x_ref, prev_ref, next_ref, g_ref, pm_ref, ps_ref, hp_scr):
    tm, d = x_ref.shape
    seq = tm * tiles_per_seq
    n = tm + 2 * POOL_HALO
    g = g_ref[...]
    xv = x_ref[...]
    hm = _rmsnorm(xv, g)
    start = (tile % tiles_per_seq) * tm
    halo_iota = jax.lax.broadcasted_iota(jnp.int32, (POOL_HALO, 1), 0)
    prev_ok = (start - POOL_HALO + halo_iota) >= 0
    next_ok = (start + tm + halo_iota) < seq
    hp_scr[0:POOL_HALO, :] = jnp.where(prev_ok, _rmsnorm(prev_ref[...], g), 0.0)
    hp_scr[POOL_HALO:POOL_HALO + tm, :] = hm
    hp_scr[POOL_HALO + tm:, :] = jnp.where(next_ok, _rmsnorm(next_ref[...], g), 0.0)

    pos = start + jax.lax.broadcasted_iota(jnp.int32, (tm, LANES), 0)
    gd = d // len(POOL_WINDOWS)
    ys = []
    for gi, w in enumerate(POOL_WINDOWS):
        lanes = slice(gi * gd, (gi + 1) * gd)
        half = w // 2
        fwd = hp_scr[:, lanes]
        span = 1
        while span < half:
            fwd = fwd + pltpu.roll(fwd, n - span, axis=0)
            span *= 2
        centred = fwd + pltpu.roll(fwd, half, axis=0)
        win = centred[POOL_HALO:POOL_HALO + tm]
        cnt = jnp.minimum(pos + (w - half), seq) - jnp.maximum(pos - half, 0)
        inv = 1.0 / cnt.astype(_F32)
        pg = win * jnp.concatenate([inv] * (gd // LANES), axis=1) - hm[:, lanes]
        ys.append(jnp.dot(pg.astype(_BF16), pm_ref[gi].astype(_BF16), preferred_element_type=_F32))
    return xv + jnp.concatenate(ys, axis=1) * ps_ref[...]


def kernel(x, norm_mix_g, norm_ffn_g, w_in_ab, fnet_map, conv_w, conv_b, conv_ln_g, conv_ln_b,
           w_out_ab, pool_map, pool_scale, ffn_w_gate, ffn_w_up, ffn_w_down, final_g):
    bsz, seq, d = x.shape
    tokens = bsz * seq
    tiles_per_seq = seq // TOKEN_TILE
    row = lambda v: v.reshape(1, -1)
    bf = lambda v: v.astype(_BF16)

    a_perm, u = _in_proj(x, row(norm_mix_g[0]), bf(w_in_ab[0]))
    ya = _fnet(a_perm, _dft_constants(seq), bf(fnet_map[0]))
    u2d = u.reshape(tokens, B_WIDTH)
    conv_p = (conv_w[0], row(conv_b[0]), row(conv_ln_g[0]), row(conv_ln_b[0]))
    mixer = (ya.reshape(tokens, A_WIDTH), _conv_first(u2d, *conv_p), u2d, *conv_p, bf(w_out_ab[0]))
    x2 = _ffn(x.reshape(tokens, d), row(norm_ffn_g[0]), ffn_w_gate, ffn_w_up, ffn_w_down, 0,
              mixer=mixer, tiles_per_seq=tiles_per_seq)

    pool = (row(norm_mix_g[1]), pool_map[0], row(pool_scale[0]))
    out = _ffn(x2, row(norm_ffn_g[1]), ffn_w_gate, ffn_w_up, ffn_w_down, 1, pool=pool,
               final_g=row(final_g), tiles_per_seq=tiles_per_seq)
    return out.reshape(bsz, seq, d)
```

```python
import functools
import math

import jax
import jax.numpy as jnp
import numpy as np
from jax.experimental import pallas as pl
from jax.experimental.pallas import tpu as pltpu

RMS_EPS = 1e-6
LN_EPS = 1e-5

A_HEADS = 4
HEAD_DIM = 128
A_WIDTH = A_HEADS * HEAD_DIM
B_WIDTH = 512
CONV_WIDTH = 31
CONV_PAD = CONV_WIDTH // 2
POOL_WINDOWS = (2, 4, 8, 16)
POOL_HALO = 8

LANES = 128
SUBLANES = 8

DFT_RADIX = 8

V7X_VMEM_LIMIT_BYTES = 56 * 1024 * 1024

TOKEN_TILE = 512
IN_PROJ_TILE = 1024
FF_PREP_CHUNK = 256
BFLY_ROW_TILE = 16
CONV_HALO = 16
CONV_ROW_TILE = 64
CONV_UNITS_PER_DOT = 2
BF16_SUBLANES = 16

_F32 = jnp.float32
_BF16 = jnp.bfloat16


def _resident(shape):
    nd = len(shape)
    return pl.BlockSpec(shape, lambda *_: (0,) * nd, pipeline_mode=pl.Buffered(1))


def _rmsnorm(xv, g):
    inv = jax.lax.rsqrt(jnp.mean(xv * xv, axis=-1, keepdims=True) + RMS_EPS)
    return (xv * inv) * g


def _one_plus_tanh(half_v):
    return 1.0 + jnp.tanh(half_v)


def _in_proj_kernel(x_ref, g_ref, w_ref, a_ref, u_ref, a_scr, w_scr):
    @pl.when((pl.program_id(0) == 0) & (pl.program_id(1) == 0))
    def _():
        w_scr[:, :A_WIDTH] = w_ref[:, :A_WIDTH].astype(_BF16)
        w_scr[:, A_WIDTH:] = (w_ref[:, A_WIDTH:] * 0.5).astype(_BF16)

    h = _rmsnorm(x_ref[...], g_ref[...]).astype(_BF16)
    p = jnp.dot(h, w_scr[...], preferred_element_type=_F32)
    rows = a_scr.shape[1] // DFT_RADIX
    for lt in range(A_WIDTH // LANES):
        lanes = slice(lt * LANES, (lt + 1) * LANES)
        a_scr[lt] = p[:, lanes]
        for jr in range(DFT_RADIX):
            a_ref[jr, :, lanes] = a_scr[lt, pl.ds(jr, rows, stride=DFT_RADIX), :].astype(_BF16)
    half_v = p[:, A_WIDTH:A_WIDTH + B_WIDTH]
    half_gate = p[:, A_WIDTH + B_WIDTH:]
    u_ref[...] = half_v * _one_plus_tanh(half_gate)


def _in_proj(x, g, w_in):
    bsz, seq, d = x.shape
    tm = IN_PROJ_TILE
    inner = seq // DFT_RADIX
    return pl.pallas_call(
        _in_proj_kernel,
        grid=(bsz, seq // tm),
        in_specs=[
            pl.BlockSpec((None, tm, d), lambda b, i: (b, i, 0)),
            _resident((1, d)),
            _resident(w_in.shape),
        ],
        out_specs=[
            pl.BlockSpec((None, DFT_RADIX, tm // DFT_RADIX, A_WIDTH), lambda b, i: (b, 0, i, 0)),
            pl.BlockSpec((None, tm, B_WIDTH), lambda b, i: (b, i, 0)),
        ],
        out_shape=[
            jax.ShapeDtypeStruct((bsz, DFT_RADIX, inner, A_WIDTH), _BF16),
            jax.ShapeDtypeStruct((bsz, seq, B_WIDTH), _F32),
        ],
        scratch_shapes=[pltpu.VMEM((A_WIDTH // LANES, tm, LANES), _F32),
                        pltpu.VMEM(w_in.shape, _BF16)],
        compiler_params=pltpu.CompilerParams(
            dimension_semantics=("arbitrary", "arbitrary"),
            vmem_limit_bytes=V7X_VMEM_LIMIT_BYTES),
        name="in_proj",
    )(x, g, w_in)


def _cadd(a, b):
    return (a[0] + b[0], a[1] + b[1])


def _csub(a, b):
    return (a[0] - b[0], a[1] - b[1])


def _dft4(a0, a1, a2, a3):
    s0, s1 = _cadd(a0, a2), _csub(a0, a2)
    s2, s3 = _cadd(a1, a3), _csub(a1, a3)
    return (_cadd(s0, s2), (s1[0] + s3[1], s1[1] - s3[0]),
            _csub(s0, s2), (s1[0] - s3[1], s1[1] + s3[0]))


def _mul_w8(k, z):
    r, i = z
    h = math.sqrt(0.5)
    if k == 0:
        return z
    if k == 1:
        return (h * (r + i), h * (i - r))
    if k == 2:
        return (i, -r)
    return (h * (i - r), -h * (r + i))


def _fnet_kernel(a_ref, cs_ref, twc_ref, tws_ref, cdsd_ref, map_ref, y_ref, yr_scr, yi_scr):
    seq = y_ref.shape[0]
    inner = seq // DFT_RADIX

    for jr in range(DFT_RADIX):
        yy = jnp.dot(cs_ref[...], a_ref[jr], preferred_element_type=_F32)
        yr_scr[jr * inner:(jr + 1) * inner, :] = yy[:inner]
        yi_scr[jr * inner:(jr + 1) * inner, :] = yy[inner:]

    rt = BFLY_ROW_TILE

    def bfly(c, carry):
        r0 = pl.multiple_of(c * rt, rt)
        for lt in range(A_WIDTH // LANES):
            lanes = slice(lt * LANES, (lt + 1) * LANES)
            z = []
            for jr in range(DFT_RADIX):
                rows = pl.ds(jr * inner + r0, rt)
                yr = yr_scr[rows, lanes]
                yi = yi_scr[rows, lanes]
                if jr == 0:
                    z.append((yr, yi))
                else:
                    tc = twc_ref[rows, :]
                    ts = tws_ref[rows, :]
                    z.append((yr * tc + yi * ts, yi * tc - yr * ts))
            ev = _dft4(z[0], z[2], z[4], z[6])
            od = _dft4(z[1], z[3], z[5], z[7])
            for k in range(4):
                w = _mul_w8(k, od[k])
                lo = _cadd(ev[k], w)
                hi = _csub(ev[k], w)
                rows_lo = pl.ds(k * inner + r0, rt)
                rows_hi = pl.ds((k + 4) * inner + r0, rt)
                yr_scr[rows_lo, lanes] = lo[0]
                yi_scr[rows_lo, lanes] = lo[1]
                yr_scr[rows_hi, lanes] = hi[0]
                yi_scr[rows_hi, lanes] = hi[1]
        return carry

    jax.lax.fori_loop(0, inner // rt, bfly, 0)

    for hd in range(A_HEADS):
        lanes = slice(hd * HEAD_DIM, (hd + 1) * HEAD_DIM)
        lhs = jnp.concatenate([yr_scr[:, lanes].astype(_BF16),
                               yi_scr[:, lanes].astype(_BF16)], axis=1)
        f = jnp.dot(lhs, cdsd_ref[...], preferred_element_type=_F32)
        ya = jnp.dot(f.astype(_BF16), map_ref[hd], preferred_element_type=_F32)
        y_ref[:, lanes] = ya.astype(_BF16)


def _fnet(a_perm, consts, fmap):
    bsz, _, inner, _ = a_perm.shape
    seq = inner * DFT_RADIX
    cs, twc, tws, cdsd = consts
    return pl.pallas_call(
        _fnet_kernel,
        grid=(bsz,),
        in_specs=[
            pl.BlockSpec((None, DFT_RADIX, inner, A_WIDTH), lambda b: (b, 0, 0, 0)),
            _resident(cs.shape), _resident(twc.shape), _resident(tws.shape),
            _resident(cdsd.shape), _resident(fmap.shape),
        ],
        out_specs=pl.BlockSpec((None, seq, A_WIDTH), lambda b: (b, 0, 0)),
        out_shape=jax.ShapeDtypeStruct((bsz, seq, A_WIDTH), _BF16),
        scratch_shapes=[
            pltpu.VMEM((seq, A_WIDTH), _F32),
            pltpu.VMEM((seq, A_WIDTH), _F32),
        ],
        compiler_params=pltpu.CompilerParams(
            dimension_semantics=("arbitrary",),
            vmem_limit_bytes=V7X_VMEM_LIMIT_BYTES),
        name="fnet",
    )(a_perm, cs, twc, tws, cdsd, fmap)


def _dft_constants(seq):
    inner = seq // DFT_RADIX
    k = np.arange(inner, dtype=np.float64)
    ang = 2.0 * np.pi * np.outer(k, k) / inner
    cs = np.concatenate([np.cos(ang), -np.sin(ang)], axis=0)
    jr = np.arange(DFT_RADIX, dtype=np.float64)[:, None]
    tw = 2.0 * np.pi * (jr * k[None, :]) / seq
    twc = np.repeat(np.cos(tw).reshape(seq, 1), LANES, axis=1)
    tws = np.repeat(np.sin(tw).reshape(seq, 1), LANES, axis=1)
    d = np.arange(HEAD_DIM, dtype=np.float64)
    angd = 2.0 * np.pi * np.outer(d, d) / HEAD_DIM
    scale = 1.0 / math.sqrt(seq * HEAD_DIM)
    cdsd = np.concatenate([np.cos(angd), np.sin(angd)], axis=0) * scale
    return (jnp.asarray(cs, _F32).astype(_BF16), jnp.asarray(twc, _F32), jnp.asarray(tws, _F32),
            jnp.asarray(cdsd, _F32).astype(_BF16))


def _conv_fill_window(tile, tiles_per_seq, main_ref, prev_ref, next_ref, win_scr):
    tm = main_ref.shape[0]
    halo = CONV_HALO
    pos = jnp.zeros((halo, 1), jnp.int32) + tile % tiles_per_seq
    win_scr[0:halo, :] = jnp.where(pos == 0, 0.0, prev_ref[...])
    win_scr[halo:halo + tm, :] = main_ref[...]
    win_scr[halo + tm:, :] = jnp.where(pos == tiles_per_seq - 1, 0.0, next_ref[...])


def _conv_unit(rc, lt, win_scr, cw_ref, cb_ref, lg_ref, lb_ref, out_ref):
    ct = CONV_ROW_TILE
    halo = CONV_HALO
    r0 = rc * ct
    lanes = slice(lt * LANES, (lt + 1) * LANES)
    first = halo - CONV_PAD
    span = ct + 2 * halo
    win = win_scr[r0:r0 + span, lanes]
    acc = None
    for s in range(SUBLANES):
        rot = win if s == 0 else pltpu.roll(win, span - s, axis=0)
        for q in range((first + CONV_WIDTH - 1) // SUBLANES + 1):
            k = SUBLANES * q + s - first
            if 0 <= k < CONV_WIDTH:
                term = rot[SUBLANES * q:SUBLANES * q + ct] * cw_ref[k:k + 1, lanes]
                acc = term if acc is None else acc + term
    cv = acc + cb_ref[:, lanes]
    mu = jnp.mean(cv, axis=-1, keepdims=True)
    dv = cv - mu
    var = jnp.mean(dv * dv, axis=-1, keepdims=True)
    half_yn = (dv * jax.lax.rsqrt(var + LN_EPS)) * (lg_ref[:, lanes] * 0.5) + lb_ref[:, lanes] * 0.5
    y = half_yn * _one_plus_tanh(half_yn)
    out_ref[r0:r0 + ct, lanes] = y.astype(_BF16)
    return y[ct - SUBLANES:, :]


def _tied(tile, dep):
    zero = pltpu.bitcast(dep, jnp.uint32)
    zero = jax.lax.shift_right_logical(jax.lax.shift_right_logical(zero, jnp.uint32(16)), jnp.uint32(16))
    zero = pltpu.bitcast(zero, _F32)
    return tile + jnp.concatenate([zero, zero], axis=0)


def _conv_units(tm):
    return [(rc, lt) for rc in range(tm // CONV_ROW_TILE) for lt in range(B_WIDTH // LANES)]


def _conv_scratch(tm):
    return [pltpu.VMEM((tm + 2 * CONV_HALO, B_WIDTH), _F32)]


def _conv_first_kernel(main_ref, next_ref, cw_ref, cb_ref, lg_ref, lb_ref, out_ref, win_scr):
    _conv_fill_window(0, 2, main_ref, next_ref, next_ref, win_scr)
    for rc, lt in _conv_units(main_ref.shape[0]):
        _conv_unit(rc, lt, win_scr, cw_ref, cb_ref, lg_ref, lb_ref, out_ref)


def _conv_first(u2d, conv_w, conv_b, ln_g, ln_b):
    tm = TOKEN_TILE
    return pl.pallas_call(
        _conv_first_kernel,
        grid=(1,),
        in_specs=[
            pl.BlockSpec((tm, B_WIDTH), lambda i: (0, 0)),
            pl.BlockSpec((CONV_HALO, B_WIDTH), lambda i: (tm // CONV_HALO, 0)),
            _resident(conv_w.shape), _resident(conv_b.shape), _resident(ln_g.shape),
            _resident(ln_b.shape),
        ],
        out_specs=pl.BlockSpec((tm, B_WIDTH), lambda i: (0, 0)),
        out_shape=jax.ShapeDtypeStruct((tm, B_WIDTH), _BF16),
        scratch_shapes=_conv_scratch(tm),
        compiler_params=pltpu.CompilerParams(
            dimension_semantics=("arbitrary",),
            vmem_limit_bytes=V7X_VMEM_LIMIT_BYTES),
        name="conv_first",
    )(u2d, u2d, conv_w, conv_b, ln_g, ln_b)


def _prep_ffn_weights(j, wg_ref, wu_ref, wd_ref, wgu_scr, wd_scr):
    cw = wg_ref.shape[1]
    for b in range(cw // LANES):
        src = slice(b * LANES, (b + 1) * LANES)
        wgu_scr[j, :, 2 * b * LANES:(2 * b + 1) * LANES] = (wg_ref[:, src] * 0.5).astype(_BF16)
        wgu_scr[j, :, (2 * b + 1) * LANES:(2 * b + 2) * LANES] = wu_ref[:, src].astype(_BF16)
    wd_scr[pl.ds(pl.multiple_of(j * cw, cw), cw), :] = wd_ref[...].astype(_BF16)


def _ffn_kernel(*refs, has_mixer, has_pool, has_final, tiles_per_seq, n_prep):
    it = iter(refs)
    x_ref = next(it)
    if has_mixer:
        ya_ref = next(it)
        yb0_ref = next(it)
        u_refs = (next(it), next(it), next(it))
        conv_refs = (next(it), next(it), next(it), next(it))
        wo_ref = next(it)
    if has_pool:
        pool_refs = (next(it), next(it), next(it), next(it), next(it))
    g_ref = next(it)
    wg_ref = next(it)
    wu_ref = next(it)
    wd_ref = next(it)
    if has_final:
        fg_ref = next(it)
    o_ref = next(it)
    wgu_scr = next(it)
    wd_scr = next(it)
    h_scr = next(it)
    act_scr = next(it)
    if has_mixer:
        yb_scr = next(it)
        win_scr = next(it)
    if has_pool:
        hp_scr = next(it)

    step = pl.program_id(0)

    @pl.when(step < n_prep)
    def _():
        _prep_ffn_weights(step, wg_ref, wu_ref, wd_ref, wgu_scr, wd_scr)

    @pl.when(step >= n_prep)
    def _():
        tm = x_ref.shape[0]
        if has_pool:
            xv = _pool_tile(step - n_prep, tiles_per_seq, x_ref, *pool_refs, hp_scr)
        else:
            xv = x_ref[...]
        conv_some = lambda n_units: None
        if has_mixer:
            i = step - n_prep
            n = pl.num_programs(0) - n_prep
            slot = i % 2

            @pl.when(i == 0)
            def _():
                yb_scr[0] = yb0_ref[...]

            yb = yb_scr[slot]
            _conv_fill_window(jnp.minimum(i + 1, n - 1), tiles_per_seq, *u_refs, win_scr)
            pending = iter(_conv_units(tm))

            def conv_some(n_units):
                dep = None
                for _ in range(n_units):
                    unit = next(pending, None)
                    if unit is not None:
                        dep = _conv_unit(*unit, win_scr, *conv_refs, yb_scr.at[1 - slot])
                return dep

            conv_some(len(_conv_units(tm)) - n_prep * CONV_UNITS_PER_DOT)
            yab = jnp.concatenate([ya_ref[...], yb], axis=1)
            xv = xv + jnp.dot(yab, wo_ref[...], preferred_element_type=_F32)
        h = _rmsnorm(xv, g_ref[...])
        h_scr[...] = h.astype(_BF16)
        h_tile = h[:BF16_SUBLANES, :LANES]

        def load_h(dep):
            if dep is not None:
                h_scr[:BF16_SUBLANES, :LANES] = _tied(h_tile, dep).astype(_BF16)
            return h_scr[...]

        gu_width = wgu_scr.shape[2]
        for c in range(n_prep):
            dep = conv_some(CONV_UNITS_PER_DOT)
            gu = jnp.dot(load_h(dep), wgu_scr[c], preferred_element_type=_F32)
            for b0 in range(0, gu_width, 2 * LANES):
                half_gate = gu[:, b0:b0 + LANES]
                up = gu[:, b0 + LANES:b0 + 2 * LANES]
                f0 = (c * gu_width + b0) // 2
                act = (half_gate * up) * _one_plus_tanh(half_gate)
                act_scr[:, f0:f0 + LANES] = act.astype(_BF16)
        acc = xv + jnp.dot(act_scr[...], wd_scr[...], preferred_element_type=_F32)
        if has_final:
            acc = _rmsnorm(acc, fg_ref[...])
        o_ref[...] = acc


def _ffn(x2d, g, wg_all, wu_all, wd_all, layer, mixer=None, pool=None, final_g=None,
         tiles_per_seq=None):
    tokens, d = x2d.shape
    ff = wg_all.shape[2]
    tm = TOKEN_TILE
    n_tiles = tokens // tm
    n_prep = ff // FF_PREP_CHUNK
    last_prep = n_prep - 1
    has_mixer = mixer is not None
    has_pool = pool is not None
    has_final = final_g is not None
    tile = lambda i: jnp.maximum(i - n_prep, 0)
    args = [x2d]
    specs = [pl.BlockSpec((tm, d), lambda i: (tile(i), 0))]
    scratch = [pltpu.VMEM((n_prep, d, 2 * FF_PREP_CHUNK), _BF16), pltpu.VMEM((ff, d), _BF16),
               pltpu.VMEM((tm, d), _BF16), pltpu.VMEM((tm, ff), _BF16)]
    if has_mixer:
        ya2d, yb0, u2d, conv_w, conv_b, ln_g, ln_b, w_out = mixer
        hb = tm // CONV_HALO
        n_halo = tokens // CONV_HALO
        nxt = lambda i: jnp.minimum(tile(i) + 1, n_tiles - 1)
        args += [ya2d, yb0, u2d, u2d, u2d, conv_w, conv_b, ln_g, ln_b, w_out]
        specs += [
            pl.BlockSpec((tm, A_WIDTH), lambda i: (tile(i), 0)),
            _resident(yb0.shape),
            pl.BlockSpec((tm, B_WIDTH), lambda i: (nxt(i), 0)),
            pl.BlockSpec((CONV_HALO, B_WIDTH), lambda i: (jnp.maximum(nxt(i) * hb - 1, 0), 0)),
            pl.BlockSpec((CONV_HALO, B_WIDTH),
                         lambda i: (jnp.minimum((nxt(i) + 1) * hb, n_halo - 1), 0)),
            _resident(conv_w.shape), _resident(conv_b.shape), _resident(ln_g.shape),
            _resident(ln_b.shape), _resident(w_out.shape),
        ]
        scratch += [pltpu.VMEM((2, tm, B_WIDTH), _BF16)] + _conv_scratch(tm)
    if has_pool:
        g_mix, pool_map, pool_scale = pool
        hb = tm // POOL_HALO
        n_halo = tokens // POOL_HALO
        args += [x2d, x2d, g_mix, pool_map, pool_scale]
        specs += [
            pl.BlockSpec((POOL_HALO, d), lambda i: (jnp.maximum(tile(i) * hb - 1, 0), 0)),
            pl.BlockSpec((POOL_HALO, d), lambda i: (jnp.minimum((tile(i) + 1) * hb, n_halo - 1), 0)),
            _resident(g_mix.shape), _resident(pool_map.shape), _resident(pool_scale.shape),
        ]
        scratch += [pltpu.VMEM((tm + 2 * POOL_HALO, d), _F32)]
    args += [g, wg_all, wu_all, wd_all]
    specs += [
        _resident(g.shape),
        pl.BlockSpec((None, d, FF_PREP_CHUNK), lambda i: (layer, 0, jnp.minimum(i, last_prep))),
        pl.BlockSpec((None, d, FF_PREP_CHUNK), lambda i: (layer, 0, jnp.minimum(i, last_prep))),
        pl.BlockSpec((None, FF_PREP_CHUNK, d), lambda i: (layer, jnp.minimum(i, last_prep), 0)),
    ]
    if has_final:
        args.append(final_g)
        specs.append(_resident(final_g.shape))
    return pl.pallas_call(
        functools.partial(_ffn_kernel, has_mixer=has_mixer, has_pool=has_pool, has_final=has_final,
                          tiles_per_seq=tiles_per_seq, n_prep=n_prep),
        grid=(n_prep + n_tiles,),
        in_specs=specs,
        out_specs=pl.BlockSpec((tm, d), lambda i: (tile(i), 0)),
        out_shape=jax.ShapeDtypeStruct((tokens, d), _F32),
        scratch_shapes=scratch,
        compiler_params=pltpu.CompilerParams(
            dimension_semantics=("arbitrary",),
            vmem_limit_bytes=V7X_VMEM_LIMIT_BYTES),
        name="ffn_mixer" if has_mixer else "ffn_final",
    )(*args)


def _pool_tile(tile, tiles_per_seq, x_ref, prev_ref, next_ref, g_ref, pm_ref, ps_ref, hp_scr):
    tm, d = x_ref.shape
    seq = tm * tiles_per_seq
    n = tm + 2 * POOL_HALO
    g = g_ref[...]
    xv = x_ref[...]
    hm = _rmsnorm(xv, g)
    start = (tile % tiles_per_seq) * tm
    halo_iota = jax.lax.broadcasted_iota(jnp.int32, (POOL_HALO, 1), 0)
    prev_ok = (start - POOL_HALO + halo_iota) >= 0
    next_ok = (start + tm + halo_iota) < seq
    hp_scr[0:POOL_HALO, :] = jnp.where(prev_ok, _rmsnorm(prev_ref[...], g), 0.0)
    hp_scr[POOL_HALO:POOL_HALO + tm, :] = hm
    hp_scr[POOL_HALO + tm:, :] = jnp.where(next_ok, _rmsnorm(next_ref[...], g), 0.0)

    pos = start + jax.lax.broadcasted_iota(jnp.int32, (tm, LANES), 0)
    gd = d // len(POOL_WINDOWS)
    ys = []
    for gi, w in enumerate(POOL_WINDOWS):
        lanes = slice(gi * gd, (gi + 1) * gd)
        half = w // 2
        fwd = hp_scr[:, lanes]
        span = 1
        while span < half:
            fwd = fwd + pltpu.roll(fwd, n - span, axis=0)
            span *= 2
        centred = fwd + pltpu.roll(fwd, half, axis=0)
        win = centred[POOL_HALO:POOL_HALO + tm]
        cnt = jnp.minimum(pos + (w - half), seq) - jnp.maximum(pos - half, 0)
        inv = 1.0 / cnt.astype(_F32)
        pg = win * jnp.concatenate([inv] * (gd // LANES), axis=1) - hm[:, lanes]
        ys.append(jnp.dot(pg.astype(_BF16), pm_ref[gi].astype(_BF16), preferred_element_type=_F32))
    return xv + jnp.concatenate(ys, axis=1) * ps_ref[...]


def kernel(x, norm_mix_g, norm_ffn_g, w_in_ab, fnet_map, conv_w, conv_b, conv_ln_g, conv_ln_b,
           w_out_ab, pool_map, pool_scale, ffn_w_gate, ffn_w_up, ffn_w_down, final_g):
    bsz, seq, d = x.shape
    tokens = bsz * seq
    tiles_per_seq = seq // TOKEN_TILE
    row = lambda v: v.reshape(1, -1)
    bf = lambda v: v.astype(_BF16)

    a_perm, u = _in_proj(x, row(norm_mix_g[0]), w_in_ab[0])
    ya = _fnet(a_perm, _dft_constants(seq), bf(fnet_map[0]))
    u2d = u.reshape(tokens, B_WIDTH)
    conv_p = (conv_w[0], row(conv_b[0]), row(conv_ln_g[0]), row(conv_ln_b[0]))
    mixer = (ya.reshape(tokens, A_WIDTH), _conv_first(u2d, *conv_p), u2d, *conv_p, bf(w_out_ab[0]))
    x2 = _ffn(x.reshape(tokens, d), row(norm_ffn_g[0]), ffn_w_gate, ffn_w_up, ffn_w_down, 0,
              mixer=mixer, tiles_per_seq=tiles_per_seq)

    pool = (row(norm_mix_g[1]), pool_map[0], row(pool_scale[0]))
    out = _ffn(x2, row(norm_ffn_g[1]), ffn_w_gate, ffn_w_up, ffn_w_down, 1, pool=pool,
               final_g=row(final_g), tiles_per_seq=tiles_per_seq)
    return out.reshape(bsz, seq, d)
```

```python
import functools
import math

import jax
import jax.numpy as jnp
import numpy as np
from jax.experimental import pallas as pl
from jax.experimental.pallas import tpu as pltpu

RMS_EPS = 1e-6
LN_EPS = 1e-5

A_HEADS = 4
HEAD_DIM = 128
A_WIDTH = A_HEADS * HEAD_DIM
B_WIDTH = 512
CONV_WIDTH = 31
CONV_PAD = CONV_WIDTH // 2
POOL_WINDOWS = (2, 4, 8, 16)
POOL_HALO = 8

LANES = 128
SUBLANES = 8

DFT_RADIX = 8

V7X_VMEM_LIMIT_BYTES = 56 * 1024 * 1024

TOKEN_TILE = 512
IN_PROJ_TILE = 1024
FFN_FINAL_TILE = 1024
FF_PREP_CHUNK = 256
BFLY_ROW_TILE = 16
CONV_HALO = 16
CONV_ROW_TILE = 64
CONV_UNITS_PER_DOT = 2
BF16_SUBLANES = 16

_F32 = jnp.float32
_BF16 = jnp.bfloat16


def _resident(shape):
    nd = len(shape)
    return pl.BlockSpec(shape, lambda *_: (0,) * nd, pipeline_mode=pl.Buffered(1))


def _rmsnorm(xv, g):
    inv = jax.lax.rsqrt(jnp.mean(xv * xv, axis=-1, keepdims=True) + RMS_EPS)
    return (xv * inv) * g


def _one_plus_tanh(half_v):
    return 1.0 + jnp.tanh(half_v)


def _in_proj_kernel(x_ref, g_ref, w_ref, a_ref, u_ref, a_scr, w_scr):
    @pl.when((pl.program_id(0) == 0) & (pl.program_id(1) == 0))
    def _():
        w_scr[:, :A_WIDTH] = w_ref[:, :A_WIDTH].astype(_BF16)
        w_scr[:, A_WIDTH:] = (w_ref[:, A_WIDTH:] * 0.5).astype(_BF16)

    h = _rmsnorm(x_ref[...], g_ref[...]).astype(_BF16)
    p = jnp.dot(h, w_scr[...], preferred_element_type=_F32)
    rows = a_scr.shape[1] // DFT_RADIX
    for lt in range(A_WIDTH // LANES):
        lanes = slice(lt * LANES, (lt + 1) * LANES)
        a_scr[lt] = p[:, lanes]
        for jr in range(DFT_RADIX):
            a_ref[jr, :, lanes] = a_scr[lt, pl.ds(jr, rows, stride=DFT_RADIX), :].astype(_BF16)
    half_v = p[:, A_WIDTH:A_WIDTH + B_WIDTH]
    half_gate = p[:, A_WIDTH + B_WIDTH:]
    u_ref[...] = half_v * _one_plus_tanh(half_gate)


def _in_proj(x, g, w_in):
    bsz, seq, d = x.shape
    tm = IN_PROJ_TILE
    inner = seq // DFT_RADIX
    return pl.pallas_call(
        _in_proj_kernel,
        grid=(bsz, seq // tm),
        in_specs=[
            pl.BlockSpec((None, tm, d), lambda b, i: (b, i, 0)),
            _resident((1, d)),
            _resident(w_in.shape),
        ],
        out_specs=[
            pl.BlockSpec((None, DFT_RADIX, tm // DFT_RADIX, A_WIDTH), lambda b, i: (b, 0, i, 0)),
            pl.BlockSpec((None, tm, B_WIDTH), lambda b, i: (b, i, 0)),
        ],
        out_shape=[
            jax.ShapeDtypeStruct((bsz, DFT_RADIX, inner, A_WIDTH), _BF16),
            jax.ShapeDtypeStruct((bsz, seq, B_WIDTH), _F32),
        ],
        scratch_shapes=[pltpu.VMEM((A_WIDTH // LANES, tm, LANES), _F32),
                        pltpu.VMEM(w_in.shape, _BF16)],
        compiler_params=pltpu.CompilerParams(
            dimension_semantics=("arbitrary", "arbitrary"),
            vmem_limit_bytes=V7X_VMEM_LIMIT_BYTES),
        name="in_proj",
    )(x, g, w_in)


def _cadd(a, b):
    return (a[0] + b[0], a[1] + b[1])


def _csub(a, b):
    return (a[0] - b[0], a[1] - b[1])


def _dft4(a0, a1, a2, a3):
    s0, s1 = _cadd(a0, a2), _csub(a0, a2)
    s2, s3 = _cadd(a1, a3), _csub(a1, a3)
    return (_cadd(s0, s2), (s1[0] + s3[1], s1[1] - s3[0]),
            _csub(s0, s2), (s1[0] - s3[1], s1[1] + s3[0]))


def _mul_w8(k, z):
    r, i = z
    h = math.sqrt(0.5)
    if k == 0:
        return z
    if k == 1:
        return (h * (r + i), h * (i - r))
    if k == 2:
        return (i, -r)
    return (h * (i - r), -h * (r + i))


def _fnet_kernel(a_ref, cs_ref, twc_ref, tws_ref, cdsd_ref, map_ref, y_ref, yr_scr, yi_scr):
    seq = y_ref.shape[0]
    inner = seq // DFT_RADIX

    for jr in range(DFT_RADIX):
        yy = jnp.dot(cs_ref[...], a_ref[jr], preferred_element_type=_F32)
        yr_scr[jr * inner:(jr + 1) * inner, :] = yy[:inner]
        yi_scr[jr * inner:(jr + 1) * inner, :] = yy[inner:]

    rt = BFLY_ROW_TILE

    def bfly(c, carry):
        r0 = pl.multiple_of(c * rt, rt)
        for lt in range(A_WIDTH // LANES):
            lanes = slice(lt * LANES, (lt + 1) * LANES)
            z = []
            for jr in range(DFT_RADIX):
                rows = pl.ds(jr * inner + r0, rt)
                yr = yr_scr[rows, lanes]
                yi = yi_scr[rows, lanes]
                if jr == 0:
                    z.append((yr, yi))
                else:
                    tc = twc_ref[rows, :]
                    ts = tws_ref[rows, :]
                    z.append((yr * tc + yi * ts, yi * tc - yr * ts))
            ev = _dft4(z[0], z[2], z[4], z[6])
            od = _dft4(z[1], z[3], z[5], z[7])
            for k in range(4):
                w = _mul_w8(k, od[k])
                lo = _cadd(ev[k], w)
                hi = _csub(ev[k], w)
                rows_lo = pl.ds(k * inner + r0, rt)
                rows_hi = pl.ds((k + 4) * inner + r0, rt)
                yr_scr[rows_lo, lanes] = lo[0]
                yi_scr[rows_lo, lanes] = lo[1]
                yr_scr[rows_hi, lanes] = hi[0]
                yi_scr[rows_hi, lanes] = hi[1]
        return carry

    jax.lax.fori_loop(0, inner // rt, bfly, 0)

    for hd in range(A_HEADS):
        lanes = slice(hd * HEAD_DIM, (hd + 1) * HEAD_DIM)
        lhs = jnp.concatenate([yr_scr[:, lanes].astype(_BF16),
                               yi_scr[:, lanes].astype(_BF16)], axis=1)
        f = jnp.dot(lhs, cdsd_ref[...], preferred_element_type=_F32)
        ya = jnp.dot(f.astype(_BF16), map_ref[hd], preferred_element_type=_F32)
        y_ref[:, lanes] = ya.astype(_BF16)


def _fnet(a_perm, consts, fmap):
    bsz, _, inner, _ = a_perm.shape
    seq = inner * DFT_RADIX
    cs, twc, tws, cdsd = consts
    return pl.pallas_call(
        _fnet_kernel,
        grid=(bsz,),
        in_specs=[
            pl.BlockSpec((None, DFT_RADIX, inner, A_WIDTH), lambda b: (b, 0, 0, 0)),
            _resident(cs.shape), _resident(twc.shape), _resident(tws.shape),
            _resident(cdsd.shape), _resident(fmap.shape),
        ],
        out_specs=pl.BlockSpec((None, seq, A_WIDTH), lambda b: (b, 0, 0)),
        out_shape=jax.ShapeDtypeStruct((bsz, seq, A_WIDTH), _BF16),
        scratch_shapes=[
            pltpu.VMEM((seq, A_WIDTH), _F32),
            pltpu.VMEM((seq, A_WIDTH), _F32),
        ],
        compiler_params=pltpu.CompilerParams(
            dimension_semantics=("arbitrary",),
            vmem_limit_bytes=V7X_VMEM_LIMIT_BYTES),
        name="fnet",
    )(a_perm, cs, twc, tws, cdsd, fmap)


def _dft_constants(seq):
    inner = seq // DFT_RADIX
    k = np.arange(inner, dtype=np.float64)
    ang = 2.0 * np.pi * np.outer(k, k) / inner
    cs = np.concatenate([np.cos(ang), -np.sin(ang)], axis=0)
    jr = np.arange(DFT_RADIX, dtype=np.float64)[:, None]
    tw = 2.0 * np.pi * (jr * k[None, :]) / seq
    twc = np.repeat(np.cos(tw).reshape(seq, 1), LANES, axis=1)
    tws = np.repeat(np.sin(tw).reshape(seq, 1), LANES, axis=1)
    d = np.arange(HEAD_DIM, dtype=np.float64)
    angd = 2.0 * np.pi * np.outer(d, d) / HEAD_DIM
    scale = 1.0 / math.sqrt(seq * HEAD_DIM)
    cdsd = np.concatenate([np.cos(angd), np.sin(angd)], axis=0) * scale
    return (jnp.asarray(cs, _F32).astype(_BF16), jnp.asarray(twc, _F32), jnp.asarray(tws, _F32),
            jnp.asarray(cdsd, _F32).astype(_BF16))


def _conv_fill_window(tile, tiles_per_seq, main_ref, prev_ref, next_ref, win_scr):
    tm = main_ref.shape[0]
    halo = CONV_HALO
    pos = jnp.zeros((halo, 1), jnp.int32) + tile % tiles_per_seq
    win_scr[0:halo, :] = jnp.where(pos == 0, 0.0, prev_ref[...])
    win_scr[halo:halo + tm, :] = main_ref[...]
    win_scr[halo + tm:, :] = jnp.where(pos == tiles_per_seq - 1, 0.0, next_ref[...])


def _conv_unit(rc, lt, win_scr, cw_ref, cb_ref, lg_ref, lb_ref, out_ref):
    ct = CONV_ROW_TILE
    halo = CONV_HALO
    r0 = rc * ct
    lanes = slice(lt * LANES, (lt + 1) * LANES)
    first = halo - CONV_PAD
    span = ct + 2 * halo
    win = win_scr[r0:r0 + span, lanes]
    acc = None
    for s in range(SUBLANES):
        rot = win if s == 0 else pltpu.roll(win, span - s, axis=0)
        for q in range((first + CONV_WIDTH - 1) // SUBLANES + 1):
            k = SUBLANES * q + s - first
            if 0 <= k < CONV_WIDTH:
                term = rot[SUBLANES * q:SUBLANES * q + ct] * cw_ref[k:k + 1, lanes]
                acc = term if acc is None else acc + term
    cv = acc + cb_ref[:, lanes]
    mu = jnp.mean(cv, axis=-1, keepdims=True)
    dv = cv - mu
    var = jnp.mean(dv * dv, axis=-1, keepdims=True)
    half_yn = (dv * jax.lax.rsqrt(var + LN_EPS)) * (lg_ref[:, lanes] * 0.5) + lb_ref[:, lanes] * 0.5
    y = half_yn * _one_plus_tanh(half_yn)
    out_ref[r0:r0 + ct, lanes] = y.astype(_BF16)
    return y[ct - SUBLANES:, :]


def _tied(tile, dep):
    zero = pltpu.bitcast(dep, jnp.uint32)
    zero = jax.lax.shift_right_logical(jax.lax.shift_right_logical(zero, jnp.uint32(16)), jnp.uint32(16))
    zero = pltpu.bitcast(zero, _F32)
    return tile + jnp.concatenate([zero, zero], axis=0)


def _conv_units(tm):
    return [(rc, lt) for rc in range(tm // CONV_ROW_TILE) for lt in range(B_WIDTH // LANES)]


def _conv_scratch(tm):
    return [pltpu.VMEM((tm + 2 * CONV_HALO, B_WIDTH), _F32)]


def _conv_first_kernel(main_ref, next_ref, cw_ref, cb_ref, lg_ref, lb_ref, out_ref, win_scr):
    _conv_fill_window(0, 2, main_ref, next_ref, next_ref, win_scr)
    for rc, lt in _conv_units(main_ref.shape[0]):
        _conv_unit(rc, lt, win_scr, cw_ref, cb_ref, lg_ref, lb_ref, out_ref)


def _conv_first(u2d, conv_w, conv_b, ln_g, ln_b):
    tm = TOKEN_TILE
    return pl.pallas_call(
        _conv_first_kernel,
        grid=(1,),
        in_specs=[
            pl.BlockSpec((tm, B_WIDTH), lambda i: (0, 0)),
            pl.BlockSpec((CONV_HALO, B_WIDTH), lambda i: (tm // CONV_HALO, 0)),
            _resident(conv_w.shape), _resident(conv_b.shape), _resident(ln_g.shape),
            _resident(ln_b.shape),
        ],
        out_specs=pl.BlockSpec((tm, B_WIDTH), lambda i: (0, 0)),
        out_shape=jax.ShapeDtypeStruct((tm, B_WIDTH), _BF16),
        scratch_shapes=_conv_scratch(tm),
        compiler_params=pltpu.CompilerParams(
            dimension_semantics=("arbitrary",),
            vmem_limit_bytes=V7X_VMEM_LIMIT_BYTES),
        name="conv_first",
    )(u2d, u2d, conv_w, conv_b, ln_g, ln_b)


def _prep_ffn_weights(j, wg_ref, wu_ref, wd_ref, wgu_scr, wd_scr):
    cw = wg_ref.shape[1]
    for b in range(cw // LANES):
        src = slice(b * LANES, (b + 1) * LANES)
        wgu_scr[j, :, 2 * b * LANES:(2 * b + 1) * LANES] = (wg_ref[:, src] * 0.5).astype(_BF16)
        wgu_scr[j, :, (2 * b + 1) * LANES:(2 * b + 2) * LANES] = wu_ref[:, src].astype(_BF16)
    wd_scr[pl.ds(pl.multiple_of(j * cw, cw), cw), :] = wd_ref[...].astype(_BF16)


def _ffn_kernel(*refs, has_mixer, has_pool, has_final, tiles_per_seq, n_prep):
    it = iter(refs)
    x_ref = next(it)
    if has_mixer:
        ya_ref = next(it)
        yb0_ref = next(it)
        u_refs = (next(it), next(it), next(it))
        conv_refs = (next(it), next(it), next(it), next(it))
        wo_ref = next(it)
    if has_pool:
        pool_refs = (next(it), next(it), next(it), next(it), next(it))
    g_ref = next(it)
    wg_ref = next(it)
    wu_ref = next(it)
    wd_ref = next(it)
    if has_final:
        fg_ref = next(it)
    o_ref = next(it)
    wgu_scr = next(it)
    wd_scr = next(it)
    h_scr = next(it)
    act_scr = next(it)
    if has_mixer:
        yb_scr = next(it)
        win_scr = next(it)
    if has_pool:
        hp_scr = next(it)

    step = pl.program_id(0)

    @pl.when(step < n_prep)
    def _():
        _prep_ffn_weights(step, wg_ref, wu_ref, wd_ref, wgu_scr, wd_scr)

    @pl.when(step >= n_prep)
    def _():
        tm = x_ref.shape[0]
        if has_pool:
            xv = _pool_tile(step - n_prep, tiles_per_seq, x_ref, *pool_refs, hp_scr)
        else:
            xv = x_ref[...]
        conv_some = lambda n_units: None
        if has_mixer:
            i = step - n_prep
            n = pl.num_programs(0) - n_prep
            slot = i % 2

            @pl.when(i == 0)
            def _():
                yb_scr[0] = yb0_ref[...]

            yb = yb_scr[slot]
            _conv_fill_window(jnp.minimum(i + 1, n - 1), tiles_per_seq, *u_refs, win_scr)
            pending = iter(_conv_units(tm))

            def conv_some(n_units):
                dep = None
                for _ in range(n_units):
                    unit = next(pending, None)
                    if unit is not None:
                        dep = _conv_unit(*unit, win_scr, *conv_refs, yb_scr.at[1 - slot])
                return dep

            conv_some(len(_conv_units(tm)) - n_prep * CONV_UNITS_PER_DOT)
            yab = jnp.concatenate([ya_ref[...], yb], axis=1)
            xv = xv + jnp.dot(yab, wo_ref[...], preferred_element_type=_F32)
        h = _rmsnorm(xv, g_ref[...])
        h_scr[...] = h.astype(_BF16)
        h_tile = h[:BF16_SUBLANES, :LANES]

        def load_h(dep):
            if dep is not None:
                h_scr[:BF16_SUBLANES, :LANES] = _tied(h_tile, dep).astype(_BF16)
            return h_scr[...]

        gu_width = wgu_scr.shape[2]
        for c in range(n_prep):
            dep = conv_some(CONV_UNITS_PER_DOT)
            gu = jnp.dot(load_h(dep), wgu_scr[c], preferred_element_type=_F32)
            for b0 in range(0, gu_width, 2 * LANES):
                half_gate = gu[:, b0:b0 + LANES]
                up = gu[:, b0 + LANES:b0 + 2 * LANES]
                f0 = (c * gu_width + b0) // 2
                act = (half_gate * up) * _one_plus_tanh(half_gate)
                act_scr[:, f0:f0 + LANES] = act.astype(_BF16)
        acc = xv + jnp.dot(act_scr[...], wd_scr[...], preferred_element_type=_F32)
        if has_final:
            acc = _rmsnorm(acc, fg_ref[...])
        o_ref[...] = acc


def _ffn(x2d, g, wg_all, wu_all, wd_all, layer, mixer=None, pool=None, final_g=None,
         seq=None, tile_rows=TOKEN_TILE):
    tokens, d = x2d.shape
    ff = wg_all.shape[2]
    tm = tile_rows
    tiles_per_seq = seq // tm
    n_tiles = tokens // tm
    n_prep = ff // FF_PREP_CHUNK
    last_prep = n_prep - 1
    has_mixer = mixer is not None
    has_pool = pool is not None
    has_final = final_g is not None
    tile = lambda i: jnp.maximum(i - n_prep, 0)
    args = [x2d]
    specs = [pl.BlockSpec((tm, d), lambda i: (tile(i), 0))]
    scratch = [pltpu.VMEM((n_prep, d, 2 * FF_PREP_CHUNK), _BF16), pltpu.VMEM((ff, d), _BF16),
               pltpu.VMEM((tm, d), _BF16), pltpu.VMEM((tm, ff), _BF16)]
    if has_mixer:
        ya2d, yb0, u2d, conv_w, conv_b, ln_g, ln_b, w_out = mixer
        hb = tm // CONV_HALO
        n_halo = tokens // CONV_HALO
        nxt = lambda i: jnp.minimum(tile(i) + 1, n_tiles - 1)
        args += [ya2d, yb0, u2d, u2d, u2d, conv_w, conv_b, ln_g, ln_b, w_out]
        specs += [
            pl.BlockSpec((tm, A_WIDTH), lambda i: (tile(i), 0)),
            _resident(yb0.shape),
            pl.BlockSpec((tm, B_WIDTH), lambda i: (nxt(i), 0)),
            pl.BlockSpec((CONV_HALO, B_WIDTH), lambda i: (jnp.maximum(nxt(i) * hb - 1, 0), 0)),
            pl.BlockSpec((CONV_HALO, B_WIDTH),
                         lambda i: (jnp.minimum((nxt(i) + 1) * hb, n_halo - 1), 0)),
            _resident(conv_w.shape), _resident(conv_b.shape), _resident(ln_g.shape),
            _resident(ln_b.shape), _resident(w_out.shape),
        ]
        scratch += [pltpu.VMEM((2, tm, B_WIDTH), _BF16)] + _conv_scratch(tm)
    if has_pool:
        g_mix, pool_map, pool_scale = pool
        hb = tm // POOL_HALO
        n_halo = tokens // POOL_HALO
        args += [x2d, x2d, g_mix, pool_map, pool_scale]
        specs += [
            pl.BlockSpec((POOL_HALO, d), lambda i: (jnp.maximum(tile(i) * hb - 1, 0), 0)),
            pl.BlockSpec((POOL_HALO, d), lambda i: (jnp.minimum((tile(i) + 1) * hb, n_halo - 1), 0)),
            _resident(g_mix.shape), _resident(pool_map.shape), _resident(pool_scale.shape),
        ]
        scratch += [pltpu.VMEM((tm + 2 * POOL_HALO, d), _F32)]
    args += [g, wg_all, wu_all, wd_all]
    specs += [
        _resident(g.shape),
        pl.BlockSpec((None, d, FF_PREP_CHUNK), lambda i: (layer, 0, jnp.minimum(i, last_prep))),
        pl.BlockSpec((None, d, FF_PREP_CHUNK), lambda i: (layer, 0, jnp.minimum(i, last_prep))),
        pl.BlockSpec((None, FF_PREP_CHUNK, d), lambda i: (layer, jnp.minimum(i, last_prep), 0)),
    ]
    if has_final:
        args.append(final_g)
        specs.append(_resident(final_g.shape))
    return pl.pallas_call(
        functools.partial(_ffn_kernel, has_mixer=has_mixer, has_pool=has_pool, has_final=has_final,
                          tiles_per_seq=tiles_per_seq, n_prep=n_prep),
        grid=(n_prep + n_tiles,),
        in_specs=specs,
        out_specs=pl.BlockSpec((tm, d), lambda i: (tile(i), 0)),
        out_shape=jax.ShapeDtypeStruct((tokens, d), _F32),
        scratch_shapes=scratch,
        compiler_params=pltpu.CompilerParams(
            dimension_semantics=("arbitrary",),
            vmem_limit_bytes=V7X_VMEM_LIMIT_BYTES),
        name="ffn_mixer" if has_mixer else "ffn_final",
    )(*args)


def _pool_tile(tile, tiles_per_seq, x_ref, prev_ref, next_ref, g_ref, pm_ref, ps_ref, hp_scr):
    tm, d = x_ref.shape
    seq = tm * tiles_per_seq
    n = tm + 2 * POOL_HALO
    g = g_ref[...]
    xv = x_ref[...]
    hm = _rmsnorm(xv, g)
    start = (tile % tiles_per_seq) * tm
    halo_iota = jax.lax.broadcasted_iota(jnp.int32, (POOL_HALO, 1), 0)
    prev_ok = (start - POOL_HALO + halo_iota) >= 0
    next_ok = (start + tm + halo_iota) < seq
    hp_scr[0:POOL_HALO, :] = jnp.where(prev_ok, _rmsnorm(prev_ref[...], g), 0.0)
    hp_scr[POOL_HALO:POOL_HALO + tm, :] = hm
    hp_scr[POOL_HALO + tm:, :] = jnp.where(next_ok, _rmsnorm(next_ref[...], g), 0.0)

    pos = start + jax.lax.broadcasted_iota(jnp.int32, (tm, LANES), 0)
    gd = d // len(POOL_WINDOWS)
    ys = []
    for gi, w in enumerate(POOL_WINDOWS):
        lanes = slice(gi * gd, (gi + 1) * gd)
        half = w // 2
        fwd = hp_scr[:, lanes]
        span = 1
        while span < half:
            fwd = fwd + pltpu.roll(fwd, n - span, axis=0)
            span *= 2
        centred = fwd + pltpu.roll(fwd, half, axis=0)
        win = centred[POOL_HALO:POOL_HALO + tm]
        cnt = jnp.minimum(pos + (w - half), seq) - jnp.maximum(pos - half, 0)
        inv = 1.0 / cnt.astype(_F32)
        pg = win * jnp.concatenate([inv] * (gd // LANES), axis=1) - hm[:, lanes]
        ys.append(jnp.dot(pg.astype(_BF16), pm_ref[gi].astype(_BF16), preferred_element_type=_F32))
    return xv + jnp.concatenate(ys, axis=1) * ps_ref[...]


def kernel(x, norm_mix_g, norm_ffn_g, w_in_ab, fnet_map, conv_w, conv_b, conv_ln_g, conv_ln_b,
           w_out_ab, pool_map, pool_scale, ffn_w_gate, ffn_w_up, ffn_w_down, final_g):
    bsz, seq, d = x.shape
    tokens = bsz * seq
    row = lambda v: v.reshape(1, -1)
    bf = lambda v: v.astype(_BF16)

    a_perm, u = _in_proj(x, row(norm_mix_g[0]), w_in_ab[0])
    ya = _fnet(a_perm, _dft_constants(seq), bf(fnet_map[0]))
    u2d = u.reshape(tokens, B_WIDTH)
    conv_p = (conv_w[0], row(conv_b[0]), row(conv_ln_g[0]), row(conv_ln_b[0]))
    mixer = (ya.reshape(tokens, A_WIDTH), _conv_first(u2d, *conv_p), u2d, *conv_p, bf(w_out_ab[0]))
    x2 = _ffn(x.reshape(tokens, d), row(norm_ffn_g[0]), ffn_w_gate, ffn_w_up, ffn_w_down, 0,
              mixer=mixer, seq=seq)

    pool = (row(norm_mix_g[1]), pool_map[0], row(pool_scale[0]))
    out = _ffn(x2, row(norm_ffn_g[1]), ffn_w_gate, ffn_w_up, ffn_w_down, 1, pool=pool,
               final_g=row(final_g), seq=seq, tile_rows=FFN_FINAL_TILE)
    return out.reshape(bsz, seq, d)
```

```python
import functools
import math

import jax
import jax.numpy as jnp
import numpy as np
from jax.experimental import pallas as pl
from jax.experimental.pallas import tpu as pltpu

RMS_EPS = 1e-6
LN_EPS = 1e-5

A_HEADS = 4
HEAD_DIM = 128
A_WIDTH = A_HEADS * HEAD_DIM
B_WIDTH = 512
CONV_WIDTH = 31
CONV_PAD = CONV_WIDTH // 2
POOL_WINDOWS = (2, 4, 8, 16)
POOL_HALO = 8

LANES = 128
SUBLANES = 8

DFT_RADIX = 8

V7X_VMEM_LIMIT_BYTES = 56 * 1024 * 1024

TOKEN_TILE = 512
IN_PROJ_TILE = 1024
FFN_FINAL_TILE = 1024
FF_PREP_CHUNK = 256
BFLY_ROW_TILE = 16
CONV_HALO = 16
CONV_ROW_TILE = 64
CONV_UNITS_PER_DOT = 2
BF16_SUBLANES = 16

_F32 = jnp.float32
_BF16 = jnp.bfloat16


def _resident(shape):
    nd = len(shape)
    return pl.BlockSpec(shape, lambda *_: (0,) * nd, pipeline_mode=pl.Buffered(1))


def _rmsnorm(xv, g):
    inv = jax.lax.rsqrt(jnp.mean(xv * xv, axis=-1, keepdims=True) + RMS_EPS)
    return (xv * inv) * g


def _one_plus_tanh(half_v):
    return 1.0 + jnp.tanh(half_v)


def _prep_ffn_weights(wg_ref, wu_ref, wd_ref, wgu_dst, wd_dst):
    cw = wg_ref.shape[1]
    for b in range(cw // LANES):
        src = slice(b * LANES, (b + 1) * LANES)
        wgu_dst[:, 2 * b * LANES:(2 * b + 1) * LANES] = (wg_ref[:, src] * 0.5).astype(_BF16)
        wgu_dst[:, (2 * b + 1) * LANES:(2 * b + 2) * LANES] = wu_ref[:, src].astype(_BF16)
    wd_dst[...] = wd_ref[...].astype(_BF16)


def _prep_specs(wg_all, wu_all, wd_all, layer, chunk_of):
    _, d, ff = wg_all.shape
    cw = FF_PREP_CHUNK
    in_specs = [
        pl.BlockSpec((None, d, cw), lambda *idx: (layer, 0, chunk_of(*idx))),
        pl.BlockSpec((None, d, cw), lambda *idx: (layer, 0, chunk_of(*idx))),
        pl.BlockSpec((None, cw, d), lambda *idx: (layer, chunk_of(*idx), 0)),
    ]
    out_specs = [
        pl.BlockSpec((None, d, 2 * cw), lambda *idx: (chunk_of(*idx), 0, 0)),
        pl.BlockSpec((cw, d), lambda *idx: (chunk_of(*idx), 0)),
    ]
    out_shape = [jax.ShapeDtypeStruct((ff // cw, d, 2 * cw), _BF16),
                 jax.ShapeDtypeStruct((ff, d), _BF16)]
    return in_specs, out_specs, out_shape


def _in_proj_kernel(x_ref, g_ref, w_ref, wg_ref, wu_ref, wd_ref, a_ref, u_ref, wgu_out, wd_out,
                    a_scr, w_scr, *, n_chunks):
    step = pl.program_id(0) * pl.num_programs(1) + pl.program_id(1)

    @pl.when(step < n_chunks)
    def _():
        _prep_ffn_weights(wg_ref, wu_ref, wd_ref, wgu_out, wd_out)

    @pl.when(step == 0)
    def _():
        w_scr[:, :A_WIDTH] = w_ref[:, :A_WIDTH].astype(_BF16)
        w_scr[:, A_WIDTH:] = (w_ref[:, A_WIDTH:] * 0.5).astype(_BF16)

    h = _rmsnorm(x_ref[...], g_ref[...]).astype(_BF16)
    p = jnp.dot(h, w_scr[...], preferred_element_type=_F32)
    rows = a_scr.shape[1] // DFT_RADIX
    for lt in range(A_WIDTH // LANES):
        lanes = slice(lt * LANES, (lt + 1) * LANES)
        a_scr[lt] = p[:, lanes]
        for jr in range(DFT_RADIX):
            a_ref[jr, :, lanes] = a_scr[lt, pl.ds(jr, rows, stride=DFT_RADIX), :].astype(_BF16)
    half_v = p[:, A_WIDTH:A_WIDTH + B_WIDTH]
    half_gate = p[:, A_WIDTH + B_WIDTH:]
    u_ref[...] = half_v * _one_plus_tanh(half_gate)


def _in_proj(x, g, w_in, ffn_weights):
    bsz, seq, d = x.shape
    tm = IN_PROJ_TILE
    inner = seq // DFT_RADIX
    steps_per_batch = seq // tm
    n_chunks = ffn_weights[0].shape[2] // FF_PREP_CHUNK
    assert bsz * steps_per_batch >= n_chunks
    chunk_of = lambda b, i: jnp.minimum(b * steps_per_batch + i, n_chunks - 1)
    prep_in, prep_out, prep_shape = _prep_specs(*ffn_weights, 0, chunk_of)
    return pl.pallas_call(
        functools.partial(_in_proj_kernel, n_chunks=n_chunks),
        grid=(bsz, steps_per_batch),
        in_specs=[
            pl.BlockSpec((None, tm, d), lambda b, i: (b, i, 0)),
            _resident((1, d)),
            _resident(w_in.shape),
        ] + prep_in,
        out_specs=[
            pl.BlockSpec((None, DFT_RADIX, tm // DFT_RADIX, A_WIDTH), lambda b, i: (b, 0, i, 0)),
            pl.BlockSpec((None, tm, B_WIDTH), lambda b, i: (b, i, 0)),
        ] + prep_out,
        out_shape=[
            jax.ShapeDtypeStruct((bsz, DFT_RADIX, inner, A_WIDTH), _BF16),
            jax.ShapeDtypeStruct((bsz, seq, B_WIDTH), _F32),
        ] + prep_shape,
        scratch_shapes=[pltpu.VMEM((A_WIDTH // LANES, tm, LANES), _F32),
                        pltpu.VMEM(w_in.shape, _BF16)],
        compiler_params=pltpu.CompilerParams(
            dimension_semantics=("arbitrary", "arbitrary"),
            vmem_limit_bytes=V7X_VMEM_LIMIT_BYTES),
        name="in_proj",
    )(x, g, w_in, *ffn_weights)


def _cadd(a, b):
    return (a[0] + b[0], a[1] + b[1])


def _csub(a, b):
    return (a[0] - b[0], a[1] - b[1])


def _dft4(a0, a1, a2, a3):
    s0, s1 = _cadd(a0, a2), _csub(a0, a2)
    s2, s3 = _cadd(a1, a3), _csub(a1, a3)
    return (_cadd(s0, s2), (s1[0] + s3[1], s1[1] - s3[0]),
            _csub(s0, s2), (s1[0] - s3[1], s1[1] + s3[0]))


def _mul_w8(k, z):
    r, i = z
    h = math.sqrt(0.5)
    if k == 0:
        return z
    if k == 1:
        return (h * (r + i), h * (i - r))
    if k == 2:
        return (i, -r)
    return (h * (i - r), -h * (r + i))


def _fnet_kernel(a_ref, cs_ref, twc_ref, tws_ref, cdsd_ref, map_ref, y_ref, yr_scr, yi_scr):
    seq = y_ref.shape[0]
    inner = seq // DFT_RADIX

    for jr in range(DFT_RADIX):
        yy = jnp.dot(cs_ref[...], a_ref[jr], preferred_element_type=_F32)
        yr_scr[jr * inner:(jr + 1) * inner, :] = yy[:inner]
        yi_scr[jr * inner:(jr + 1) * inner, :] = yy[inner:]

    rt = BFLY_ROW_TILE

    def bfly(c, carry):
        r0 = pl.multiple_of(c * rt, rt)
        for lt in range(A_WIDTH // LANES):
            lanes = slice(lt * LANES, (lt + 1) * LANES)
            z = []
            for jr in range(DFT_RADIX):
                rows = pl.ds(jr * inner + r0, rt)
                yr = yr_scr[rows, lanes]
                yi = yi_scr[rows, lanes]
                if jr == 0:
                    z.append((yr, yi))
                else:
                    tc = twc_ref[rows, :]
                    ts = tws_ref[rows, :]
                    z.append((yr * tc + yi * ts, yi * tc - yr * ts))
            ev = _dft4(z[0], z[2], z[4], z[6])
            od = _dft4(z[1], z[3], z[5], z[7])
            for k in range(4):
                w = _mul_w8(k, od[k])
                lo = _cadd(ev[k], w)
                hi = _csub(ev[k], w)
                rows_lo = pl.ds(k * inner + r0, rt)
                rows_hi = pl.ds((k + 4) * inner + r0, rt)
                yr_scr[rows_lo, lanes] = lo[0]
                yi_scr[rows_lo, lanes] = lo[1]
                yr_scr[rows_hi, lanes] = hi[0]
                yi_scr[rows_hi, lanes] = hi[1]
        return carry

    jax.lax.fori_loop(0, inner // rt, bfly, 0)

    for hd in range(A_HEADS):
        lanes = slice(hd * HEAD_DIM, (hd + 1) * HEAD_DIM)
        lhs = jnp.concatenate([yr_scr[:, lanes].astype(_BF16),
                               yi_scr[:, lanes].astype(_BF16)], axis=1)
        f = jnp.dot(lhs, cdsd_ref[...], preferred_element_type=_F32)
        ya = jnp.dot(f.astype(_BF16), map_ref[hd], preferred_element_type=_F32)
        y_ref[:, lanes] = ya.astype(_BF16)


def _fnet(a_perm, consts, fmap):
    bsz, _, inner, _ = a_perm.shape
    seq = inner * DFT_RADIX
    cs, twc, tws, cdsd = consts
    return pl.pallas_call(
        _fnet_kernel,
        grid=(bsz,),
        in_specs=[
            pl.BlockSpec((None, DFT_RADIX, inner, A_WIDTH), lambda b: (b, 0, 0, 0)),
            _resident(cs.shape), _resident(twc.shape), _resident(tws.shape),
            _resident(cdsd.shape), _resident(fmap.shape),
        ],
        out_specs=pl.BlockSpec((None, seq, A_WIDTH), lambda b: (b, 0, 0)),
        out_shape=jax.ShapeDtypeStruct((bsz, seq, A_WIDTH), _BF16),
        scratch_shapes=[
            pltpu.VMEM((seq, A_WIDTH), _F32),
            pltpu.VMEM((seq, A_WIDTH), _F32),
        ],
        compiler_params=pltpu.CompilerParams(
            dimension_semantics=("arbitrary",),
            vmem_limit_bytes=V7X_VMEM_LIMIT_BYTES),
        name="fnet",
    )(a_perm, cs, twc, tws, cdsd, fmap)


def _dft_constants(seq):
    inner = seq // DFT_RADIX
    k = np.arange(inner, dtype=np.float64)
    ang = 2.0 * np.pi * np.outer(k, k) / inner
    cs = np.concatenate([np.cos(ang), -np.sin(ang)], axis=0)
    jr = np.arange(DFT_RADIX, dtype=np.float64)[:, None]
    tw = 2.0 * np.pi * (jr * k[None, :]) / seq
    twc = np.repeat(np.cos(tw).reshape(seq, 1), LANES, axis=1)
    tws = np.repeat(np.sin(tw).reshape(seq, 1), LANES, axis=1)
    d = np.arange(HEAD_DIM, dtype=np.float64)
    angd = 2.0 * np.pi * np.outer(d, d) / HEAD_DIM
    scale = 1.0 / math.sqrt(seq * HEAD_DIM)
    cdsd = np.concatenate([np.cos(angd), np.sin(angd)], axis=0) * scale
    return (jnp.asarray(cs, _F32).astype(_BF16), jnp.asarray(twc, _F32), jnp.asarray(tws, _F32),
            jnp.asarray(cdsd, _F32).astype(_BF16))


def _conv_fill_window(tile, tiles_per_seq, main_ref, prev_ref, next_ref, win_scr):
    tm = main_ref.shape[0]
    halo = CONV_HALO
    pos = jnp.zeros((halo, 1), jnp.int32) + tile % tiles_per_seq
    win_scr[0:halo, :] = jnp.where(pos == 0, 0.0, prev_ref[...])
    win_scr[halo:halo + tm, :] = main_ref[...]
    win_scr[halo + tm:, :] = jnp.where(pos == tiles_per_seq - 1, 0.0, next_ref[...])


def _conv_unit(rc, lt, win_scr, cw_ref, cb_ref, lg_ref, lb_ref, out_ref):
    ct = CONV_ROW_TILE
    halo = CONV_HALO
    r0 = rc * ct
    lanes = slice(lt * LANES, (lt + 1) * LANES)
    first = halo - CONV_PAD
    span = ct + 2 * halo
    win = win_scr[r0:r0 + span, lanes]
    acc = None
    for s in range(SUBLANES):
        rot = win if s == 0 else pltpu.roll(win, span - s, axis=0)
        for q in range((first + CONV_WIDTH - 1) // SUBLANES + 1):
            k = SUBLANES * q + s - first
            if 0 <= k < CONV_WIDTH:
                term = rot[SUBLANES * q:SUBLANES * q + ct] * cw_ref[k:k + 1, lanes]
                acc = term if acc is None else acc + term
    cv = acc + cb_ref[:, lanes]
    mu = jnp.mean(cv, axis=-1, keepdims=True)
    dv = cv - mu
    var = jnp.mean(dv * dv, axis=-1, keepdims=True)
    half_yn = (dv * jax.lax.rsqrt(var + LN_EPS)) * (lg_ref[:, lanes] * 0.5) + lb_ref[:, lanes] * 0.5
    y = half_yn * _one_plus_tanh(half_yn)
    out_ref[r0:r0 + ct, lanes] = y.astype(_BF16)
    return y[ct - SUBLANES:, :]


def _tied(tile, dep):
    zero = pltpu.bitcast(dep, jnp.uint32)
    zero = jax.lax.shift_right_logical(jax.lax.shift_right_logical(zero, jnp.uint32(16)), jnp.uint32(16))
    zero = pltpu.bitcast(zero, _F32)
    return tile + jnp.concatenate([zero, zero], axis=0)


def _conv_units(tm):
    return [(rc, lt) for rc in range(tm // CONV_ROW_TILE) for lt in range(B_WIDTH // LANES)]


def _conv_scratch(tm):
    return [pltpu.VMEM((tm + 2 * CONV_HALO, B_WIDTH), _F32)]


def _conv_first_kernel(main_ref, next_ref, cw_ref, cb_ref, lg_ref, lb_ref, out_ref, win_scr):
    _conv_fill_window(0, 2, main_ref, next_ref, next_ref, win_scr)
    for rc, lt in _conv_units(main_ref.shape[0]):
        _conv_unit(rc, lt, win_scr, cw_ref, cb_ref, lg_ref, lb_ref, out_ref)


def _conv_first(u2d, conv_w, conv_b, ln_g, ln_b):
    tm = TOKEN_TILE
    return pl.pallas_call(
        _conv_first_kernel,
        grid=(1,),
        in_specs=[
            pl.BlockSpec((tm, B_WIDTH), lambda i: (0, 0)),
            pl.BlockSpec((CONV_HALO, B_WIDTH), lambda i: (tm // CONV_HALO, 0)),
            _resident(conv_w.shape), _resident(conv_b.shape), _resident(ln_g.shape),
            _resident(ln_b.shape),
        ],
        out_specs=pl.BlockSpec((tm, B_WIDTH), lambda i: (0, 0)),
        out_shape=jax.ShapeDtypeStruct((tm, B_WIDTH), _BF16),
        scratch_shapes=_conv_scratch(tm),
        compiler_params=pltpu.CompilerParams(
            dimension_semantics=("arbitrary",),
            vmem_limit_bytes=V7X_VMEM_LIMIT_BYTES),
        name="conv_first",
    )(u2d, u2d, conv_w, conv_b, ln_g, ln_b)


def _pool_tile(tile, tiles_per_seq, x_ref, prev_ref, next_ref, g_ref, pm_ref, ps_ref, hp_scr):
    tm, d = x_ref.shape
    seq = tm * tiles_per_seq
    n = tm + 2 * POOL_HALO
    g = g_ref[...]
    xv = x_ref[...]
    hm = _rmsnorm(xv, g)
    start = (tile % tiles_per_seq) * tm
    halo_iota = jax.lax.broadcasted_iota(jnp.int32, (POOL_HALO, 1), 0)
    prev_ok = (start - POOL_HALO + halo_iota) >= 0
    next_ok = (start + tm + halo_iota) < seq
    hp_scr[0:POOL_HALO, :] = jnp.where(prev_ok, _rmsnorm(prev_ref[...], g), 0.0)
    hp_scr[POOL_HALO:POOL_HALO + tm, :] = hm
    hp_scr[POOL_HALO + tm:, :] = jnp.where(next_ok, _rmsnorm(next_ref[...], g), 0.0)

    pos = start + jax.lax.broadcasted_iota(jnp.int32, (tm, LANES), 0)
    gd = d // len(POOL_WINDOWS)
    ys = []
    for gi, w in enumerate(POOL_WINDOWS):
        lanes = slice(gi * gd, (gi + 1) * gd)
        half = w // 2
        fwd = hp_scr[:, lanes]
        span = 1
        while span < half:
            fwd = fwd + pltpu.roll(fwd, n - span, axis=0)
            span *= 2
        centred = fwd + pltpu.roll(fwd, half, axis=0)
        win = centred[POOL_HALO:POOL_HALO + tm]
        cnt = jnp.minimum(pos + (w - half), seq) - jnp.maximum(pos - half, 0)
        inv = 1.0 / cnt.astype(_F32)
        pg = win * jnp.concatenate([inv] * (gd // LANES), axis=1) - hm[:, lanes]
        ys.append(jnp.dot(pg.astype(_BF16), pm_ref[gi].astype(_BF16), preferred_element_type=_F32))
    return xv + jnp.concatenate(ys, axis=1) * ps_ref[...]


def _ffn_kernel(*refs, has_mixer, has_pool, has_final, has_prep, tiles_per_seq, n_chunks):
    it = iter(refs)
    x_ref = next(it)
    if has_mixer:
        ya_ref = next(it)
        yb0_ref = next(it)
        u_refs = (next(it), next(it), next(it))
        conv_refs = (next(it), next(it), next(it), next(it))
        wo_ref = next(it)
    if has_pool:
        pool_refs = (next(it), next(it), next(it), next(it), next(it))
    g_ref = next(it)
    wgu_ref = next(it)
    wd_ref = next(it)
    if has_prep:
        prep_in = (next(it), next(it), next(it))
    if has_final:
        fg_ref = next(it)
    o_ref = next(it)
    if has_prep:
        prep_out = (next(it), next(it))
    h_scr = next(it)
    act_scr = next(it)
    if has_mixer:
        yb_scr = next(it)
        win_scr = next(it)
    if has_pool:
        hp_scr = next(it)

    i = pl.program_id(0)
    if has_prep:
        @pl.when(i < n_chunks)
        def _():
            _prep_ffn_weights(*prep_in, *prep_out)

    tm = x_ref.shape[0]
    if has_pool:
        xv = _pool_tile(i, tiles_per_seq, x_ref, *pool_refs, hp_scr)
    else:
        xv = x_ref[...]
    conv_some = lambda n_units: None
    if has_mixer:
        n = pl.num_programs(0)
        slot = i % 2

        @pl.when(i == 0)
        def _():
            yb_scr[0] = yb0_ref[...]

        yb = yb_scr[slot]
        _conv_fill_window(jnp.minimum(i + 1, n - 1), tiles_per_seq, *u_refs, win_scr)
        pending = iter(_conv_units(tm))

        def conv_some(n_units):
            dep = None
            for _ in range(n_units):
                unit = next(pending, None)
                if unit is not None:
                    dep = _conv_unit(*unit, win_scr, *conv_refs, yb_scr.at[1 - slot])
            return dep

        conv_some(len(_conv_units(tm)) - n_chunks * CONV_UNITS_PER_DOT)
        yab = jnp.concatenate([ya_ref[...], yb], axis=1)
        xv = xv + jnp.dot(yab, wo_ref[...], preferred_element_type=_F32)
    h = _rmsnorm(xv, g_ref[...])
    h_scr[...] = h.astype(_BF16)
    h_tile = h[:BF16_SUBLANES, :LANES]

    def load_h(dep):
        if dep is not None:
            h_scr[:BF16_SUBLANES, :LANES] = _tied(h_tile, dep).astype(_BF16)
        return h_scr[...]

    gu_width = wgu_ref.shape[2]
    for c in range(n_chunks):
        dep = conv_some(CONV_UNITS_PER_DOT)
        gu = jnp.dot(load_h(dep), wgu_ref[c], preferred_element_type=_F32)
        for b0 in range(0, gu_width, 2 * LANES):
            half_gate = gu[:, b0:b0 + LANES]
            up = gu[:, b0 + LANES:b0 + 2 * LANES]
            f0 = (c * gu_width + b0) // 2
            act = (half_gate * up) * _one_plus_tanh(half_gate)
            act_scr[:, f0:f0 + LANES] = act.astype(_BF16)
    acc = xv + jnp.dot(act_scr[...], wd_ref[...], preferred_element_type=_F32)
    if has_final:
        acc = _rmsnorm(acc, fg_ref[...])
    o_ref[...] = acc


def _ffn(x2d, g, wgu, wd, mixer=None, pool=None, final_g=None, prep=None, seq=None,
         tile_rows=TOKEN_TILE):
    tokens, d = x2d.shape
    n_chunks, _, _ = wgu.shape
    ff = wd.shape[0]
    tm = tile_rows
    tiles_per_seq = seq // tm
    n_tiles = tokens // tm
    has_mixer = mixer is not None
    has_pool = pool is not None
    has_final = final_g is not None
    has_prep = prep is not None
    assert n_tiles >= n_chunks
    args = [x2d]
    specs = [pl.BlockSpec((tm, d), lambda i: (i, 0))]
    out_specs = [pl.BlockSpec((tm, d), lambda i: (i, 0))]
    out_shape = [jax.ShapeDtypeStruct((tokens, d), _F32)]
    scratch = [pltpu.VMEM((tm, d), _BF16), pltpu.VMEM((tm, ff), _BF16)]
    if has_mixer:
        ya2d, yb0, u2d, conv_w, conv_b, ln_g, ln_b, w_out = mixer
        hb = tm // CONV_HALO
        n_halo = tokens // CONV_HALO
        nxt = lambda i: jnp.minimum(i + 1, n_tiles - 1)
        args += [ya2d, yb0, u2d, u2d, u2d, conv_w, conv_b, ln_g, ln_b, w_out]
        specs += [
            pl.BlockSpec((tm, A_WIDTH), lambda i: (i, 0)),
            _resident(yb0.shape),
            pl.BlockSpec((tm, B_WIDTH), lambda i: (nxt(i), 0)),
            pl.BlockSpec((CONV_HALO, B_WIDTH), lambda i: (jnp.maximum(nxt(i) * hb - 1, 0), 0)),
            pl.BlockSpec((CONV_HALO, B_WIDTH),
                         lambda i: (jnp.minimum((nxt(i) + 1) * hb, n_halo - 1), 0)),
            _resident(conv_w.shape), _resident(conv_b.shape), _resident(ln_g.shape),
            _resident(ln_b.shape), _resident(w_out.shape),
        ]
        scratch += [pltpu.VMEM((2, tm, B_WIDTH), _BF16)] + _conv_scratch(tm)
    if has_pool:
        g_mix, pool_map, pool_scale = pool
        hb = tm // POOL_HALO
        n_halo = tokens // POOL_HALO
        args += [x2d, x2d, g_mix, pool_map, pool_scale]
        specs += [
            pl.BlockSpec((POOL_HALO, d), lambda i: (jnp.maximum(i * hb - 1, 0), 0)),
            pl.BlockSpec((POOL_HALO, d), lambda i: (jnp.minimum((i + 1) * hb, n_halo - 1), 0)),
            _resident(g_mix.shape), _resident(pool_map.shape), _resident(pool_scale.shape),
        ]
        scratch += [pltpu.VMEM((tm + 2 * POOL_HALO, d), _F32)]
    args += [g, wgu, wd]
    specs += [_resident(g.shape), _resident(wgu.shape), _resident(wd.shape)]
    if has_prep:
        prep_in, prep_out, prep_shape = _prep_specs(
            *prep, lambda i: jnp.minimum(i, n_chunks - 1))
        args += list(prep[:3])
        specs += prep_in
        out_specs += prep_out
        out_shape += prep_shape
    if has_final:
        args.append(final_g)
        specs.append(_resident(final_g.shape))
    outs = pl.pallas_call(
        functools.partial(_ffn_kernel, has_mixer=has_mixer, has_pool=has_pool, has_final=has_final,
                          has_prep=has_prep, tiles_per_seq=tiles_per_seq, n_chunks=n_chunks),
        grid=(n_tiles,),
        in_specs=specs,
        out_specs=out_specs,
        out_shape=out_shape,
        scratch_shapes=scratch,
        compiler_params=pltpu.CompilerParams(
            dimension_semantics=("arbitrary",),
            vmem_limit_bytes=V7X_VMEM_LIMIT_BYTES),
        name="ffn_mixer" if has_mixer else "ffn_final",
    )(*args)
    return outs if has_prep else outs[0]


def kernel(x, norm_mix_g, norm_ffn_g, w_in_ab, fnet_map, conv_w, conv_b, conv_ln_g, conv_ln_b,
           w_out_ab, pool_map, pool_scale, ffn_w_gate, ffn_w_up, ffn_w_down, final_g):
    bsz, seq, d = x.shape
    tokens = bsz * seq
    row = lambda v: v.reshape(1, -1)
    bf = lambda v: v.astype(_BF16)

    ffn_weights = (ffn_w_gate, ffn_w_up, ffn_w_down)
    a_perm, u, wgu0, wd0 = _in_proj(x, row(norm_mix_g[0]), w_in_ab[0], ffn_weights)
    ya = _fnet(a_perm, _dft_constants(seq), bf(fnet_map[0]))
    u2d = u.reshape(tokens, B_WIDTH)
    conv_p = (conv_w[0], row(conv_b[0]), row(conv_ln_g[0]), row(conv_ln_b[0]))
    mixer = (ya.reshape(tokens, A_WIDTH), _conv_first(u2d, *conv_p), u2d, *conv_p, bf(w_out_ab[0]))
    x2, wgu1, wd1 = _ffn(x.reshape(tokens, d), row(norm_ffn_g[0]), wgu0, wd0, mixer=mixer,
                         prep=(*ffn_weights, 1), seq=seq)

    pool = (row(norm_mix_g[1]), pool_map[0], row(pool_scale[0]))
    out = _ffn(x2, row(norm_ffn_g[1]), wgu1, wd1, pool=pool, final_g=row(final_g), seq=seq,
               tile_rows=FFN_FINAL_TILE)
    return out.reshape(bsz, seq, d)
```

```python
import functools
import math

import jax
import jax.numpy as jnp
import numpy as np
from jax.experimental import pallas as pl
from jax.experimental.pallas import tpu as pltpu

RMS_EPS = 1e-6
LN_EPS = 1e-5

A_HEADS = 4
HEAD_DIM = 128
A_WIDTH = A_HEADS * HEAD_DIM
B_WIDTH = 512
CONV_WIDTH = 31
CONV_PAD = CONV_WIDTH // 2
POOL_WINDOWS = (2, 4, 8, 16)
POOL_HALO = 8

LANES = 128
SUBLANES = 8

DFT_RADIX = 8

V7X_VMEM_LIMIT_BYTES = 56 * 1024 * 1024

TOKEN_TILE = 512
IN_PROJ_TILE = 1024
FFN_FINAL_TILE = 1024
FF_PREP_CHUNK = 256
BFLY_ROW_TILE = 16
CONV_HALO = 16
CONV_ROW_TILE = 64
CONV_UNITS_PER_DOT = 2
BF16_SUBLANES = 16

_F32 = jnp.float32
_BF16 = jnp.bfloat16


def _resident(shape):
    nd = len(shape)
    return pl.BlockSpec(shape, lambda *_: (0,) * nd, pipeline_mode=pl.Buffered(1))


def _rms_scale(xv):
    return xv * jax.lax.rsqrt(jnp.mean(xv * xv, axis=-1, keepdims=True) + RMS_EPS)


def _rmsnorm(xv, g):
    return _rms_scale(xv) * g


def _one_plus_tanh(half_v):
    return 1.0 + jnp.tanh(half_v)


def _prep_ffn_weights(gcol_ref, wg_ref, wu_ref, wd_ref, wgu_dst, wd_dst):
    cw = wg_ref.shape[1]
    gain = gcol_ref[...]
    half_gain = gain * 0.5
    for b in range(cw // LANES):
        src = slice(b * LANES, (b + 1) * LANES)
        wgu_dst[:, 2 * b * LANES:(2 * b + 1) * LANES] = (wg_ref[:, src] * half_gain).astype(_BF16)
        wgu_dst[:, (2 * b + 1) * LANES:(2 * b + 2) * LANES] = (wu_ref[:, src] * gain).astype(_BF16)
    wd_dst[...] = wd_ref[...].astype(_BF16)


def _prep_specs(gcol, wg_all, wu_all, wd_all, layer, chunk_of):
    _, d, ff = wg_all.shape
    cw = FF_PREP_CHUNK
    in_specs = [
        _resident(gcol.shape),
        pl.BlockSpec((None, d, cw), lambda *idx: (layer, 0, chunk_of(*idx))),
        pl.BlockSpec((None, d, cw), lambda *idx: (layer, 0, chunk_of(*idx))),
        pl.BlockSpec((None, cw, d), lambda *idx: (layer, chunk_of(*idx), 0)),
    ]
    out_specs = [
        pl.BlockSpec((None, d, 2 * cw), lambda *idx: (chunk_of(*idx), 0, 0)),
        pl.BlockSpec((cw, d), lambda *idx: (chunk_of(*idx), 0)),
    ]
    out_shape = [jax.ShapeDtypeStruct((ff // cw, d, 2 * cw), _BF16),
                 jax.ShapeDtypeStruct((ff, d), _BF16)]
    return in_specs, out_specs, out_shape


def _in_proj_kernel(x_ref, gcol_ref, w_ref, wo_ref, fgcol_ref, wg_ref, wu_ref, wd_ref,
                    a_ref, u_ref, wo_out, wgu_out, wd_out, a_scr, w_scr, *, n_chunks):
    step = pl.program_id(0) * pl.num_programs(1) + pl.program_id(1)

    @pl.when(step < n_chunks)
    def _():
        _prep_ffn_weights(fgcol_ref, wg_ref, wu_ref, wd_ref, wgu_out, wd_out)

    @pl.when(step == 0)
    def _():
        gain = gcol_ref[...]
        w_scr[:, :A_WIDTH] = (w_ref[:, :A_WIDTH] * gain).astype(_BF16)
        w_scr[:, A_WIDTH:] = (w_ref[:, A_WIDTH:] * (gain * 0.5)).astype(_BF16)
        wo_out[...] = wo_ref[...].astype(_BF16)

    h = _rms_scale(x_ref[...]).astype(_BF16)
    p = jnp.dot(h, w_scr[...], preferred_element_type=_F32)
    rows = a_scr.shape[1] // DFT_RADIX
    for lt in range(A_WIDTH // LANES):
        lanes = slice(lt * LANES, (lt + 1) * LANES)
        a_scr[lt] = p[:, lanes]
        for jr in range(DFT_RADIX):
            a_ref[jr, :, lanes] = a_scr[lt, pl.ds(jr, rows, stride=DFT_RADIX), :].astype(_BF16)
    half_v = p[:, A_WIDTH:A_WIDTH + B_WIDTH]
    half_gate = p[:, A_WIDTH + B_WIDTH:]
    u_ref[...] = half_v * _one_plus_tanh(half_gate)


def _in_proj(x, gcol, w_in, w_out, ffn_gcol, ffn_weights):
    bsz, seq, d = x.shape
    tm = IN_PROJ_TILE
    inner = seq // DFT_RADIX
    steps_per_batch = seq // tm
    n_chunks = ffn_weights[0].shape[2] // FF_PREP_CHUNK
    assert bsz * steps_per_batch >= n_chunks
    chunk_of = lambda b, i: jnp.minimum(b * steps_per_batch + i, n_chunks - 1)
    prep_in, prep_out, prep_shape = _prep_specs(ffn_gcol, *ffn_weights, 0, chunk_of)
    return pl.pallas_call(
        functools.partial(_in_proj_kernel, n_chunks=n_chunks),
        grid=(bsz, steps_per_batch),
        in_specs=[
            pl.BlockSpec((None, tm, d), lambda b, i: (b, i, 0)),
            _resident(gcol.shape),
            _resident(w_in.shape),
            _resident(w_out.shape),
        ] + prep_in,
        out_specs=[
            pl.BlockSpec((None, DFT_RADIX, tm // DFT_RADIX, A_WIDTH), lambda b, i: (b, 0, i, 0)),
            pl.BlockSpec((None, tm, B_WIDTH), lambda b, i: (b, i, 0)),
            _resident(w_out.shape),
        ] + prep_out,
        out_shape=[
            jax.ShapeDtypeStruct((bsz, DFT_RADIX, inner, A_WIDTH), _BF16),
            jax.ShapeDtypeStruct((bsz, seq, B_WIDTH), _F32),
            jax.ShapeDtypeStruct(w_out.shape, _BF16),
        ] + prep_shape,
        scratch_shapes=[pltpu.VMEM((A_WIDTH // LANES, tm, LANES), _F32),
                        pltpu.VMEM(w_in.shape, _BF16)],
        compiler_params=pltpu.CompilerParams(
            dimension_semantics=("arbitrary", "arbitrary"),
            vmem_limit_bytes=V7X_VMEM_LIMIT_BYTES),
        name="in_proj",
    )(x, gcol, w_in, w_out, ffn_gcol, *ffn_weights)


def _cadd(a, b):
    return (a[0] + b[0], a[1] + b[1])


def _csub(a, b):
    return (a[0] - b[0], a[1] - b[1])


def _dft4(a0, a1, a2, a3):
    s0, s1 = _cadd(a0, a2), _csub(a0, a2)
    s2, s3 = _cadd(a1, a3), _csub(a1, a3)
    return (_cadd(s0, s2), (s1[0] + s3[1], s1[1] - s3[0]),
            _csub(s0, s2), (s1[0] - s3[1], s1[1] + s3[0]))


def _mul_w8(k, z):
    r, i = z
    h = math.sqrt(0.5)
    if k == 0:
        return z
    if k == 1:
        return (h * (r + i), h * (i - r))
    if k == 2:
        return (i, -r)
    return (h * (i - r), -h * (r + i))


def _fnet_kernel(a_ref, cs_ref, twc_ref, tws_ref, cdsd_ref, map_ref, y_ref, yr_scr, yi_scr):
    seq = y_ref.shape[0]
    inner = seq // DFT_RADIX

    for jr in range(DFT_RADIX):
        yy = jnp.dot(cs_ref[...], a_ref[jr], preferred_element_type=_F32)
        yr_scr[jr * inner:(jr + 1) * inner, :] = yy[:inner]
        yi_scr[jr * inner:(jr + 1) * inner, :] = yy[inner:]

    rt = BFLY_ROW_TILE

    def bfly(c, carry):
        r0 = pl.multiple_of(c * rt, rt)
        for lt in range(A_WIDTH // LANES):
            lanes = slice(lt * LANES, (lt + 1) * LANES)
            z = []
            for jr in range(DFT_RADIX):
                rows = pl.ds(jr * inner + r0, rt)
                yr = yr_scr[rows, lanes]
                yi = yi_scr[rows, lanes]
                if jr == 0:
                    z.append((yr, yi))
                else:
                    tc = twc_ref[rows, :]
                    ts = tws_ref[rows, :]
                    z.append((yr * tc + yi * ts, yi * tc - yr * ts))
            ev = _dft4(z[0], z[2], z[4], z[6])
            od = _dft4(z[1], z[3], z[5], z[7])
            for k in range(4):
                w = _mul_w8(k, od[k])
                lo = _cadd(ev[k], w)
                hi = _csub(ev[k], w)
                rows_lo = pl.ds(k * inner + r0, rt)
                rows_hi = pl.ds((k + 4) * inner + r0, rt)
                yr_scr[rows_lo, lanes] = lo[0]
                yi_scr[rows_lo, lanes] = lo[1]
                yr_scr[rows_hi, lanes] = hi[0]
                yi_scr[rows_hi, lanes] = hi[1]
        return carry

    jax.lax.fori_loop(0, inner // rt, bfly, 0)

    for hd in range(A_HEADS):
        lanes = slice(hd * HEAD_DIM, (hd + 1) * HEAD_DIM)
        lhs = jnp.concatenate([yr_scr[:, lanes].astype(_BF16),
                               yi_scr[:, lanes].astype(_BF16)], axis=1)
        f = jnp.dot(lhs, cdsd_ref[...], preferred_element_type=_F32)
        ya = jnp.dot(f.astype(_BF16), map_ref[hd].astype(_BF16), preferred_element_type=_F32)
        y_ref[:, lanes] = ya.astype(_BF16)


def _fnet(a_perm, consts, fmap):
    bsz, _, inner, _ = a_perm.shape
    seq = inner * DFT_RADIX
    cs, twc, tws, cdsd = consts
    return pl.pallas_call(
        _fnet_kernel,
        grid=(bsz,),
        in_specs=[
            pl.BlockSpec((None, DFT_RADIX, inner, A_WIDTH), lambda b: (b, 0, 0, 0)),
            _resident(cs.shape), _resident(twc.shape), _resident(tws.shape),
            _resident(cdsd.shape), _resident(fmap.shape),
        ],
        out_specs=pl.BlockSpec((None, seq, A_WIDTH), lambda b: (b, 0, 0)),
        out_shape=jax.ShapeDtypeStruct((bsz, seq, A_WIDTH), _BF16),
        scratch_shapes=[
            pltpu.VMEM((seq, A_WIDTH), _F32),
            pltpu.VMEM((seq, A_WIDTH), _F32),
        ],
        compiler_params=pltpu.CompilerParams(
            dimension_semantics=("arbitrary",),
            vmem_limit_bytes=V7X_VMEM_LIMIT_BYTES),
        name="fnet",
    )(a_perm, cs, twc, tws, cdsd, fmap)


def _dft_constants(seq):
    inner = seq // DFT_RADIX
    k = np.arange(inner, dtype=np.float64)
    ang = 2.0 * np.pi * np.outer(k, k) / inner
    cs = np.concatenate([np.cos(ang), -np.sin(ang)], axis=0)
    jr = np.arange(DFT_RADIX, dtype=np.float64)[:, None]
    tw = 2.0 * np.pi * (jr * k[None, :]) / seq
    twc = np.repeat(np.cos(tw).reshape(seq, 1), LANES, axis=1)
    tws = np.repeat(np.sin(tw).reshape(seq, 1), LANES, axis=1)
    d = np.arange(HEAD_DIM, dtype=np.float64)
    angd = 2.0 * np.pi * np.outer(d, d) / HEAD_DIM
    scale = 1.0 / math.sqrt(seq * HEAD_DIM)
    cdsd = np.concatenate([np.cos(angd), np.sin(angd)], axis=0) * scale
    return (jnp.asarray(cs, _F32).astype(_BF16), jnp.asarray(twc, _F32), jnp.asarray(tws, _F32),
            jnp.asarray(cdsd, _F32).astype(_BF16))


def _conv_fill_window(tile, tiles_per_seq, main_ref, prev_ref, next_ref, win_scr):
    tm = main_ref.shape[0]
    halo = CONV_HALO
    pos = jnp.zeros((halo, 1), jnp.int32) + tile % tiles_per_seq
    win_scr[0:halo, :] = jnp.where(pos == 0, 0.0, prev_ref[...])
    win_scr[halo:halo + tm, :] = main_ref[...]
    win_scr[halo + tm:, :] = jnp.where(pos == tiles_per_seq - 1, 0.0, next_ref[...])


def _conv_unit(rc, lt, win_scr, cw_ref, cb_ref, lg_ref, lb_ref, out_ref):
    ct = CONV_ROW_TILE
    halo = CONV_HALO
    r0 = rc * ct
    lanes = slice(lt * LANES, (lt + 1) * LANES)
    first = halo - CONV_PAD
    span = ct + 2 * halo
    win = win_scr[r0:r0 + span, lanes]
    acc = None
    for s in range(SUBLANES):
        rot = win if s == 0 else pltpu.roll(win, span - s, axis=0)
        for q in range((first + CONV_WIDTH - 1) // SUBLANES + 1):
            k = SUBLANES * q + s - first
            if 0 <= k < CONV_WIDTH:
                term = rot[SUBLANES * q:SUBLANES * q + ct] * cw_ref[k:k + 1, lanes]
                acc = term if acc is None else acc + term
    cv = acc + cb_ref[:, lanes]
    mu = jnp.mean(cv, axis=-1, keepdims=True)
    dv = cv - mu
    var = jnp.mean(dv * dv, axis=-1, keepdims=True)
    half_yn = (dv * jax.lax.rsqrt(var + LN_EPS)) * (lg_ref[:, lanes] * 0.5) + lb_ref[:, lanes] * 0.5
    y = half_yn * _one_plus_tanh(half_yn)
    out_ref[r0:r0 + ct, lanes] = y.astype(_BF16)
    return y[ct - SUBLANES:, :]


def _tied(tile, dep):
    zero = pltpu.bitcast(dep, jnp.uint32)
    zero = jax.lax.shift_right_logical(jax.lax.shift_right_logical(zero, jnp.uint32(16)), jnp.uint32(16))
    zero = pltpu.bitcast(zero, _F32)
    return tile + jnp.concatenate([zero, zero], axis=0)


def _conv_units(tm):
    return [(rc, lt) for rc in range(tm // CONV_ROW_TILE) for lt in range(B_WIDTH // LANES)]


def _conv_scratch(tm):
    return [pltpu.VMEM((tm + 2 * CONV_HALO, B_WIDTH), _F32)]


def _conv_first_kernel(main_ref, next_ref, cw_ref, cb_ref, lg_ref, lb_ref, out_ref, win_scr):
    _conv_fill_window(0, 2, main_ref, next_ref, next_ref, win_scr)
    for rc, lt in _conv_units(main_ref.shape[0]):
        _conv_unit(rc, lt, win_scr, cw_ref, cb_ref, lg_ref, lb_ref, out_ref)


def _conv_first(u2d, conv_w, conv_b, ln_g, ln_b):
    tm = TOKEN_TILE
    return pl.pallas_call(
        _conv_first_kernel,
        grid=(1,),
        in_specs=[
            pl.BlockSpec((tm, B_WIDTH), lambda i: (0, 0)),
            pl.BlockSpec((CONV_HALO, B_WIDTH), lambda i: (tm // CONV_HALO, 0)),
            _resident(conv_w.shape), _resident(conv_b.shape), _resident(ln_g.shape),
            _resident(ln_b.shape),
        ],
        out_specs=pl.BlockSpec((tm, B_WIDTH), lambda i: (0, 0)),
        out_shape=jax.ShapeDtypeStruct((tm, B_WIDTH), _BF16),
        scratch_shapes=_conv_scratch(tm),
        compiler_params=pltpu.CompilerParams(
            dimension_semantics=("arbitrary",),
            vmem_limit_bytes=V7X_VMEM_LIMIT_BYTES),
        name="conv_first",
    )(u2d, u2d, conv_w, conv_b, ln_g, ln_b)


def _pool_tile(tile, tiles_per_seq, x_ref, prev_ref, next_ref, gcol_ref, pm_ref, ps_ref, hp_scr,
               pm_scr):
    tm, d = x_ref.shape
    seq = tm * tiles_per_seq
    n = tm + 2 * POOL_HALO
    gd = d // len(POOL_WINDOWS)

    @pl.when(tile == 0)
    def _():
        for gi in range(len(POOL_WINDOWS)):
            pm_scr[gi] = (pm_ref[gi] * gcol_ref[gi * gd:(gi + 1) * gd, :]).astype(_BF16)

    xv = x_ref[...]
    hm = _rms_scale(xv)
    start = (tile % tiles_per_seq) * tm
    halo_iota = jax.lax.broadcasted_iota(jnp.int32, (POOL_HALO, 1), 0)
    prev_ok = (start - POOL_HALO + halo_iota) >= 0
    next_ok = (start + tm + halo_iota) < seq
    hp_scr[0:POOL_HALO, :] = jnp.where(prev_ok, _rms_scale(prev_ref[...]), 0.0)
    hp_scr[POOL_HALO:POOL_HALO + tm, :] = hm
    hp_scr[POOL_HALO + tm:, :] = jnp.where(next_ok, _rms_scale(next_ref[...]), 0.0)

    edge_iota = jax.lax.broadcasted_iota(jnp.int32, (POOL_HALO, LANES), 0)
    ys = []
    for gi, w in enumerate(POOL_WINDOWS):
        lanes = slice(gi * gd, (gi + 1) * gd)
        half = w // 2
        fwd = hp_scr[:, lanes]
        span = 1
        while span < half:
            fwd = fwd + pltpu.roll(fwd, n - span, axis=0)
            span *= 2
        centred = fwd + pltpu.roll(fwd, half, axis=0)
        win = centred[POOL_HALO:POOL_HALO + tm]

        def mean_rows(r0):
            pos = start + r0 + edge_iota
            cnt = jnp.minimum(pos + (w - half), seq) - jnp.maximum(pos - half, 0)
            inv = 1.0 / cnt.astype(_F32)
            return win[r0:r0 + POOL_HALO] * jnp.concatenate([inv] * (gd // LANES), axis=1)

        mean = jnp.concatenate([mean_rows(0), win[POOL_HALO:tm - POOL_HALO] * (1.0 / w),
                                mean_rows(tm - POOL_HALO)], axis=0)
        pg = mean - hm[:, lanes]
        ys.append(jnp.dot(pg.astype(_BF16), pm_scr[gi], preferred_element_type=_F32))
    return xv + jnp.concatenate(ys, axis=1) * ps_ref[...]


def _ffn_kernel(*refs, has_mixer, has_pool, has_final, has_prep, tiles_per_seq, n_chunks):
    it = iter(refs)
    x_ref = next(it)
    if has_mixer:
        ya_ref = next(it)
        yb0_ref = next(it)
        u_refs = (next(it), next(it), next(it))
        conv_refs = (next(it), next(it), next(it), next(it))
        wo_ref = next(it)
    if has_pool:
        pool_refs = (next(it), next(it), next(it), next(it), next(it))
    wgu_ref = next(it)
    wd_ref = next(it)
    if has_prep:
        prep_in = (next(it), next(it), next(it), next(it))
    if has_final:
        fg_ref = next(it)
    o_ref = next(it)
    if has_prep:
        prep_out = (next(it), next(it))
    h_scr = next(it)
    act_scr = next(it)
    if has_mixer:
        yb_scr = next(it)
        win_scr = next(it)
    if has_pool:
        hp_scr = next(it)
        pm_scr = next(it)

    i = pl.program_id(0)
    if has_prep:
        @pl.when(i < n_chunks)
        def _():
            _prep_ffn_weights(*prep_in, *prep_out)

    tm = x_ref.shape[0]
    if has_pool:
        xv = _pool_tile(i, tiles_per_seq, x_ref, *pool_refs, hp_scr, pm_scr)
    else:
        xv = x_ref[...]
    conv_some = lambda n_units: None
    if has_mixer:
        n = pl.num_programs(0)
        slot = i % 2

        @pl.when(i == 0)
        def _():
            yb_scr[0] = yb0_ref[...]

        yb = yb_scr[slot]
        _conv_fill_window(jnp.minimum(i + 1, n - 1), tiles_per_seq, *u_refs, win_scr)
        pending = iter(_conv_units(tm))

        def conv_some(n_units):
            dep = None
            for _ in range(n_units):
                unit = next(pending, None)
                if unit is not None:
                    dep = _conv_unit(*unit, win_scr, *conv_refs, yb_scr.at[1 - slot])
            return dep

        conv_some(len(_conv_units(tm)) - n_chunks * CONV_UNITS_PER_DOT)
        yab = jnp.concatenate([ya_ref[...], yb], axis=1)
        xv = xv + jnp.dot(yab, wo_ref[...], preferred_element_type=_F32)
    h = _rms_scale(xv)
    h_scr[...] = h.astype(_BF16)
    h_tile = h[:BF16_SUBLANES, :LANES]

    def load_h(dep):
        if dep is not None:
            h_scr[:BF16_SUBLANES, :LANES] = _tied(h_tile, dep).astype(_BF16)
        return h_scr[...]

    gu_width = wgu_ref.shape[2]
    for c in range(n_chunks):
        dep = conv_some(CONV_UNITS_PER_DOT)
        gu = jnp.dot(load_h(dep), wgu_ref[c], preferred_element_type=_F32)
        for b0 in range(0, gu_width, 2 * LANES):
            half_gate = gu[:, b0:b0 + LANES]
            up = gu[:, b0 + LANES:b0 + 2 * LANES]
            f0 = (c * gu_width + b0) // 2
            act = (half_gate * up) * _one_plus_tanh(half_gate)
            act_scr[:, f0:f0 + LANES] = act.astype(_BF16)
    acc = xv + jnp.dot(act_scr[...], wd_ref[...], preferred_element_type=_F32)
    if has_final:
        acc = _rmsnorm(acc, fg_ref[...])
    o_ref[...] = acc


def _ffn(x2d, wgu, wd, mixer=None, pool=None, final_g=None, prep=None, seq=None,
         tile_rows=TOKEN_TILE):
    tokens, d = x2d.shape
    n_chunks, _, _ = wgu.shape
    ff = wd.shape[0]
    tm = tile_rows
    tiles_per_seq = seq // tm
    n_tiles = tokens // tm
    has_mixer = mixer is not None
    has_pool = pool is not None
    has_final = final_g is not None
    has_prep = prep is not None
    assert n_tiles >= n_chunks
    args = [x2d]
    specs = [pl.BlockSpec((tm, d), lambda i: (i, 0))]
    out_specs = [pl.BlockSpec((tm, d), lambda i: (i, 0))]
    out_shape = [jax.ShapeDtypeStruct((tokens, d), _F32)]
    scratch = [pltpu.VMEM((tm, d), _BF16), pltpu.VMEM((tm, ff), _BF16)]
    if has_mixer:
        ya2d, yb0, u2d, conv_w, conv_b, ln_g, ln_b, w_out = mixer
        hb = tm // CONV_HALO
        n_halo = tokens // CONV_HALO
        nxt = lambda i: jnp.minimum(i + 1, n_tiles - 1)
        args += [ya2d, yb0, u2d, u2d, u2d, conv_w, conv_b, ln_g, ln_b, w_out]
        specs += [
            pl.BlockSpec((tm, A_WIDTH), lambda i: (i, 0)),
            _resident(yb0.shape),
            pl.BlockSpec((tm, B_WIDTH), lambda i: (nxt(i), 0)),
            pl.BlockSpec((CONV_HALO, B_WIDTH), lambda i: (jnp.maximum(nxt(i) * hb - 1, 0), 0)),
            pl.BlockSpec((CONV_HALO, B_WIDTH),
                         lambda i: (jnp.minimum((nxt(i) + 1) * hb, n_halo - 1), 0)),
            _resident(conv_w.shape), _resident(conv_b.shape), _resident(ln_g.shape),
            _resident(ln_b.shape), _resident(w_out.shape),
        ]
        scratch += [pltpu.VMEM((2, tm, B_WIDTH), _BF16)] + _conv_scratch(tm)
    if has_pool:
        g_mix, pool_map, pool_scale = pool
        hb = tm // POOL_HALO
        n_halo = tokens // POOL_HALO
        args += [x2d, x2d, g_mix, pool_map, pool_scale]
        specs += [
            pl.BlockSpec((POOL_HALO, d), lambda i: (jnp.maximum(i * hb - 1, 0), 0)),
            pl.BlockSpec((POOL_HALO, d), lambda i: (jnp.minimum((i + 1) * hb, n_halo - 1), 0)),
            _resident(g_mix.shape), _resident(pool_map.shape), _resident(pool_scale.shape),
        ]
        scratch += [pltpu.VMEM((tm + 2 * POOL_HALO, d), _F32), pltpu.VMEM(pool_map.shape, _BF16)]
    args += [wgu, wd]
    specs += [_resident(wgu.shape), _resident(wd.shape)]
    if has_prep:
        prep_in, prep_out, prep_shape = _prep_specs(
            *prep, lambda i: jnp.minimum(i, n_chunks - 1))
        args += list(prep[:4])
        specs += prep_in
        out_specs += prep_out
        out_shape += prep_shape
    if has_final:
        args.append(final_g)
        specs.append(_resident(final_g.shape))
    outs = pl.pallas_call(
        functools.partial(_ffn_kernel, has_mixer=has_mixer, has_pool=has_pool, has_final=has_final,
                          has_prep=has_prep, tiles_per_seq=tiles_per_seq, n_chunks=n_chunks),
        grid=(n_tiles,),
        in_specs=specs,
        out_specs=out_specs,
        out_shape=out_shape,
        scratch_shapes=scratch,
        compiler_params=pltpu.CompilerParams(
            dimension_semantics=("arbitrary",),
            vmem_limit_bytes=V7X_VMEM_LIMIT_BYTES),
        name="ffn_mixer" if has_mixer else "ffn_final",
    )(*args)
    return outs if has_prep else outs[0]


def kernel(x, norm_mix_g, norm_ffn_g, w_in_ab, fnet_map, conv_w, conv_b, conv_ln_g, conv_ln_b,
           w_out_ab, pool_map, pool_scale, ffn_w_gate, ffn_w_up, ffn_w_down, final_g):
    bsz, seq, d = x.shape
    tokens = bsz * seq
    row = lambda v: v.reshape(1, -1)
    col = lambda v: v.reshape(-1, 1)

    ffn_weights = (ffn_w_gate, ffn_w_up, ffn_w_down)
    a_perm, u, w_out, wgu0, wd0 = _in_proj(x, col(norm_mix_g[0]), w_in_ab[0], w_out_ab[0],
                                           col(norm_ffn_g[0]), ffn_weights)
    ya = _fnet(a_perm, _dft_constants(seq), fnet_map[0])
    u2d = u.reshape(tokens, B_WIDTH)
    conv_p = (conv_w[0], row(conv_b[0]), row(conv_ln_g[0]), row(conv_ln_b[0]))
    mixer = (ya.reshape(tokens, A_WIDTH), _conv_first(u2d, *conv_p), u2d, *conv_p, w_out)
    x2, wgu1, wd1 = _ffn(x.reshape(tokens, d), wgu0, wd0, mixer=mixer,
                         prep=(col(norm_ffn_g[1]), *ffn_weights, 1), seq=seq)

    pool = (col(norm_mix_g[1]), pool_map[0], row(pool_scale[0]))
    out = _ffn(x2, wgu1, wd1, pool=pool, final_g=row(final_g), seq=seq, tile_rows=FFN_FINAL_TILE)
    return out.reshape(bsz, seq, d)
```

```python
import functools
import math

import jax
import jax.numpy as jnp
import numpy as np
from jax.experimental import pallas as pl
from jax.experimental.pallas import tpu as pltpu

RMS_EPS = 1e-6
LN_EPS = 1e-5

A_HEADS = 4
HEAD_DIM = 128
A_WIDTH = A_HEADS * HEAD_DIM
B_WIDTH = 512
CONV_WIDTH = 31
CONV_PAD = CONV_WIDTH // 2
POOL_WINDOWS = (2, 4, 8, 16)
POOL_HALO = 8

LANES = 128
SUBLANES = 8

DFT_RADIX = 8

V7X_VMEM_LIMIT_BYTES = 56 * 1024 * 1024

TOKEN_TILE = 512
IN_PROJ_TILE = 1024
FFN_FINAL_TILE = 1024
FF_PREP_CHUNK = 256
BFLY_ROW_TILE = 16
CONV_HALO = 16
CONV_ROW_TILE = 16
CONV_UNITS_PER_DOT = 8

_F32 = jnp.float32
_BF16 = jnp.bfloat16


def _resident(shape):
    nd = len(shape)
    return pl.BlockSpec(shape, lambda *_: (0,) * nd, pipeline_mode=pl.Buffered(1))


def _rms_scale(xv):
    return xv * jax.lax.rsqrt(jnp.mean(xv * xv, axis=-1, keepdims=True) + RMS_EPS)


def _rmsnorm(xv, g):
    return _rms_scale(xv) * g


def _one_plus_tanh(half_v):
    return 1.0 + jnp.tanh(half_v)


def _prep_ffn_weights(gcol_ref, wg_ref, wu_ref, wd_ref, wgu_dst, wd_dst):
    cw = wg_ref.shape[1]
    gain = gcol_ref[...]
    half_gain = gain * 0.5
    for b in range(cw // LANES):
        src = slice(b * LANES, (b + 1) * LANES)
        wgu_dst[:, 2 * b * LANES:(2 * b + 1) * LANES] = (wg_ref[:, src] * half_gain).astype(_BF16)
        wgu_dst[:, (2 * b + 1) * LANES:(2 * b + 2) * LANES] = (wu_ref[:, src] * gain).astype(_BF16)
    wd_dst[...] = wd_ref[...].astype(_BF16)


def _prep_specs(gcol, wg_all, wu_all, wd_all, layer, chunk_of):
    _, d, ff = wg_all.shape
    cw = FF_PREP_CHUNK
    in_specs = [
        _resident(gcol.shape),
        pl.BlockSpec((None, d, cw), lambda *idx: (layer, 0, chunk_of(*idx))),
        pl.BlockSpec((None, d, cw), lambda *idx: (layer, 0, chunk_of(*idx))),
        pl.BlockSpec((None, cw, d), lambda *idx: (layer, chunk_of(*idx), 0)),
    ]
    out_specs = [
        pl.BlockSpec((None, d, 2 * cw), lambda *idx: (chunk_of(*idx), 0, 0)),
        pl.BlockSpec((cw, d), lambda *idx: (chunk_of(*idx), 0)),
    ]
    out_shape = [jax.ShapeDtypeStruct((ff // cw, d, 2 * cw), _BF16),
                 jax.ShapeDtypeStruct((ff, d), _BF16)]
    return in_specs, out_specs, out_shape


def _in_proj_kernel(x_ref, gcol_ref, w_ref, wo_ref, fgcol_ref, wg_ref, wu_ref, wd_ref,
                    a_ref, u_ref, wo_out, wgu_out, wd_out, a_scr, w_scr, *, n_chunks):
    step = pl.program_id(0) * pl.num_programs(1) + pl.program_id(1)

    @pl.when(step < n_chunks)
    def _():
        _prep_ffn_weights(fgcol_ref, wg_ref, wu_ref, wd_ref, wgu_out, wd_out)

    @pl.when(step == 0)
    def _():
        gain = gcol_ref[...]
        w_scr[:, :A_WIDTH] = (w_ref[:, :A_WIDTH] * gain).astype(_BF16)
        w_scr[:, A_WIDTH:] = (w_ref[:, A_WIDTH:] * (gain * 0.5)).astype(_BF16)
        wo_out[...] = wo_ref[...].astype(_BF16)

    h = _rms_scale(x_ref[...]).astype(_BF16)
    p = jnp.dot(h, w_scr[...], preferred_element_type=_F32)
    rows = a_scr.shape[1] // DFT_RADIX
    for lt in range(A_WIDTH // LANES):
        lanes = slice(lt * LANES, (lt + 1) * LANES)
        a_scr[lt] = p[:, lanes]
        for jr in range(DFT_RADIX):
            a_ref[jr, :, lanes] = a_scr[lt, pl.ds(jr, rows, stride=DFT_RADIX), :].astype(_BF16)
    half_v = p[:, A_WIDTH:A_WIDTH + B_WIDTH]
    half_gate = p[:, A_WIDTH + B_WIDTH:]
    u_ref[...] = half_v * _one_plus_tanh(half_gate)


def _in_proj(x, gcol, w_in, w_out, ffn_gcol, ffn_weights):
    bsz, seq, d = x.shape
    tm = IN_PROJ_TILE
    inner = seq // DFT_RADIX
    steps_per_batch = seq // tm
    n_chunks = ffn_weights[0].shape[2] // FF_PREP_CHUNK
    assert bsz * steps_per_batch >= n_chunks
    chunk_of = lambda b, i: jnp.minimum(b * steps_per_batch + i, n_chunks - 1)
    prep_in, prep_out, prep_shape = _prep_specs(ffn_gcol, *ffn_weights, 0, chunk_of)
    return pl.pallas_call(
        functools.partial(_in_proj_kernel, n_chunks=n_chunks),
        grid=(bsz, steps_per_batch),
        in_specs=[
            pl.BlockSpec((None, tm, d), lambda b, i: (b, i, 0)),
            _resident(gcol.shape),
            _resident(w_in.shape),
            _resident(w_out.shape),
        ] + prep_in,
        out_specs=[
            pl.BlockSpec((None, DFT_RADIX, tm // DFT_RADIX, A_WIDTH), lambda b, i: (b, 0, i, 0)),
            pl.BlockSpec((None, tm, B_WIDTH), lambda b, i: (b, i, 0)),
            _resident(w_out.shape),
        ] + prep_out,
        out_shape=[
            jax.ShapeDtypeStruct((bsz, DFT_RADIX, inner, A_WIDTH), _BF16),
            jax.ShapeDtypeStruct((bsz, seq, B_WIDTH), _F32),
            jax.ShapeDtypeStruct(w_out.shape, _BF16),
        ] + prep_shape,
        scratch_shapes=[pltpu.VMEM((A_WIDTH // LANES, tm, LANES), _F32),
                        pltpu.VMEM(w_in.shape, _BF16)],
        compiler_params=pltpu.CompilerParams(
            dimension_semantics=("arbitrary", "arbitrary"),
            vmem_limit_bytes=V7X_VMEM_LIMIT_BYTES),
        name="in_proj",
    )(x, gcol, w_in, w_out, ffn_gcol, *ffn_weights)


def _cadd(a, b):
    return (a[0] + b[0], a[1] + b[1])


def _csub(a, b):
    return (a[0] - b[0], a[1] - b[1])


def _dft4(a0, a1, a2, a3):
    s0, s1 = _cadd(a0, a2), _csub(a0, a2)
    s2, s3 = _cadd(a1, a3), _csub(a1, a3)
    return (_cadd(s0, s2), (s1[0] + s3[1], s1[1] - s3[0]),
            _csub(s0, s2), (s1[0] - s3[1], s1[1] + s3[0]))


def _mul_w8(k, z):
    r, i = z
    h = math.sqrt(0.5)
    if k == 0:
        return z
    if k == 1:
        return (h * (r + i), h * (i - r))
    if k == 2:
        return (i, -r)
    return (h * (i - r), -h * (r + i))


def _fnet_kernel(a_ref, cs_ref, twc_ref, tws_ref, cdsd_ref, map_ref, y_ref, yr_scr, yi_scr):
    seq = y_ref.shape[0]
    inner = seq // DFT_RADIX

    for jr in range(DFT_RADIX):
        yy = jnp.dot(cs_ref[...], a_ref[jr], preferred_element_type=_F32)
        yr_scr[jr * inner:(jr + 1) * inner, :] = yy[:inner]
        yi_scr[jr * inner:(jr + 1) * inner, :] = yy[inner:]

    rt = BFLY_ROW_TILE

    def bfly(c, carry):
        r0 = pl.multiple_of(c * rt, rt)
        for lt in range(A_WIDTH // LANES):
            lanes = slice(lt * LANES, (lt + 1) * LANES)
            z = []
            for jr in range(DFT_RADIX):
                rows = pl.ds(jr * inner + r0, rt)
                yr = yr_scr[rows, lanes]
                yi = yi_scr[rows, lanes]
                if jr == 0:
                    z.append((yr, yi))
                else:
                    tc = twc_ref[rows, :]
                    ts = tws_ref[rows, :]
                    z.append((yr * tc + yi * ts, yi * tc - yr * ts))
            ev = _dft4(z[0], z[2], z[4], z[6])
            od = _dft4(z[1], z[3], z[5], z[7])
            for k in range(4):
                w = _mul_w8(k, od[k])
                lo = _cadd(ev[k], w)
                hi = _csub(ev[k], w)
                rows_lo = pl.ds(k * inner + r0, rt)
                rows_hi = pl.ds((k + 4) * inner + r0, rt)
                yr_scr[rows_lo, lanes] = lo[0]
                yi_scr[rows_lo, lanes] = lo[1]
                yr_scr[rows_hi, lanes] = hi[0]
                yi_scr[rows_hi, lanes] = hi[1]
        return carry

    jax.lax.fori_loop(0, inner // rt, bfly, 0)

    for hd in range(A_HEADS):
        lanes = slice(hd * HEAD_DIM, (hd + 1) * HEAD_DIM)
        lhs = jnp.concatenate([yr_scr[:, lanes].astype(_BF16),
                               yi_scr[:, lanes].astype(_BF16)], axis=1)
        f = jnp.dot(lhs, cdsd_ref[...], preferred_element_type=_F32)
        ya = jnp.dot(f.astype(_BF16), map_ref[hd].astype(_BF16), preferred_element_type=_F32)
        y_ref[:, lanes] = ya.astype(_BF16)


def _fnet(a_perm, consts, fmap):
    bsz, _, inner, _ = a_perm.shape
    seq = inner * DFT_RADIX
    cs, twc, tws, cdsd = consts
    return pl.pallas_call(
        _fnet_kernel,
        grid=(bsz,),
        in_specs=[
            pl.BlockSpec((None, DFT_RADIX, inner, A_WIDTH), lambda b: (b, 0, 0, 0)),
            _resident(cs.shape), _resident(twc.shape), _resident(tws.shape),
            _resident(cdsd.shape), _resident(fmap.shape),
        ],
        out_specs=pl.BlockSpec((None, seq, A_WIDTH), lambda b: (b, 0, 0)),
        out_shape=jax.ShapeDtypeStruct((bsz, seq, A_WIDTH), _BF16),
        scratch_shapes=[
            pltpu.VMEM((seq, A_WIDTH), _F32),
            pltpu.VMEM((seq, A_WIDTH), _F32),
        ],
        compiler_params=pltpu.CompilerParams(
            dimension_semantics=("arbitrary",),
            vmem_limit_bytes=V7X_VMEM_LIMIT_BYTES),
        name="fnet",
    )(a_perm, cs, twc, tws, cdsd, fmap)


def _dft_constants(seq):
    inner = seq // DFT_RADIX
    k = np.arange(inner, dtype=np.float64)
    ang = 2.0 * np.pi * np.outer(k, k) / inner
    cs = np.concatenate([np.cos(ang), -np.sin(ang)], axis=0)
    jr = np.arange(DFT_RADIX, dtype=np.float64)[:, None]
    tw = 2.0 * np.pi * (jr * k[None, :]) / seq
    twc = np.repeat(np.cos(tw).reshape(seq, 1), LANES, axis=1)
    tws = np.repeat(np.sin(tw).reshape(seq, 1), LANES, axis=1)
    d = np.arange(HEAD_DIM, dtype=np.float64)
    angd = 2.0 * np.pi * np.outer(d, d) / HEAD_DIM
    scale = 1.0 / math.sqrt(seq * HEAD_DIM)
    cdsd = np.concatenate([np.cos(angd), np.sin(angd)], axis=0) * scale
    return (jnp.asarray(cs, _F32).astype(_BF16), jnp.asarray(twc, _F32), jnp.asarray(tws, _F32),
            jnp.asarray(cdsd, _F32).astype(_BF16))


def _conv_fill_window(tile, tiles_per_seq, main_ref, prev_ref, next_ref, win_scr):
    tm = main_ref.shape[0]
    halo = CONV_HALO
    pos = jnp.zeros((halo, 1), jnp.int32) + tile % tiles_per_seq
    win_scr[0:halo, :] = jnp.where(pos == 0, 0.0, prev_ref[...])
    win_scr[halo:halo + tm, :] = main_ref[...]
    win_scr[halo + tm:, :] = jnp.where(pos == tiles_per_seq - 1, 0.0, next_ref[...])


def _conv_unit(rc, lt, win_scr, cw_ref, cb_ref, lg_ref, lb_ref, out_ref):
    ct = CONV_ROW_TILE
    halo = CONV_HALO
    r0 = rc * ct
    lanes = slice(lt * LANES, (lt + 1) * LANES)
    first = halo - CONV_PAD
    span = ct + 2 * halo
    win = win_scr[r0:r0 + span, lanes]
    acc = None
    for s in range(SUBLANES):
        rot = win if s == 0 else pltpu.roll(win, span - s, axis=0)
        for q in range((first + CONV_WIDTH - 1) // SUBLANES + 1):
            k = SUBLANES * q + s - first
            if 0 <= k < CONV_WIDTH:
                term = rot[SUBLANES * q:SUBLANES * q + ct] * cw_ref[k:k + 1, lanes]
                acc = term if acc is None else acc + term
    cv = acc + cb_ref[:, lanes]
    mu = jnp.mean(cv, axis=-1, keepdims=True)
    dv = cv - mu
    var = jnp.mean(dv * dv, axis=-1, keepdims=True)
    half_yn = (dv * jax.lax.rsqrt(var + LN_EPS)) * (lg_ref[:, lanes] * 0.5) + lb_ref[:, lanes] * 0.5
    y = half_yn * _one_plus_tanh(half_yn)
    out_ref[r0:r0 + ct, lanes] = y.astype(_BF16)


def _conv_units(tm):
    return [(rc, lt) for rc in range(tm // CONV_ROW_TILE) for lt in range(B_WIDTH // LANES)]


def _conv_scratch(tm):
    return [pltpu.VMEM((tm + 2 * CONV_HALO, B_WIDTH), _F32)]


def _conv_first_kernel(main_ref, next_ref, cw_ref, cb_ref, lg_ref, lb_ref, out_ref, win_scr):
    _conv_fill_window(0, 2, main_ref, next_ref, next_ref, win_scr)
    for rc, lt in _conv_units(main_ref.shape[0]):
        _conv_unit(rc, lt, win_scr, cw_ref, cb_ref, lg_ref, lb_ref, out_ref)


def _conv_first(u2d, conv_w, conv_b, ln_g, ln_b):
    tm = TOKEN_TILE
    return pl.pallas_call(
        _conv_first_kernel,
        grid=(1,),
        in_specs=[
            pl.BlockSpec((tm, B_WIDTH), lambda i: (0, 0)),
            pl.BlockSpec((CONV_HALO, B_WIDTH), lambda i: (tm // CONV_HALO, 0)),
            _resident(conv_w.shape), _resident(conv_b.shape), _resident(ln_g.shape),
            _resident(ln_b.shape),
        ],
        out_specs=pl.BlockSpec((tm, B_WIDTH), lambda i: (0, 0)),
        out_shape=jax.ShapeDtypeStruct((tm, B_WIDTH), _BF16),
        scratch_shapes=_conv_scratch(tm),
        compiler_params=pltpu.CompilerParams(
            dimension_semantics=("arbitrary",),
            vmem_limit_bytes=V7X_VMEM_LIMIT_BYTES),
        name="conv_first",
    )(u2d, u2d, conv_w, conv_b, ln_g, ln_b)


def _pool_tile(tile, tiles_per_seq, x_ref, prev_ref, next_ref, gcol_ref, pm_ref, ps_ref, hp_scr,
               pm_scr):
    tm, d = x_ref.shape
    seq = tm * tiles_per_seq
    n = tm + 2 * POOL_HALO
    gd = d // len(POOL_WINDOWS)

    @pl.when(tile == 0)
    def _():
        for gi in range(len(POOL_WINDOWS)):
            pm_scr[gi] = (pm_ref[gi] * gcol_ref[gi * gd:(gi + 1) * gd, :]).astype(_BF16)

    xv = x_ref[...]
    hm = _rms_scale(xv)
    start = (tile % tiles_per_seq) * tm
    halo_iota = jax.lax.broadcasted_iota(jnp.int32, (POOL_HALO, 1), 0)
    prev_ok = (start - POOL_HALO + halo_iota) >= 0
    next_ok = (start + tm + halo_iota) < seq
    hp_scr[0:POOL_HALO, :] = jnp.where(prev_ok, _rms_scale(prev_ref[...]), 0.0)
    hp_scr[POOL_HALO:POOL_HALO + tm, :] = hm
    hp_scr[POOL_HALO + tm:, :] = jnp.where(next_ok, _rms_scale(next_ref[...]), 0.0)

    edge_iota = jax.lax.broadcasted_iota(jnp.int32, (POOL_HALO, LANES), 0)
    ys = []
    for gi, w in enumerate(POOL_WINDOWS):
        lanes = slice(gi * gd, (gi + 1) * gd)
        half = w // 2
        fwd = hp_scr[:, lanes]
        span = 1
        while span < half:
            fwd = fwd + pltpu.roll(fwd, n - span, axis=0)
            span *= 2
        centred = fwd + pltpu.roll(fwd, half, axis=0)
        win = centred[POOL_HALO:POOL_HALO + tm]

        def mean_rows(r0):
            pos = start + r0 + edge_iota
            cnt = jnp.minimum(pos + (w - half), seq) - jnp.maximum(pos - half, 0)
            inv = 1.0 / cnt.astype(_F32)
            return win[r0:r0 + POOL_HALO] * jnp.concatenate([inv] * (gd // LANES), axis=1)

        mean = jnp.concatenate([mean_rows(0), win[POOL_HALO:tm - POOL_HALO] * (1.0 / w),
                                mean_rows(tm - POOL_HALO)], axis=0)
        pg = mean - hm[:, lanes]
        ys.append(jnp.dot(pg.astype(_BF16), pm_scr[gi], preferred_element_type=_F32))
    return xv + jnp.concatenate(ys, axis=1) * ps_ref[...]


def _ffn_kernel(*refs, has_mixer, has_pool, has_final, has_prep, tiles_per_seq, n_chunks):
    it = iter(refs)
    x_ref = next(it)
    if has_mixer:
        ya_ref = next(it)
        yb0_ref = next(it)
        u_refs = (next(it), next(it), next(it))
        conv_refs = (next(it), next(it), next(it), next(it))
        wo_ref = next(it)
    if has_pool:
        pool_refs = (next(it), next(it), next(it), next(it), next(it))
    wgu_ref = next(it)
    wd_ref = next(it)
    if has_prep:
        prep_in = (next(it), next(it), next(it), next(it))
    if has_final:
        fg_ref = next(it)
    o_ref = next(it)
    if has_prep:
        prep_out = (next(it), next(it))
    h_scr = next(it)
    act_scr = next(it)
    if has_mixer:
        yb_scr = next(it)
        win_scr = next(it)
    if has_pool:
        hp_scr = next(it)
        pm_scr = next(it)

    i = pl.program_id(0)
    if has_prep:
        @pl.when(i < n_chunks)
        def _():
            _prep_ffn_weights(*prep_in, *prep_out)

    tm = x_ref.shape[0]
    if has_pool:
        xv = _pool_tile(i, tiles_per_seq, x_ref, *pool_refs, hp_scr, pm_scr)
    else:
        xv = x_ref[...]
    conv_some = lambda n_units: None
    if has_mixer:
        n = pl.num_programs(0)
        slot = i % 2

        @pl.when(i == 0)
        def _():
            yb_scr[0] = yb0_ref[...]

        yb = yb_scr[slot]
        _conv_fill_window(jnp.minimum(i + 1, n - 1), tiles_per_seq, *u_refs, win_scr)
        pending = iter(_conv_units(tm))

        def conv_some(n_units):
            for _ in range(n_units):
                _conv_unit(*next(pending), win_scr, *conv_refs, yb_scr.at[1 - slot])

        conv_some(len(_conv_units(tm)) - n_chunks * CONV_UNITS_PER_DOT)
        yab = jnp.concatenate([ya_ref[...], yb], axis=1)
        xv = xv + jnp.dot(yab, wo_ref[...], preferred_element_type=_F32)
    h_scr[...] = _rms_scale(xv).astype(_BF16)

    gu_width = wgu_ref.shape[2]
    for c in range(n_chunks):
        conv_some(CONV_UNITS_PER_DOT)
        gu = jnp.dot(h_scr[...], wgu_ref[c], preferred_element_type=_F32)
        for b0 in range(0, gu_width, 2 * LANES):
            half_gate = gu[:, b0:b0 + LANES]
            up = gu[:, b0 + LANES:b0 + 2 * LANES]
            f0 = (c * gu_width + b0) // 2
            act = (half_gate * up) * _one_plus_tanh(half_gate)
            act_scr[:, f0:f0 + LANES] = act.astype(_BF16)
    acc = xv + jnp.dot(act_scr[...], wd_ref[...], preferred_element_type=_F32)
    if has_final:
        acc = _rmsnorm(acc, fg_ref[...])
    o_ref[...] = acc


def _ffn(x2d, wgu, wd, mixer=None, pool=None, final_g=None, prep=None, seq=None,
         tile_rows=TOKEN_TILE):
    tokens, d = x2d.shape
    n_chunks, _, _ = wgu.shape
    ff = wd.shape[0]
    tm = tile_rows
    tiles_per_seq = seq // tm
    n_tiles = tokens // tm
    has_mixer = mixer is not None
    has_pool = pool is not None
    has_final = final_g is not None
    has_prep = prep is not None
    assert n_tiles >= n_chunks
    args = [x2d]
    specs = [pl.BlockSpec((tm, d), lambda i: (i, 0))]
    out_specs = [pl.BlockSpec((tm, d), lambda i: (i, 0))]
    out_shape = [jax.ShapeDtypeStruct((tokens, d), _F32)]
    scratch = [pltpu.VMEM((tm, d), _BF16), pltpu.VMEM((tm, ff), _BF16)]
    if has_mixer:
        ya2d, yb0, u2d, conv_w, conv_b, ln_g, ln_b, w_out = mixer
        hb = tm // CONV_HALO
        n_halo = tokens // CONV_HALO
        nxt = lambda i: jnp.minimum(i + 1, n_tiles - 1)
        args += [ya2d, yb0, u2d, u2d, u2d, conv_w, conv_b, ln_g, ln_b, w_out]
        specs += [
            pl.BlockSpec((tm, A_WIDTH), lambda i: (i, 0)),
            _resident(yb0.shape),
            pl.BlockSpec((tm, B_WIDTH), lambda i: (nxt(i), 0)),
            pl.BlockSpec((CONV_HALO, B_WIDTH), lambda i: (jnp.maximum(nxt(i) * hb - 1, 0), 0)),
            pl.BlockSpec((CONV_HALO, B_WIDTH),
                         lambda i: (jnp.minimum((nxt(i) + 1) * hb, n_halo - 1), 0)),
            _resident(conv_w.shape), _resident(conv_b.shape), _resident(ln_g.shape),
            _resident(ln_b.shape), _resident(w_out.shape),
        ]
        scratch += [pltpu.VMEM((2, tm, B_WIDTH), _BF16)] + _conv_scratch(tm)
    if has_pool:
        g_mix, pool_map, pool_scale = pool
        hb = tm // POOL_HALO
        n_halo = tokens // POOL_HALO
        args += [x2d, x2d, g_mix, pool_map, pool_scale]
        specs += [
            pl.BlockSpec((POOL_HALO, d), lambda i: (jnp.maximum(i * hb - 1, 0), 0)),
            pl.BlockSpec((POOL_HALO, d), lambda i: (jnp.minimum((i + 1) * hb, n_halo - 1), 0)),
            _resident(g_mix.shape), _resident(pool_map.shape), _resident(pool_scale.shape),
        ]
        scratch += [pltpu.VMEM((tm + 2 * POOL_HALO, d), _F32), pltpu.VMEM(pool_map.shape, _BF16)]
    args += [wgu, wd]
    specs += [_resident(wgu.shape), _resident(wd.shape)]
    if has_prep:
        prep_in, prep_out, prep_shape = _prep_specs(
            *prep, lambda i: jnp.minimum(i, n_chunks - 1))
        args += list(prep[:4])
        specs += prep_in
        out_specs += prep_out
        out_shape += prep_shape
    if has_final:
        args.append(final_g)
        specs.append(_resident(final_g.shape))
    outs = pl.pallas_call(
        functools.partial(_ffn_kernel, has_mixer=has_mixer, has_pool=has_pool, has_final=has_final,
                          has_prep=has_prep, tiles_per_seq=tiles_per_seq, n_chunks=n_chunks),
        grid=(n_tiles,),
        in_specs=specs,
        out_specs=out_specs,
        out_shape=out_shape,
        scratch_shapes=scratch,
        compiler_params=pltpu.CompilerParams(
            dimension_semantics=("arbitrary",),
            vmem_limit_bytes=V7X_VMEM_LIMIT_BYTES),
        name="ffn_mixer" if has_mixer else "ffn_final",
    )(*args)
    return outs if has_prep else outs[0]


def kernel(x, norm_mix_g, norm_ffn_g, w_in_ab, fnet_map, conv_w, conv_b, conv_ln_g, conv_ln_b,
           w_out_ab, pool_map, pool_scale, ffn_w_gate, ffn_w_up, ffn_w_down, final_g):
    bsz, seq, d = x.shape
    tokens = bsz * seq
    row = lambda v: v.reshape(1, -1)
    col = lambda v: v.reshape(-1, 1)

    ffn_weights = (ffn_w_gate, ffn_w_up, ffn_w_down)
    a_perm, u, w_out, wgu0, wd0 = _in_proj(x, col(norm_mix_g[0]), w_in_ab[0], w_out_ab[0],
                                           col(norm_ffn_g[0]), ffn_weights)
    ya = _fnet(a_perm, _dft_constants(seq), fnet_map[0])
    u2d = u.reshape(tokens, B_WIDTH)
    conv_p = (conv_w[0], row(conv_b[0]), row(conv_ln_g[0]), row(conv_ln_b[0]))
    mixer = (ya.reshape(tokens, A_WIDTH), _conv_first(u2d, *conv_p), u2d, *conv_p, w_out)
    x2, wgu1, wd1 = _ffn(x.reshape(tokens, d), wgu0, wd0, mixer=mixer,
                         prep=(col(norm_ffn_g[1]), *ffn_weights, 1), seq=seq)

    pool = (col(norm_mix_g[1]), pool_map[0], row(pool_scale[0]))
    out = _ffn(x2, wgu1, wd1, pool=pool, final_g=row(final_g), seq=seq, tile_rows=FFN_FINAL_TILE)
    return out.reshape(bsz, seq, d)
```

```python
import functools
import math

import jax
import jax.numpy as jnp
import numpy as np
from jax.experimental import pallas as pl
from jax.experimental.pallas import tpu as pltpu

RMS_EPS = 1e-6
LN_EPS = 1e-5

A_HEADS = 4
HEAD_DIM = 128
A_WIDTH = A_HEADS * HEAD_DIM
B_WIDTH = 512
CONV_WIDTH = 31
CONV_PAD = CONV_WIDTH // 2
POOL_WINDOWS = (2, 4, 8, 16)
POOL_HALO = 8

LANES = 128
SUBLANES = 8

DFT_RADIX = 8

V7X_VMEM_LIMIT_BYTES = 56 * 1024 * 1024

TOKEN_TILE = 512
IN_PROJ_TILE = 1024
FFN_FINAL_TILE = 1024
FF_PREP_CHUNK = 256
BFLY_ROW_TILE = 8
FNET_COLS = 256
FNET_ROW_BLOCK = 512
CONV_HALO = 16
CONV_ROW_TILE = 16
CONV_FIRST_ROW_TILE = 64
CONV_UNITS_PER_DOT = 8

_F32 = jnp.float32
_BF16 = jnp.bfloat16


def _resident(shape):
    nd = len(shape)
    return pl.BlockSpec(shape, lambda *_: (0,) * nd, pipeline_mode=pl.Buffered(1))


def _rms_scale(xv):
    return xv * jax.lax.rsqrt(jnp.mean(xv * xv, axis=-1, keepdims=True) + RMS_EPS)


def _rmsnorm(xv, g):
    return _rms_scale(xv) * g


def _one_plus_tanh(half_v):
    return 1.0 + jnp.tanh(half_v)


def _prep_ffn_weights(gcol_ref, wg_ref, wu_ref, wd_ref, wgu_dst, wd_dst):
    cw = wg_ref.shape[1]
    gain = gcol_ref[...]
    half_gain = gain * 0.5
    for b in range(cw // LANES):
        src = slice(b * LANES, (b + 1) * LANES)
        wgu_dst[:, 2 * b * LANES:(2 * b + 1) * LANES] = (wg_ref[:, src] * half_gain).astype(_BF16)
        wgu_dst[:, (2 * b + 1) * LANES:(2 * b + 2) * LANES] = (wu_ref[:, src] * gain).astype(_BF16)
    wd_dst[...] = wd_ref[...].astype(_BF16)


def _prep_specs(gcol, wg_all, wu_all, wd_all, layer, chunk_of):
    _, d, ff = wg_all.shape
    cw = FF_PREP_CHUNK
    in_specs = [
        _resident(gcol.shape),
        pl.BlockSpec((None, d, cw), lambda *idx: (layer, 0, chunk_of(*idx))),
        pl.BlockSpec((None, d, cw), lambda *idx: (layer, 0, chunk_of(*idx))),
        pl.BlockSpec((None, cw, d), lambda *idx: (layer, chunk_of(*idx), 0)),
    ]
    out_specs = [
        pl.BlockSpec((None, d, 2 * cw), lambda *idx: (chunk_of(*idx), 0, 0)),
        pl.BlockSpec((cw, d), lambda *idx: (chunk_of(*idx), 0)),
    ]
    out_shape = [jax.ShapeDtypeStruct((ff // cw, d, 2 * cw), _BF16),
                 jax.ShapeDtypeStruct((ff, d), _BF16)]
    return in_specs, out_specs, out_shape


def _in_proj_kernel(x_ref, gcol_ref, w_ref, wo_ref, fgcol_ref, wg_ref, wu_ref, wd_ref,
                    a_ref, u_ref, wo_out, wgu_out, wd_out, a_scr, w_scr, *, n_chunks):
    step = pl.program_id(0) * pl.num_programs(1) + pl.program_id(1)

    @pl.when(step < n_chunks)
    def _():
        _prep_ffn_weights(fgcol_ref, wg_ref, wu_ref, wd_ref, wgu_out, wd_out)

    @pl.when(step == 0)
    def _():
        gain = gcol_ref[...]
        w_scr[:, :A_WIDTH] = (w_ref[:, :A_WIDTH] * gain).astype(_BF16)
        w_scr[:, A_WIDTH:] = (w_ref[:, A_WIDTH:] * (gain * 0.5)).astype(_BF16)
        wo_out[...] = wo_ref[...].astype(_BF16)

    h = _rms_scale(x_ref[...]).astype(_BF16)
    p = jnp.dot(h, w_scr[...], preferred_element_type=_F32)
    rows = a_scr.shape[1] // DFT_RADIX
    for lt in range(A_WIDTH // LANES):
        lanes = slice(lt * LANES, (lt + 1) * LANES)
        a_scr[lt] = p[:, lanes]
        for jr in range(DFT_RADIX):
            a_ref[jr, :, lanes] = a_scr[lt, pl.ds(jr, rows, stride=DFT_RADIX), :].astype(_BF16)
    half_v = p[:, A_WIDTH:A_WIDTH + B_WIDTH]
    half_gate = p[:, A_WIDTH + B_WIDTH:]
    u_ref[...] = half_v * _one_plus_tanh(half_gate)


def _in_proj(x, gcol, w_in, w_out, ffn_gcol, ffn_weights):
    bsz, seq, d = x.shape
    tm = IN_PROJ_TILE
    inner = seq // DFT_RADIX
    steps_per_batch = seq // tm
    n_chunks = ffn_weights[0].shape[2] // FF_PREP_CHUNK
    assert bsz * steps_per_batch >= n_chunks
    chunk_of = lambda b, i: jnp.minimum(b * steps_per_batch + i, n_chunks - 1)
    prep_in, prep_out, prep_shape = _prep_specs(ffn_gcol, *ffn_weights, 0, chunk_of)
    return pl.pallas_call(
        functools.partial(_in_proj_kernel, n_chunks=n_chunks),
        grid=(bsz, steps_per_batch),
        in_specs=[
            pl.BlockSpec((None, tm, d), lambda b, i: (b, i, 0)),
            _resident(gcol.shape),
            _resident(w_in.shape),
            _resident(w_out.shape),
        ] + prep_in,
        out_specs=[
            pl.BlockSpec((None, DFT_RADIX, tm // DFT_RADIX, A_WIDTH), lambda b, i: (b, 0, i, 0)),
            pl.BlockSpec((None, tm, B_WIDTH), lambda b, i: (b, i, 0)),
            _resident(w_out.shape),
        ] + prep_out,
        out_shape=[
            jax.ShapeDtypeStruct((bsz, DFT_RADIX, inner, A_WIDTH), _BF16),
            jax.ShapeDtypeStruct((bsz, seq, B_WIDTH), _F32),
            jax.ShapeDtypeStruct(w_out.shape, _BF16),
        ] + prep_shape,
        scratch_shapes=[pltpu.VMEM((A_WIDTH // LANES, tm, LANES), _F32),
                        pltpu.VMEM(w_in.shape, _BF16)],
        compiler_params=pltpu.CompilerParams(
            dimension_semantics=("arbitrary", "arbitrary"),
            vmem_limit_bytes=V7X_VMEM_LIMIT_BYTES),
        name="in_proj",
    )(x, gcol, w_in, w_out, ffn_gcol, *ffn_weights)


def _cadd(a, b):
    return (a[0] + b[0], a[1] + b[1])


def _csub(a, b):
    return (a[0] - b[0], a[1] - b[1])


def _dft4(a0, a1, a2, a3):
    s0, s1 = _cadd(a0, a2), _csub(a0, a2)
    s2, s3 = _cadd(a1, a3), _csub(a1, a3)
    return (_cadd(s0, s2), (s1[0] + s3[1], s1[1] - s3[0]),
            _csub(s0, s2), (s1[0] - s3[1], s1[1] + s3[0]))


def _mul_w8(k, z):
    r, i = z
    h = math.sqrt(0.5)
    if k == 0:
        return z
    if k == 1:
        return (h * (r + i), h * (i - r))
    if k == 2:
        return (i, -r)
    return (h * (i - r), -h * (r + i))


def _bfly_unit(r0, lanes, inner, twc_ref, tws_ref, yr_scr, yi_scr):
    rt = BFLY_ROW_TILE
    z = []
    for jr in range(DFT_RADIX):
        rows = slice(jr * inner + r0, jr * inner + r0 + rt)
        yr = yr_scr[rows, lanes]
        yi = yi_scr[rows, lanes]
        if jr == 0:
            z.append((yr, yi))
        else:
            tc = twc_ref[rows, :]
            ts = tws_ref[rows, :]
            z.append((yr * tc + yi * ts, yi * tc - yr * ts))
    ev = _dft4(z[0], z[2], z[4], z[6])
    od = _dft4(z[1], z[3], z[5], z[7])
    for k in range(4):
        w = _mul_w8(k, od[k])
        lo = _cadd(ev[k], w)
        hi = _csub(ev[k], w)
        rows_lo = slice(k * inner + r0, k * inner + r0 + rt)
        rows_hi = slice((k + 4) * inner + r0, (k + 4) * inner + r0 + rt)
        yr_scr[rows_lo, lanes] = lo[0]
        yi_scr[rows_lo, lanes] = lo[1]
        yr_scr[rows_hi, lanes] = hi[0]
        yi_scr[rows_hi, lanes] = hi[1]


def _fnet_kernel(a_ref, cs_ref, twc_ref, tws_ref, cdsd_ref, map_ref, y_ref, yr_scr, yi_scr):
    seq = y_ref.shape[0]
    inner = seq // DFT_RADIX

    for c0 in range(0, A_WIDTH, FNET_COLS):
        cols = slice(c0, c0 + FNET_COLS)
        for jr in range(DFT_RADIX):
            yy = jnp.dot(cs_ref[...], a_ref[jr, :, cols], preferred_element_type=_F32)
            yr_scr[jr * inner:(jr + 1) * inner, cols] = yy[:inner]
            yi_scr[jr * inner:(jr + 1) * inner, cols] = yy[inner:]

        for r0 in range(0, inner, BFLY_ROW_TILE):
            for l0 in range(c0, c0 + FNET_COLS, LANES):
                _bfly_unit(r0, slice(l0, l0 + LANES), inner, twc_ref, tws_ref, yr_scr, yi_scr)

        for hd in range(c0 // HEAD_DIM, (c0 + FNET_COLS) // HEAD_DIM):
            lanes = slice(hd * HEAD_DIM, (hd + 1) * HEAD_DIM)
            fmap = map_ref[hd].astype(_BF16)
            for m0 in range(0, seq, FNET_ROW_BLOCK):
                rows = slice(m0, m0 + FNET_ROW_BLOCK)
                lhs = jnp.concatenate([yr_scr[rows, lanes].astype(_BF16),
                                       yi_scr[rows, lanes].astype(_BF16)], axis=1)
                f = jnp.dot(lhs, cdsd_ref[...], preferred_element_type=_F32)
                ya = jnp.dot(f.astype(_BF16), fmap, preferred_element_type=_F32)
                y_ref[rows, lanes] = ya.astype(_BF16)


def _fnet(a_perm, consts, fmap):
    bsz, _, inner, _ = a_perm.shape
    seq = inner * DFT_RADIX
    cs, twc, tws, cdsd = consts
    return pl.pallas_call(
        _fnet_kernel,
        grid=(bsz,),
        in_specs=[
            pl.BlockSpec((None, DFT_RADIX, inner, A_WIDTH), lambda b: (b, 0, 0, 0)),
            _resident(cs.shape), _resident(twc.shape), _resident(tws.shape),
            _resident(cdsd.shape), _resident(fmap.shape),
        ],
        out_specs=pl.BlockSpec((None, seq, A_WIDTH), lambda b: (b, 0, 0)),
        out_shape=jax.ShapeDtypeStruct((bsz, seq, A_WIDTH), _BF16),
        scratch_shapes=[
            pltpu.VMEM((seq, A_WIDTH), _F32),
            pltpu.VMEM((seq, A_WIDTH), _F32),
        ],
        compiler_params=pltpu.CompilerParams(
            dimension_semantics=("arbitrary",),
            vmem_limit_bytes=V7X_VMEM_LIMIT_BYTES),
        name="fnet",
    )(a_perm, cs, twc, tws, cdsd, fmap)


def _dft_constants(seq):
    inner = seq // DFT_RADIX
    k = np.arange(inner, dtype=np.float64)
    ang = 2.0 * np.pi * np.outer(k, k) / inner
    cs = np.concatenate([np.cos(ang), -np.sin(ang)], axis=0)
    jr = np.arange(DFT_RADIX, dtype=np.float64)[:, None]
    tw = 2.0 * np.pi * (jr * k[None, :]) / seq
    twc = np.repeat(np.cos(tw).reshape(seq, 1), LANES, axis=1)
    tws = np.repeat(np.sin(tw).reshape(seq, 1), LANES, axis=1)
    d = np.arange(HEAD_DIM, dtype=np.float64)
    angd = 2.0 * np.pi * np.outer(d, d) / HEAD_DIM
    scale = 1.0 / math.sqrt(seq * HEAD_DIM)
    cdsd = np.concatenate([np.cos(angd), np.sin(angd)], axis=0) * scale
    return (jnp.asarray(cs, _F32).astype(_BF16), jnp.asarray(twc, _F32), jnp.asarray(tws, _F32),
            jnp.asarray(cdsd, _F32).astype(_BF16))


def _conv_fill_window(tile, tiles_per_seq, main_ref, prev_ref, next_ref, win_scr):
    tm = main_ref.shape[0]
    halo = CONV_HALO
    pos = jnp.zeros((halo, 1), jnp.int32) + tile % tiles_per_seq
    win_scr[0:halo, :] = jnp.where(pos == 0, 0.0, prev_ref[...])
    win_scr[halo:halo + tm, :] = main_ref[...]
    win_scr[halo + tm:, :] = jnp.where(pos == tiles_per_seq - 1, 0.0, next_ref[...])


def _conv_unit(rc, lt, win_scr, cw_ref, cb_ref, lg_ref, lb_ref, out_ref, ct=CONV_ROW_TILE):
    halo = CONV_HALO
    r0 = rc * ct
    lanes = slice(lt * LANES, (lt + 1) * LANES)
    first = halo - CONV_PAD
    span = ct + 2 * halo
    win = win_scr[r0:r0 + span, lanes]
    acc = None
    for s in range(SUBLANES):
        rot = win if s == 0 else pltpu.roll(win, span - s, axis=0)
        for q in range((first + CONV_WIDTH - 1) // SUBLANES + 1):
            k = SUBLANES * q + s - first
            if 0 <= k < CONV_WIDTH:
                term = rot[SUBLANES * q:SUBLANES * q + ct] * cw_ref[k:k + 1, lanes]
                acc = term if acc is None else acc + term
    cv = acc + cb_ref[:, lanes]
    mu = jnp.mean(cv, axis=-1, keepdims=True)
    dv = cv - mu
    var = jnp.mean(dv * dv, axis=-1, keepdims=True)
    half_yn = (dv * jax.lax.rsqrt(var + LN_EPS)) * (lg_ref[:, lanes] * 0.5) + lb_ref[:, lanes] * 0.5
    y = half_yn * _one_plus_tanh(half_yn)
    out_ref[r0:r0 + ct, lanes] = y.astype(_BF16)


def _conv_units(tm, ct=CONV_ROW_TILE):
    return [(rc, lt) for rc in range(tm // ct) for lt in range(B_WIDTH // LANES)]


def _conv_scratch(tm):
    return [pltpu.VMEM((tm + 2 * CONV_HALO, B_WIDTH), _F32)]


def _conv_first_kernel(main_ref, next_ref, cw_ref, cb_ref, lg_ref, lb_ref, out_ref, win_scr):
    _conv_fill_window(0, 2, main_ref, next_ref, next_ref, win_scr)
    for rc, lt in _conv_units(main_ref.shape[0], CONV_FIRST_ROW_TILE):
        _conv_unit(rc, lt, win_scr, cw_ref, cb_ref, lg_ref, lb_ref, out_ref, CONV_FIRST_ROW_TILE)


def _conv_first(u2d, conv_w, conv_b, ln_g, ln_b):
    tm = TOKEN_TILE
    return pl.pallas_call(
        _conv_first_kernel,
        grid=(1,),
        in_specs=[
            pl.BlockSpec((tm, B_WIDTH), lambda i: (0, 0)),
            pl.BlockSpec((CONV_HALO, B_WIDTH), lambda i: (tm // CONV_HALO, 0)),
            _resident(conv_w.shape), _resident(conv_b.shape), _resident(ln_g.shape),
            _resident(ln_b.shape),
        ],
        out_specs=pl.BlockSpec((tm, B_WIDTH), lambda i: (0, 0)),
        out_shape=jax.ShapeDtypeStruct((tm, B_WIDTH), _BF16),
        scratch_shapes=_conv_scratch(tm),
        compiler_params=pltpu.CompilerParams(
            dimension_semantics=("arbitrary",),
            vmem_limit_bytes=V7X_VMEM_LIMIT_BYTES),
        name="conv_first",
    )(u2d, u2d, conv_w, conv_b, ln_g, ln_b)


def _pool_tile(tile, tiles_per_seq, x_ref, prev_ref, next_ref, gcol_ref, pm_ref, ps_ref, hp_scr,
               pm_scr):
    tm, d = x_ref.shape
    seq = tm * tiles_per_seq
    n = tm + 2 * POOL_HALO
    gd = d // len(POOL_WINDOWS)

    @pl.when(tile == 0)
    def _():
        for gi in range(len(POOL_WINDOWS)):
            pm_scr[gi] = (pm_ref[gi] * gcol_ref[gi * gd:(gi + 1) * gd, :]).astype(_BF16)

    xv = x_ref[...]
    hm = _rms_scale(xv)
    start = (tile % tiles_per_seq) * tm
    halo_iota = jax.lax.broadcasted_iota(jnp.int32, (POOL_HALO, 1), 0)
    prev_ok = (start - POOL_HALO + halo_iota) >= 0
    next_ok = (start + tm + halo_iota) < seq
    hp_scr[0:POOL_HALO, :] = jnp.where(prev_ok, _rms_scale(prev_ref[...]), 0.0)
    hp_scr[POOL_HALO:POOL_HALO + tm, :] = hm
    hp_scr[POOL_HALO + tm:, :] = jnp.where(next_ok, _rms_scale(next_ref[...]), 0.0)

    edge_iota = jax.lax.broadcasted_iota(jnp.int32, (POOL_HALO, LANES), 0)
    ys = []
    for gi, w in enumerate(POOL_WINDOWS):
        lanes = slice(gi * gd, (gi + 1) * gd)
        half = w // 2
        fwd = hp_scr[:, lanes]
        span = 1
        while span < half:
            fwd = fwd + pltpu.roll(fwd, n - span, axis=0)
            span *= 2
        centred = fwd + pltpu.roll(fwd, half, axis=0)
        win = centred[POOL_HALO:POOL_HALO + tm]

        def mean_rows(r0):
            pos = start + r0 + edge_iota
            cnt = jnp.minimum(pos + (w - half), seq) - jnp.maximum(pos - half, 0)
            inv = 1.0 / cnt.astype(_F32)
            return win[r0:r0 + POOL_HALO] * jnp.concatenate([inv] * (gd // LANES), axis=1)

        mean = jnp.concatenate([mean_rows(0), win[POOL_HALO:tm - POOL_HALO] * (1.0 / w),
                                mean_rows(tm - POOL_HALO)], axis=0)
        pg = mean - hm[:, lanes]
        ys.append(jnp.dot(pg.astype(_BF16), pm_scr[gi], preferred_element_type=_F32))
    return xv + jnp.concatenate(ys, axis=1) * ps_ref[...]


def _ffn_kernel(*refs, has_mixer, has_pool, has_final, has_prep, tiles_per_seq, n_chunks):
    it = iter(refs)
    x_ref = next(it)
    if has_mixer:
        ya_ref = next(it)
        yb0_ref = next(it)
        u_refs = (next(it), next(it), next(it))
        conv_refs = (next(it), next(it), next(it), next(it))
        wo_ref = next(it)
    if has_pool:
        pool_refs = (next(it), next(it), next(it), next(it), next(it))
    wgu_ref = next(it)
    wd_ref = next(it)
    if has_prep:
        prep_in = (next(it), next(it), next(it), next(it))
    if has_final:
        fg_ref = next(it)
    o_ref = next(it)
    if has_prep:
        prep_out = (next(it), next(it))
    h_scr = next(it)
    act_scr = next(it)
    if has_mixer:
        yb_scr = next(it)
        win_scr = next(it)
    if has_pool:
        hp_scr = next(it)
        pm_scr = next(it)

    i = pl.program_id(0)
    if has_prep:
        @pl.when(i < n_chunks)
        def _():
            _prep_ffn_weights(*prep_in, *prep_out)

    tm = x_ref.shape[0]
    if has_pool:
        xv = _pool_tile(i, tiles_per_seq, x_ref, *pool_refs, hp_scr, pm_scr)
    else:
        xv = x_ref[...]
    conv_some = lambda n_units: None
    if has_mixer:
        n = pl.num_programs(0)
        slot = i % 2

        @pl.when(i == 0)
        def _():
            yb_scr[0] = yb0_ref[...]

        yb = yb_scr[slot]
        _conv_fill_window(jnp.minimum(i + 1, n - 1), tiles_per_seq, *u_refs, win_scr)
        pending = iter(_conv_units(tm))

        def conv_some(n_units):
            for _ in range(n_units):
                _conv_unit(*next(pending), win_scr, *conv_refs, yb_scr.at[1 - slot])

        conv_some(len(_conv_units(tm)) - n_chunks * CONV_UNITS_PER_DOT)
        yab = jnp.concatenate([ya_ref[...], yb], axis=1)
        xv = xv + jnp.dot(yab, wo_ref[...], preferred_element_type=_F32)
    h_scr[...] = _rms_scale(xv).astype(_BF16)

    gu_width = wgu_ref.shape[2]
    for c in range(n_chunks):
        conv_some(CONV_UNITS_PER_DOT)
        gu = jnp.dot(h_scr[...], wgu_ref[c], preferred_element_type=_F32)
        for b0 in range(0, gu_width, 2 * LANES):
            half_gate = gu[:, b0:b0 + LANES]
            up = gu[:, b0 + LANES:b0 + 2 * LANES]
            f0 = (c * gu_width + b0) // 2
            act = (half_gate * up) * _one_plus_tanh(half_gate)
            act_scr[:, f0:f0 + LANES] = act.astype(_BF16)
    acc = xv + jnp.dot(act_scr[...], wd_ref[...], preferred_element_type=_F32)
    if has_final:
        acc = _rmsnorm(acc, fg_ref[...])
    o_ref[...] = acc


def _ffn(x2d, wgu, wd, mixer=None, pool=None, final_g=None, prep=None, seq=None,
         tile_rows=TOKEN_TILE):
    tokens, d = x2d.shape
    n_chunks, _, _ = wgu.shape
    ff = wd.shape[0]
    tm = tile_rows
    tiles_per_seq = seq // tm
    n_tiles = tokens // tm
    has_mixer = mixer is not None
    has_pool = pool is not None
    has_final = final_g is not None
    has_prep = prep is not None
    assert n_tiles >= n_chunks
    args = [x2d]
    specs = [pl.BlockSpec((tm, d), lambda i: (i, 0))]
    out_specs = [pl.BlockSpec((tm, d), lambda i: (i, 0))]
    out_shape = [jax.ShapeDtypeStruct((tokens, d), _F32)]
    scratch = [pltpu.VMEM((tm, d), _BF16), pltpu.VMEM((tm, ff), _BF16)]
    if has_mixer:
        ya2d, yb0, u2d, conv_w, conv_b, ln_g, ln_b, w_out = mixer
        hb = tm // CONV_HALO
        n_halo = tokens // CONV_HALO
        nxt = lambda i: jnp.minimum(i + 1, n_tiles - 1)
        args += [ya2d, yb0, u2d, u2d, u2d, conv_w, conv_b, ln_g, ln_b, w_out]
        specs += [
            pl.BlockSpec((tm, A_WIDTH), lambda i: (i, 0)),
            _resident(yb0.shape),
            pl.BlockSpec((tm, B_WIDTH), lambda i: (nxt(i), 0)),
            pl.BlockSpec((CONV_HALO, B_WIDTH), lambda i: (jnp.maximum(nxt(i) * hb - 1, 0), 0)),
            pl.BlockSpec((CONV_HALO, B_WIDTH),
                         lambda i: (jnp.minimum((nxt(i) + 1) * hb, n_halo - 1), 0)),
            _resident(conv_w.shape), _resident(conv_b.shape), _resident(ln_g.shape),
            _resident(ln_b.shape), _resident(w_out.shape),
        ]
        scratch += [pltpu.VMEM((2, tm, B_WIDTH), _BF16)] + _conv_scratch(tm)
    if has_pool:
        g_mix, pool_map, pool_scale = pool
        hb = tm // POOL_HALO
        n_halo = tokens // POOL_HALO
        args += [x2d, x2d, g_mix, pool_map, pool_scale]
        specs += [
            pl.BlockSpec((POOL_HALO, d), lambda i: (jnp.maximum(i * hb - 1, 0), 0)),
            pl.BlockSpec((POOL_HALO, d), lambda i: (jnp.minimum((i + 1) * hb, n_halo - 1), 0)),
            _resident(g_mix.shape), _resident(pool_map.shape), _resident(pool_scale.shape),
        ]
        scratch += [pltpu.VMEM((tm + 2 * POOL_HALO, d), _F32), pltpu.VMEM(pool_map.shape, _BF16)]
    args += [wgu, wd]
    specs += [_resident(wgu.shape), _resident(wd.shape)]
    if has_prep:
        prep_in, prep_out, prep_shape = _prep_specs(
            *prep, lambda i: jnp.minimum(i, n_chunks - 1))
        args += list(prep[:4])
        specs += prep_in
        out_specs += prep_out
        out_shape += prep_shape
    if has_final:
        args.append(final_g)
        specs.append(_resident(final_g.shape))
    outs = pl.pallas_call(
        functools.partial(_ffn_kernel, has_mixer=has_mixer, has_pool=has_pool, has_final=has_final,
                          has_prep=has_prep, tiles_per_seq=tiles_per_seq, n_chunks=n_chunks),
        grid=(n_tiles,),
        in_specs=specs,
        out_specs=out_specs,
        out_shape=out_shape,
        scratch_shapes=scratch,
        compiler_params=pltpu.CompilerParams(
            dimension_semantics=("arbitrary",),
            vmem_limit_bytes=V7X_VMEM_LIMIT_BYTES),
        name="ffn_mixer" if has_mixer else "ffn_final",
    )(*args)
    return outs if has_prep else outs[0]


def kernel(x, norm_mix_g, norm_ffn_g, w_in_ab, fnet_map, conv_w, conv_b, conv_ln_g, conv_ln_b,
           w_out_ab, pool_map, pool_scale, ffn_w_gate, ffn_w_up, ffn_w_down, final_g):
    bsz, seq, d = x.shape
    tokens = bsz * seq
    row = lambda v: v.reshape(1, -1)
    col = lambda v: v.reshape(-1, 1)

    ffn_weights = (ffn_w_gate, ffn_w_up, ffn_w_down)
    a_perm, u, w_out, wgu0, wd0 = _in_proj(x, col(norm_mix_g[0]), w_in_ab[0], w_out_ab[0],
                                           col(norm_ffn_g[0]), ffn_weights)
    ya = _fnet(a_perm, _dft_constants(seq), fnet_map[0])
    u2d = u.reshape(tokens, B_WIDTH)
    conv_p = (conv_w[0], row(conv_b[0]), row(conv_ln_g[0]), row(conv_ln_b[0]))
    mixer = (ya.reshape(tokens, A_WIDTH), _conv_first(u2d, *conv_p), u2d, *conv_p, w_out)
    x2, wgu1, wd1 = _ffn(x.reshape(tokens, d), wgu0, wd0, mixer=mixer,
                         prep=(col(norm_ffn_g[1]), *ffn_weights, 1), seq=seq)

    pool = (col(norm_mix_g[1]), pool_map[0], row(pool_scale[0]))
    out = _ffn(x2, wgu1, wd1, pool=pool, final_g=row(final_g), seq=seq, tile_rows=FFN_FINAL_TILE)
    return out.reshape(bsz, seq, d)
```

```python
import functools
import math

import jax
import jax.numpy as jnp
import numpy as np
from jax.experimental import pallas as pl
from jax.experimental.pallas import tpu as pltpu

RMS_EPS = 1e-6
LN_EPS = 1e-5

A_HEADS = 4
HEAD_DIM = 128
A_WIDTH = A_HEADS * HEAD_DIM
B_WIDTH = 512
CONV_WIDTH = 31
CONV_PAD = CONV_WIDTH // 2
POOL_WINDOWS = (2, 4, 8, 16)
POOL_HALO = 8

LANES = 128
SUBLANES = 8

DFT_RADIX = 8

V7X_VMEM_LIMIT_BYTES = 56 * 1024 * 1024

TOKEN_TILE = 512
IN_PROJ_TILE = 1024
FFN_FINAL_TILE = 512
FF_PREP_CHUNK = 256
BFLY_ROW_TILE = 8
FNET_COLS = 256
FNET_ROW_BLOCK = 512
CONV_HALO = 16
CONV_ROW_TILE = 16
CONV_FIRST_ROW_TILE = 64
POOL_ROW_TILE = 32

_F32 = jnp.float32
_BF16 = jnp.bfloat16


def _resident(shape):
    nd = len(shape)
    return pl.BlockSpec(shape, lambda *_: (0,) * nd, pipeline_mode=pl.Buffered(1))


def _rms_scale(xv):
    return xv * jax.lax.rsqrt(jnp.mean(xv * xv, axis=-1, keepdims=True) + RMS_EPS)


def _rmsnorm(xv, g):
    return _rms_scale(xv) * g


def _one_plus_tanh(half_v):
    return 1.0 + jnp.tanh(half_v)


def _prep_ffn_weights(gcol_ref, wg_ref, wu_ref, wd_ref, wgu_dst, wd_dst):
    cw = wg_ref.shape[1]
    gain = gcol_ref[...]
    half_gain = gain * 0.5
    for b in range(cw // LANES):
        src = slice(b * LANES, (b + 1) * LANES)
        wgu_dst[:, 2 * b * LANES:(2 * b + 1) * LANES] = (wg_ref[:, src] * half_gain).astype(_BF16)
        wgu_dst[:, (2 * b + 1) * LANES:(2 * b + 2) * LANES] = (wu_ref[:, src] * gain).astype(_BF16)
    wd_dst[...] = wd_ref[...].astype(_BF16)


def _prep_specs(gcol, wg_all, wu_all, wd_all, layer, chunk_of):
    _, d, ff = wg_all.shape
    cw = FF_PREP_CHUNK
    in_specs = [
        _resident(gcol.shape),
        pl.BlockSpec((None, d, cw), lambda *idx: (layer, 0, chunk_of(*idx))),
        pl.BlockSpec((None, d, cw), lambda *idx: (layer, 0, chunk_of(*idx))),
        pl.BlockSpec((None, cw, d), lambda *idx: (layer, chunk_of(*idx), 0)),
    ]
    out_specs = [
        pl.BlockSpec((None, d, 2 * cw), lambda *idx: (chunk_of(*idx), 0, 0)),
        pl.BlockSpec((cw, d), lambda *idx: (chunk_of(*idx), 0)),
    ]
    out_shape = [jax.ShapeDtypeStruct((ff // cw, d, 2 * cw), _BF16),
                 jax.ShapeDtypeStruct((ff, d), _BF16)]
    return in_specs, out_specs, out_shape


def _in_proj_kernel(x_ref, gcol_ref, w_ref, wo_ref, fgcol_ref, wg_ref, wu_ref, wd_ref,
                    a_ref, u_ref, wo_out, wgu_out, wd_out, a_scr, w_scr, *, n_chunks):
    step = pl.program_id(0) * pl.num_programs(1) + pl.program_id(1)

    @pl.when(step < n_chunks)
    def _():
        _prep_ffn_weights(fgcol_ref, wg_ref, wu_ref, wd_ref, wgu_out, wd_out)

    @pl.when(step == 0)
    def _():
        gain = gcol_ref[...]
        w_scr[:, :A_WIDTH] = (w_ref[:, :A_WIDTH] * gain).astype(_BF16)
        w_scr[:, A_WIDTH:] = (w_ref[:, A_WIDTH:] * (gain * 0.5)).astype(_BF16)
        wo_out[...] = wo_ref[...].astype(_BF16)

    h = _rms_scale(x_ref[...]).astype(_BF16)
    p = jnp.dot(h, w_scr[...], preferred_element_type=_F32)
    rows = a_scr.shape[1] // DFT_RADIX
    for lt in range(A_WIDTH // LANES):
        lanes = slice(lt * LANES, (lt + 1) * LANES)
        a_scr[lt] = p[:, lanes]
        for jr in range(DFT_RADIX):
            a_ref[jr, :, lanes] = a_scr[lt, pl.ds(jr, rows, stride=DFT_RADIX), :].astype(_BF16)
    half_v = p[:, A_WIDTH:A_WIDTH + B_WIDTH]
    half_gate = p[:, A_WIDTH + B_WIDTH:]
    u_ref[...] = half_v * _one_plus_tanh(half_gate)


def _in_proj(x, gcol, w_in, w_out, ffn_gcol, ffn_weights):
    bsz, seq, d = x.shape
    tm = IN_PROJ_TILE
    inner = seq // DFT_RADIX
    steps_per_batch = seq // tm
    n_chunks = ffn_weights[0].shape[2] // FF_PREP_CHUNK
    assert bsz * steps_per_batch >= n_chunks
    chunk_of = lambda b, i: jnp.minimum(b * steps_per_batch + i, n_chunks - 1)
    prep_in, prep_out, prep_shape = _prep_specs(ffn_gcol, *ffn_weights, 0, chunk_of)
    return pl.pallas_call(
        functools.partial(_in_proj_kernel, n_chunks=n_chunks),
        grid=(bsz, steps_per_batch),
        in_specs=[
            pl.BlockSpec((None, tm, d), lambda b, i: (b, i, 0)),
            _resident(gcol.shape),
            _resident(w_in.shape),
            _resident(w_out.shape),
        ] + prep_in,
        out_specs=[
            pl.BlockSpec((None, DFT_RADIX, tm // DFT_RADIX, A_WIDTH), lambda b, i: (b, 0, i, 0)),
            pl.BlockSpec((None, tm, B_WIDTH), lambda b, i: (b, i, 0)),
            _resident(w_out.shape),
        ] + prep_out,
        out_shape=[
            jax.ShapeDtypeStruct((bsz, DFT_RADIX, inner, A_WIDTH), _BF16),
            jax.ShapeDtypeStruct((bsz, seq, B_WIDTH), _F32),
            jax.ShapeDtypeStruct(w_out.shape, _BF16),
        ] + prep_shape,
        scratch_shapes=[pltpu.VMEM((A_WIDTH // LANES, tm, LANES), _F32),
                        pltpu.VMEM(w_in.shape, _BF16)],
        compiler_params=pltpu.CompilerParams(
            dimension_semantics=("arbitrary", "arbitrary"),
            vmem_limit_bytes=V7X_VMEM_LIMIT_BYTES),
        name="in_proj",
    )(x, gcol, w_in, w_out, ffn_gcol, *ffn_weights)


def _cadd(a, b):
    return (a[0] + b[0], a[1] + b[1])


def _csub(a, b):
    return (a[0] - b[0], a[1] - b[1])


def _dft4(a0, a1, a2, a3):
    s0, s1 = _cadd(a0, a2), _csub(a0, a2)
    s2, s3 = _cadd(a1, a3), _csub(a1, a3)
    return (_cadd(s0, s2), (s1[0] + s3[1], s1[1] - s3[0]),
            _csub(s0, s2), (s1[0] - s3[1], s1[1] + s3[0]))


def _mul_w8(k, z):
    r, i = z
    h = math.sqrt(0.5)
    if k == 0:
        return z
    if k == 1:
        return (h * (r + i), h * (i - r))
    if k == 2:
        return (i, -r)
    return (h * (i - r), -h * (r + i))


def _bfly_unit(r0, lanes, inner, twc_ref, tws_ref, yr_scr, yi_scr):
    rt = BFLY_ROW_TILE
    z = []
    for jr in range(DFT_RADIX):
        rows = slice(jr * inner + r0, jr * inner + r0 + rt)
        yr = yr_scr[rows, lanes]
        yi = yi_scr[rows, lanes]
        if jr == 0:
            z.append((yr, yi))
        else:
            tc = twc_ref[rows, :]
            ts = tws_ref[rows, :]
            z.append((yr * tc + yi * ts, yi * tc - yr * ts))
    ev = _dft4(z[0], z[2], z[4], z[6])
    od = _dft4(z[1], z[3], z[5], z[7])
    for k in range(4):
        w = _mul_w8(k, od[k])
        lo = _cadd(ev[k], w)
        hi = _csub(ev[k], w)
        rows_lo = slice(k * inner + r0, k * inner + r0 + rt)
        rows_hi = slice((k + 4) * inner + r0, (k + 4) * inner + r0 + rt)
        yr_scr[rows_lo, lanes] = lo[0]
        yi_scr[rows_lo, lanes] = lo[1]
        yr_scr[rows_hi, lanes] = hi[0]
        yi_scr[rows_hi, lanes] = hi[1]


def _fnet_kernel(a_ref, cs_ref, twc_ref, tws_ref, cdsd_ref, map_ref, y_ref, yr_scr, yi_scr):
    seq = y_ref.shape[0]
    inner = seq // DFT_RADIX

    for c0 in range(0, A_WIDTH, FNET_COLS):
        cols = slice(c0, c0 + FNET_COLS)
        for jr in range(DFT_RADIX):
            yy = jnp.dot(cs_ref[...], a_ref[jr, :, cols], preferred_element_type=_F32)
            yr_scr[jr * inner:(jr + 1) * inner, cols] = yy[:inner]
            yi_scr[jr * inner:(jr + 1) * inner, cols] = yy[inner:]

        for r0 in range(0, inner, BFLY_ROW_TILE):
            for l0 in range(c0, c0 + FNET_COLS, LANES):
                _bfly_unit(r0, slice(l0, l0 + LANES), inner, twc_ref, tws_ref, yr_scr, yi_scr)

        for hd in range(c0 // HEAD_DIM, (c0 + FNET_COLS) // HEAD_DIM):
            lanes = slice(hd * HEAD_DIM, (hd + 1) * HEAD_DIM)
            fmap = map_ref[hd].astype(_BF16)
            for m0 in range(0, seq, FNET_ROW_BLOCK):
                rows = slice(m0, m0 + FNET_ROW_BLOCK)
                lhs = jnp.concatenate([yr_scr[rows, lanes].astype(_BF16),
                                       yi_scr[rows, lanes].astype(_BF16)], axis=1)
                f = jnp.dot(lhs, cdsd_ref[...], preferred_element_type=_F32)
                ya = jnp.dot(f.astype(_BF16), fmap, preferred_element_type=_F32)
                y_ref[rows, lanes] = ya.astype(_BF16)


def _fnet(a_perm, consts, fmap):
    bsz, _, inner, _ = a_perm.shape
    seq = inner * DFT_RADIX
    cs, twc, tws, cdsd = consts
    return pl.pallas_call(
        _fnet_kernel,
        grid=(bsz,),
        in_specs=[
            pl.BlockSpec((None, DFT_RADIX, inner, A_WIDTH), lambda b: (b, 0, 0, 0)),
            _resident(cs.shape), _resident(twc.shape), _resident(tws.shape),
            _resident(cdsd.shape), _resident(fmap.shape),
        ],
        out_specs=pl.BlockSpec((None, seq, A_WIDTH), lambda b: (b, 0, 0)),
        out_shape=jax.ShapeDtypeStruct((bsz, seq, A_WIDTH), _BF16),
        scratch_shapes=[
            pltpu.VMEM((seq, A_WIDTH), _F32),
            pltpu.VMEM((seq, A_WIDTH), _F32),
        ],
        compiler_params=pltpu.CompilerParams(
            dimension_semantics=("arbitrary",),
            vmem_limit_bytes=V7X_VMEM_LIMIT_BYTES),
        name="fnet",
    )(a_perm, cs, twc, tws, cdsd, fmap)


def _dft_constants(seq):
    inner = seq // DFT_RADIX
    k = np.arange(inner, dtype=np.float64)
    ang = 2.0 * np.pi * np.outer(k, k) / inner
    cs = np.concatenate([np.cos(ang), -np.sin(ang)], axis=0)
    jr = np.arange(DFT_RADIX, dtype=np.float64)[:, None]
    tw = 2.0 * np.pi * (jr * k[None, :]) / seq
    twc = np.repeat(np.cos(tw).reshape(seq, 1), LANES, axis=1)
    tws = np.repeat(np.sin(tw).reshape(seq, 1), LANES, axis=1)
    d = np.arange(HEAD_DIM, dtype=np.float64)
    angd = 2.0 * np.pi * np.outer(d, d) / HEAD_DIM
    scale = 1.0 / math.sqrt(seq * HEAD_DIM)
    cdsd = np.concatenate([np.cos(angd), np.sin(angd)], axis=0) * scale
    return (jnp.asarray(cs, _F32).astype(_BF16), jnp.asarray(twc, _F32), jnp.asarray(tws, _F32),
            jnp.asarray(cdsd, _F32).astype(_BF16))


def _conv_fill_window(tile, tiles_per_seq, main_ref, prev_ref, next_ref, win_scr):
    tm = main_ref.shape[0]
    halo = CONV_HALO
    pos = jnp.zeros((halo, 1), jnp.int32) + tile % tiles_per_seq
    win_scr[0:halo, :] = jnp.where(pos == 0, 0.0, prev_ref[...])
    win_scr[halo:halo + tm, :] = main_ref[...]
    win_scr[halo + tm:, :] = jnp.where(pos == tiles_per_seq - 1, 0.0, next_ref[...])


def _conv_unit(rc, lt, win_scr, cw_ref, cb_ref, lg_ref, lb_ref, out_ref, ct=CONV_ROW_TILE):
    halo = CONV_HALO
    r0 = rc * ct
    lanes = slice(lt * LANES, (lt + 1) * LANES)
    first = halo - CONV_PAD
    span = ct + 2 * halo
    win = win_scr[r0:r0 + span, lanes]
    acc = None
    for s in range(SUBLANES):
        rot = win if s == 0 else pltpu.roll(win, span - s, axis=0)
        for q in range((first + CONV_WIDTH - 1) // SUBLANES + 1):
            k = SUBLANES * q + s - first
            if 0 <= k < CONV_WIDTH:
                term = rot[SUBLANES * q:SUBLANES * q + ct] * cw_ref[k:k + 1, lanes]
                acc = term if acc is None else acc + term
    cv = acc + cb_ref[:, lanes]
    mu = jnp.mean(cv, axis=-1, keepdims=True)
    dv = cv - mu
    var = jnp.mean(dv * dv, axis=-1, keepdims=True)
    half_yn = (dv * jax.lax.rsqrt(var + LN_EPS)) * (lg_ref[:, lanes] * 0.5) + lb_ref[:, lanes] * 0.5
    y = half_yn * _one_plus_tanh(half_yn)
    out_ref[r0:r0 + ct, lanes] = y.astype(_BF16)


def _conv_units(tm, ct=CONV_ROW_TILE):
    return [(rc, lt) for rc in range(tm // ct) for lt in range(B_WIDTH // LANES)]


def _conv_scratch(tm):
    return [pltpu.VMEM((tm + 2 * CONV_HALO, B_WIDTH), _F32)]


def _conv_first_kernel(main_ref, next_ref, cw_ref, cb_ref, lg_ref, lb_ref, out_ref, win_scr):
    _conv_fill_window(0, 2, main_ref, next_ref, next_ref, win_scr)
    for rc, lt in _conv_units(main_ref.shape[0], CONV_FIRST_ROW_TILE):
        _conv_unit(rc, lt, win_scr, cw_ref, cb_ref, lg_ref, lb_ref, out_ref, CONV_FIRST_ROW_TILE)


def _conv_first(u2d, conv_w, conv_b, ln_g, ln_b):
    tm = TOKEN_TILE
    return pl.pallas_call(
        _conv_first_kernel,
        grid=(1,),
        in_specs=[
            pl.BlockSpec((tm, B_WIDTH), lambda i: (0, 0)),
            pl.BlockSpec((CONV_HALO, B_WIDTH), lambda i: (tm // CONV_HALO, 0)),
            _resident(conv_w.shape), _resident(conv_b.shape), _resident(ln_g.shape),
            _resident(ln_b.shape),
        ],
        out_specs=pl.BlockSpec((tm, B_WIDTH), lambda i: (0, 0)),
        out_shape=jax.ShapeDtypeStruct((tm, B_WIDTH), _BF16),
        scratch_shapes=_conv_scratch(tm),
        compiler_params=pltpu.CompilerParams(
            dimension_semantics=("arbitrary",),
            vmem_limit_bytes=V7X_VMEM_LIMIT_BYTES),
        name="conv_first",
    )(u2d, u2d, conv_w, conv_b, ln_g, ln_b)


def _pool_fill_units(tile, tiles_per_seq, x_ref, prev_ref, next_ref, hp_scr):
    tm = x_ref.shape[0]
    seq = tm * tiles_per_seq
    rt = POOL_ROW_TILE

    def halos():
        start = (tile % tiles_per_seq) * tm
        halo_iota = jax.lax.broadcasted_iota(jnp.int32, (POOL_HALO, 1), 0)
        prev_ok = (start - POOL_HALO + halo_iota) >= 0
        next_ok = (start + tm + halo_iota) < seq
        hp_scr[0:POOL_HALO, :] = jnp.where(prev_ok, _rms_scale(prev_ref[...]), 0.0)
        hp_scr[POOL_HALO + tm:, :] = jnp.where(next_ok, _rms_scale(next_ref[...]), 0.0)

    def rows(r0):
        hp_scr[POOL_HALO + r0:POOL_HALO + r0 + rt, :] = _rms_scale(x_ref[r0:r0 + rt, :])

    return [halos] + [functools.partial(rows, r0) for r0 in range(0, tm, rt)]


def _pool_units(tm):
    return [(rc, gi) for rc in range(tm // POOL_ROW_TILE) for gi in range(len(POOL_WINDOWS))]


def _pool_unit(rc, gi, tile, tiles_per_seq, hp_scr, out_ref):
    rt = POOL_ROW_TILE
    w = POOL_WINDOWS[gi]
    half = w // 2
    tm = hp_scr.shape[0] - 2 * POOL_HALO
    gd = hp_scr.shape[1] // len(POOL_WINDOWS)
    seq = tm * tiles_per_seq
    lanes = slice(gi * gd, (gi + 1) * gd)
    r0 = rc * rt
    n = rt + 2 * POOL_HALO
    fwd = hp_scr[r0:r0 + n, lanes]
    span = 1
    while span < half:
        fwd = fwd + pltpu.roll(fwd, n - span, axis=0)
        span *= 2
    centred = fwd + pltpu.roll(fwd, half, axis=0)
    win = centred[POOL_HALO:POOL_HALO + rt]

    def edge_mean(e0):
        pos = ((tile % tiles_per_seq) * tm + r0 + e0
               + jax.lax.broadcasted_iota(jnp.int32, (POOL_HALO, LANES), 0))
        cnt = jnp.minimum(pos + (w - half), seq) - jnp.maximum(pos - half, 0)
        inv = 1.0 / cnt.astype(_F32)
        return win[e0:e0 + POOL_HALO] * jnp.concatenate([inv] * (gd // LANES), axis=1)

    lo = POOL_HALO if rc == 0 else 0
    hi = rt - POOL_HALO if rc == tm // rt - 1 else rt
    parts = [win[lo:hi] * (1.0 / w)]
    if lo:
        parts.insert(0, edge_mean(0))
    if hi < rt:
        parts.append(edge_mean(hi))
    mean = jnp.concatenate(parts, axis=0)
    pgv = mean - hp_scr[POOL_HALO + r0:POOL_HALO + r0 + rt, lanes]
    out_ref[r0:r0 + rt, lanes] = pgv.astype(_BF16)
    return pgv[rt - SUBLANES:, :LANES]


def _derived_zero(dep):
    bits = pltpu.bitcast(dep, jnp.uint32)
    bits = jax.lax.shift_right_logical(jax.lax.shift_right_logical(bits, jnp.uint32(16)), jnp.uint32(16))
    zero = pltpu.bitcast(bits, _F32)
    return jnp.concatenate([zero, zero], axis=0)


def _pool_first_kernel(x_ref, next_ref, out_ref, hp_scr):
    for unit in _pool_fill_units(0, 2, x_ref, next_ref, next_ref, hp_scr):
        unit()
    for rc, gi in _pool_units(x_ref.shape[0]):
        _pool_unit(rc, gi, 0, 2, hp_scr, out_ref)


def _pool_first(x2d):
    tm = TOKEN_TILE
    d = x2d.shape[1]
    return pl.pallas_call(
        _pool_first_kernel,
        grid=(1,),
        in_specs=[
            pl.BlockSpec((tm, d), lambda i: (0, 0)),
            pl.BlockSpec((POOL_HALO, d), lambda i: (tm // POOL_HALO, 0)),
        ],
        out_specs=pl.BlockSpec((tm, d), lambda i: (0, 0)),
        out_shape=jax.ShapeDtypeStruct((tm, d), _BF16),
        scratch_shapes=[pltpu.VMEM((tm + 2 * POOL_HALO, d), _F32)],
        compiler_params=pltpu.CompilerParams(
            dimension_semantics=("arbitrary",),
            vmem_limit_bytes=V7X_VMEM_LIMIT_BYTES),
        name="pool_first",
    )(x2d, x2d)


def _ffn_kernel(*refs, has_mixer, has_pool, has_final, has_prep, tiles_per_seq, n_chunks):
    it = iter(refs)
    x_ref = next(it)
    if has_mixer:
        ya_ref = next(it)
        yb0_ref = next(it)
        u_refs = (next(it), next(it), next(it))
        conv_refs = (next(it), next(it), next(it), next(it))
        wo_ref = next(it)
    if has_pool:
        pg0_ref = next(it)
        xn_refs = (next(it), next(it), next(it))
        gcol_ref, pm_ref, ps_ref = next(it), next(it), next(it)
    wgu_ref = next(it)
    wd_ref = next(it)
    if has_prep:
        prep_in = (next(it), next(it), next(it), next(it))
    if has_final:
        fg_ref = next(it)
    o_ref = next(it)
    if has_prep:
        prep_out = (next(it), next(it))
    h_scr = next(it)
    act_scr = next(it)
    if has_mixer:
        yb_scr = next(it)
        win_scr = next(it)
    if has_pool:
        pg_scr = next(it)
        hp_scr = next(it)
        pm_scr = next(it)

    i = pl.program_id(0)
    n = pl.num_programs(0)
    slot = i % 2
    nxt = jnp.minimum(i + 1, n - 1)
    if has_prep:
        @pl.when(i < n_chunks)
        def _():
            _prep_ffn_weights(*prep_in, *prep_out)

    tm, d = x_ref.shape
    xv = x_ref[...]
    units = []
    if has_mixer:
        @pl.when(i == 0)
        def _():
            yb_scr[0] = yb0_ref[...]

        yb = yb_scr[slot]
        _conv_fill_window(nxt, tiles_per_seq, *u_refs, win_scr)
        units = [functools.partial(_conv_unit, rc, lt, win_scr, *conv_refs, yb_scr.at[1 - slot])
                 for rc, lt in _conv_units(tm)]
    if has_pool:
        gd = d // len(POOL_WINDOWS)

        @pl.when(i == 0)
        def _():
            pg_scr[0] = pg0_ref[...]
            for gi in range(len(POOL_WINDOWS)):
                pm_scr[gi] = (pm_ref[gi] * gcol_ref[gi * gd:(gi + 1) * gd, :]).astype(_BF16)

        pg = pg_scr[slot]
        units = _pool_fill_units(nxt, tiles_per_seq, *xn_refs, hp_scr) + [
            functools.partial(_pool_unit, rc, gi, nxt, tiles_per_seq, hp_scr, pg_scr.at[1 - slot])
            for rc, gi in _pool_units(tm)]
    pending = iter(units)
    units_per_dot = len(units) // n_chunks

    def side_work(n_units):
        dep = None
        for _ in range(n_units):
            out = next(pending)()
            dep = dep if out is None else out
        return dep

    side_work(len(units) - n_chunks * units_per_dot)
    if has_mixer:
        yab = jnp.concatenate([ya_ref[...], yb], axis=1)
        xv = xv + jnp.dot(yab, wo_ref[...], preferred_element_type=_F32)
    if has_pool:
        ys = [jnp.dot(pg[:, gi * gd:(gi + 1) * gd], pm_scr[gi], preferred_element_type=_F32)
              for gi in range(len(POOL_WINDOWS))]
        xv = xv + jnp.concatenate(ys, axis=1) * ps_ref[...]
    h = _rms_scale(xv)
    h_scr[...] = h.astype(_BF16)
    h_tile = h[:2 * SUBLANES, :LANES]

    gu_width = wgu_ref.shape[2]
    for c in range(n_chunks):
        dep = side_work(units_per_dot)
        if dep is not None:
            h_scr[:2 * SUBLANES, :LANES] = (h_tile + _derived_zero(dep)).astype(_BF16)
        gu = jnp.dot(h_scr[...], wgu_ref[c], preferred_element_type=_F32)
        for b0 in range(0, gu_width, 2 * LANES):
            half_gate = gu[:, b0:b0 + LANES]
            up = gu[:, b0 + LANES:b0 + 2 * LANES]
            f0 = (c * gu_width + b0) // 2
            act = (half_gate * up) * _one_plus_tanh(half_gate)
            act_scr[:, f0:f0 + LANES] = act.astype(_BF16)
    acc = xv + jnp.dot(act_scr[...], wd_ref[...], preferred_element_type=_F32)
    if has_final:
        acc = _rmsnorm(acc, fg_ref[...])
    o_ref[...] = acc


def _ffn(x2d, wgu, wd, mixer=None, pool=None, final_g=None, prep=None, seq=None,
         tile_rows=TOKEN_TILE):
    tokens, d = x2d.shape
    n_chunks, _, _ = wgu.shape
    ff = wd.shape[0]
    tm = tile_rows
    tiles_per_seq = seq // tm
    n_tiles = tokens // tm
    has_mixer = mixer is not None
    has_pool = pool is not None
    has_final = final_g is not None
    has_prep = prep is not None
    assert n_tiles >= n_chunks
    args = [x2d]
    specs = [pl.BlockSpec((tm, d), lambda i: (i, 0))]
    out_specs = [pl.BlockSpec((tm, d), lambda i: (i, 0))]
    out_shape = [jax.ShapeDtypeStruct((tokens, d), _F32)]
    scratch = [pltpu.VMEM((tm, d), _BF16), pltpu.VMEM((tm, ff), _BF16)]
    if has_mixer:
        ya2d, yb0, u2d, conv_w, conv_b, ln_g, ln_b, w_out = mixer
        hb = tm // CONV_HALO
        n_halo = tokens // CONV_HALO
        nxt = lambda i: jnp.minimum(i + 1, n_tiles - 1)
        args += [ya2d, yb0, u2d, u2d, u2d, conv_w, conv_b, ln_g, ln_b, w_out]
        specs += [
            pl.BlockSpec((tm, A_WIDTH), lambda i: (i, 0)),
            _resident(yb0.shape),
            pl.BlockSpec((tm, B_WIDTH), lambda i: (nxt(i), 0)),
            pl.BlockSpec((CONV_HALO, B_WIDTH), lambda i: (jnp.maximum(nxt(i) * hb - 1, 0), 0)),
            pl.BlockSpec((CONV_HALO, B_WIDTH),
                         lambda i: (jnp.minimum((nxt(i) + 1) * hb, n_halo - 1), 0)),
            _resident(conv_w.shape), _resident(conv_b.shape), _resident(ln_g.shape),
            _resident(ln_b.shape), _resident(w_out.shape),
        ]
        scratch += [pltpu.VMEM((2, tm, B_WIDTH), _BF16)] + _conv_scratch(tm)
    if has_pool:
        pg0, g_mix, pool_map, pool_scale = pool
        hb = tm // POOL_HALO
        n_halo = tokens // POOL_HALO
        nxt = lambda i: jnp.minimum(i + 1, n_tiles - 1)
        args += [pg0, x2d, x2d, x2d, g_mix, pool_map, pool_scale]
        specs += [
            _resident(pg0.shape),
            pl.BlockSpec((tm, d), lambda i: (nxt(i), 0)),
            pl.BlockSpec((POOL_HALO, d), lambda i: (jnp.maximum(nxt(i) * hb - 1, 0), 0)),
            pl.BlockSpec((POOL_HALO, d), lambda i: (jnp.minimum((nxt(i) + 1) * hb, n_halo - 1), 0)),
            _resident(g_mix.shape), _resident(pool_map.shape), _resident(pool_scale.shape),
        ]
        scratch += [pltpu.VMEM((2, tm, d), _BF16), pltpu.VMEM((tm + 2 * POOL_HALO, d), _F32),
                    pltpu.VMEM(pool_map.shape, _BF16)]
    args += [wgu, wd]
    specs += [_resident(wgu.shape), _resident(wd.shape)]
    if has_prep:
        prep_in, prep_out, prep_shape = _prep_specs(
            *prep, lambda i: jnp.minimum(i, n_chunks - 1))
        args += list(prep[:4])
        specs += prep_in
        out_specs += prep_out
        out_shape += prep_shape
    if has_final:
        args.append(final_g)
        specs.append(_resident(final_g.shape))
    outs = pl.pallas_call(
        functools.partial(_ffn_kernel, has_mixer=has_mixer, has_pool=has_pool, has_final=has_final,
                          has_prep=has_prep, tiles_per_seq=tiles_per_seq, n_chunks=n_chunks),
        grid=(n_tiles,),
        in_specs=specs,
        out_specs=out_specs,
        out_shape=out_shape,
        scratch_shapes=scratch,
        compiler_params=pltpu.CompilerParams(
            dimension_semantics=("arbitrary",),
            vmem_limit_bytes=V7X_VMEM_LIMIT_BYTES),
        name="ffn_mixer" if has_mixer else "ffn_final",
    )(*args)
    return outs if has_prep else outs[0]


def kernel(x, norm_mix_g, norm_ffn_g, w_in_ab, fnet_map, conv_w, conv_b, conv_ln_g, conv_ln_b,
           w_out_ab, pool_map, pool_scale, ffn_w_gate, ffn_w_up, ffn_w_down, final_g):
    bsz, seq, d = x.shape
    tokens = bsz * seq
    row = lambda v: v.reshape(1, -1)
    col = lambda v: v.reshape(-1, 1)

    ffn_weights = (ffn_w_gate, ffn_w_up, ffn_w_down)
    a_perm, u, w_out, wgu0, wd0 = _in_proj(x, col(norm_mix_g[0]), w_in_ab[0], w_out_ab[0],
                                           col(norm_ffn_g[0]), ffn_weights)
    ya = _fnet(a_perm, _dft_constants(seq), fnet_map[0])
    u2d = u.reshape(tokens, B_WIDTH)
    conv_p = (conv_w[0], row(conv_b[0]), row(conv_ln_g[0]), row(conv_ln_b[0]))
    mixer = (ya.reshape(tokens, A_WIDTH), _conv_first(u2d, *conv_p), u2d, *conv_p, w_out)
    x2, wgu1, wd1 = _ffn(x.reshape(tokens, d), wgu0, wd0, mixer=mixer,
                         prep=(col(norm_ffn_g[1]), *ffn_weights, 1), seq=seq)

    pool = (_pool_first(x2), col(norm_mix_g[1]), pool_map[0], row(pool_scale[0]))
    out = _ffn(x2, wgu1, wd1, pool=pool, final_g=row(final_g), seq=seq, tile_rows=FFN_FINAL_TILE)
    return out.reshape(bsz, seq, d)
```

```python
import functools
import math

import jax
import jax.numpy as jnp
import numpy as np
from jax.experimental import pallas as pl
from jax.experimental.pallas import tpu as pltpu

RMS_EPS = 1e-6
LN_EPS = 1e-5

A_HEADS = 4
HEAD_DIM = 128
A_WIDTH = A_HEADS * HEAD_DIM
B_WIDTH = 512
CONV_WIDTH = 31
CONV_PAD = CONV_WIDTH // 2
POOL_WINDOWS = (2, 4, 8, 16)
POOL_HALO = 8

LANES = 128
SUBLANES = 8

DFT_RADIX = 8

V7X_VMEM_LIMIT_BYTES = 56 * 1024 * 1024

TOKEN_TILE = 512
IN_PROJ_TILE = 1024
FFN_FINAL_TILE = 512
FF_PREP_CHUNK = 256
BFLY_ROW_TILE = 16
FNET_COLS = 256
CONV_HALO = 16
CONV_ROW_TILE = 16
CONV_FIRST_ROW_TILE = 64
POOL_ROW_TILE = 32

_F32 = jnp.float32
_BF16 = jnp.bfloat16


def _resident(shape):
    nd = len(shape)
    return pl.BlockSpec(shape, lambda *_: (0,) * nd, pipeline_mode=pl.Buffered(1))


def _rms_scale(xv):
    return xv * jax.lax.rsqrt(jnp.mean(xv * xv, axis=-1, keepdims=True) + RMS_EPS)


def _rmsnorm(xv, g):
    return _rms_scale(xv) * g


def _one_plus_tanh(half_v):
    return 1.0 + jnp.tanh(half_v)


def _prep_ffn_weights(gcol_ref, wg_ref, wu_ref, wd_ref, wgu_dst, wd_dst):
    cw = wg_ref.shape[1]
    gain = gcol_ref[...]
    half_gain = gain * 0.5
    for b in range(cw // LANES):
        src = slice(b * LANES, (b + 1) * LANES)
        wgu_dst[:, 2 * b * LANES:(2 * b + 1) * LANES] = (wg_ref[:, src] * half_gain).astype(_BF16)
        wgu_dst[:, (2 * b + 1) * LANES:(2 * b + 2) * LANES] = (wu_ref[:, src] * gain).astype(_BF16)
    wd_dst[...] = wd_ref[...].astype(_BF16)


def _prep_specs(gcol, wg_all, wu_all, wd_all, layer, chunk_of):
    _, d, ff = wg_all.shape
    cw = FF_PREP_CHUNK
    in_specs = [
        _resident(gcol.shape),
        pl.BlockSpec((None, d, cw), lambda *idx: (layer, 0, chunk_of(*idx))),
        pl.BlockSpec((None, d, cw), lambda *idx: (layer, 0, chunk_of(*idx))),
        pl.BlockSpec((None, cw, d), lambda *idx: (layer, chunk_of(*idx), 0)),
    ]
    out_specs = [
        pl.BlockSpec((None, d, 2 * cw), lambda *idx: (chunk_of(*idx), 0, 0)),
        pl.BlockSpec((cw, d), lambda *idx: (chunk_of(*idx), 0)),
    ]
    out_shape = [jax.ShapeDtypeStruct((ff // cw, d, 2 * cw), _BF16),
                 jax.ShapeDtypeStruct((ff, d), _BF16)]
    return in_specs, out_specs, out_shape


def _in_proj_kernel(x_ref, gcol_ref, w_ref, wo_ref, fgcol_ref, wg_ref, wu_ref, wd_ref,
                    a_ref, u_ref, wo_out, wgu_out, wd_out, a_scr, w_scr, *, n_chunks):
    step = pl.program_id(0) * pl.num_programs(1) + pl.program_id(1)

    @pl.when(step < n_chunks)
    def _():
        _prep_ffn_weights(fgcol_ref, wg_ref, wu_ref, wd_ref, wgu_out, wd_out)

    @pl.when(step == 0)
    def _():
        gain = gcol_ref[...]
        w_scr[:, :A_WIDTH] = (w_ref[:, :A_WIDTH] * gain).astype(_BF16)
        w_scr[:, A_WIDTH:] = (w_ref[:, A_WIDTH:] * (gain * 0.5)).astype(_BF16)
        wo_out[...] = wo_ref[...].astype(_BF16)

    h = _rms_scale(x_ref[...]).astype(_BF16)
    pa = jnp.dot(h, w_scr[:, :A_WIDTH], preferred_element_type=_F32)
    rows = a_scr.shape[1] // DFT_RADIX
    for lt in range(A_WIDTH // LANES):
        lanes = slice(lt * LANES, (lt + 1) * LANES)
        a_scr[lt] = pa[:, lanes]
        for jr in range(DFT_RADIX):
            a_ref[jr, :, lanes] = a_scr[lt, pl.ds(jr, rows, stride=DFT_RADIX), :].astype(_BF16)
    p = jnp.dot(h, w_scr[:, A_WIDTH:], preferred_element_type=_F32)
    u_ref[...] = p[:, :B_WIDTH] * _one_plus_tanh(p[:, B_WIDTH:])


def _in_proj(x, gcol, w_in, w_out, ffn_gcol, ffn_weights):
    bsz, seq, d = x.shape
    tm = IN_PROJ_TILE
    inner = seq // DFT_RADIX
    steps_per_batch = seq // tm
    n_chunks = ffn_weights[0].shape[2] // FF_PREP_CHUNK
    assert bsz * steps_per_batch >= n_chunks
    chunk_of = lambda b, i: jnp.minimum(b * steps_per_batch + i, n_chunks - 1)
    prep_in, prep_out, prep_shape = _prep_specs(ffn_gcol, *ffn_weights, 0, chunk_of)
    return pl.pallas_call(
        functools.partial(_in_proj_kernel, n_chunks=n_chunks),
        grid=(bsz, steps_per_batch),
        in_specs=[
            pl.BlockSpec((None, tm, d), lambda b, i: (b, i, 0)),
            _resident(gcol.shape),
            _resident(w_in.shape),
            _resident(w_out.shape),
        ] + prep_in,
        out_specs=[
            pl.BlockSpec((None, DFT_RADIX, tm // DFT_RADIX, A_WIDTH), lambda b, i: (b, 0, i, 0)),
            pl.BlockSpec((None, tm, B_WIDTH), lambda b, i: (b, i, 0)),
            _resident(w_out.shape),
        ] + prep_out,
        out_shape=[
            jax.ShapeDtypeStruct((bsz, DFT_RADIX, inner, A_WIDTH), _BF16),
            jax.ShapeDtypeStruct((bsz, seq, B_WIDTH), _F32),
            jax.ShapeDtypeStruct(w_out.shape, _BF16),
        ] + prep_shape,
        scratch_shapes=[pltpu.VMEM((A_WIDTH // LANES, tm, LANES), _F32),
                        pltpu.VMEM(w_in.shape, _BF16)],
        compiler_params=pltpu.CompilerParams(
            dimension_semantics=("arbitrary", "arbitrary"),
            vmem_limit_bytes=V7X_VMEM_LIMIT_BYTES),
        name="in_proj",
    )(x, gcol, w_in, w_out, ffn_gcol, *ffn_weights)


def _cadd(a, b):
    return (a[0] + b[0], a[1] + b[1])


def _csub(a, b):
    return (a[0] - b[0], a[1] - b[1])


def _dft4(a0, a1, a2, a3):
    s0, s1 = _cadd(a0, a2), _csub(a0, a2)
    s2, s3 = _cadd(a1, a3), _csub(a1, a3)
    return (_cadd(s0, s2), (s1[0] + s3[1], s1[1] - s3[0]),
            _csub(s0, s2), (s1[0] - s3[1], s1[1] + s3[0]))


def _mul_w8(k, z):
    r, i = z
    h = math.sqrt(0.5)
    if k == 0:
        return z
    if k == 1:
        return (h * (r + i), h * (i - r))
    if k == 2:
        return (i, -r)
    return (h * (i - r), -h * (r + i))


def _bfly_unit(r0, lanes, inner, twc_ref, tws_ref, yr_scr, yi_scr):
    rt = BFLY_ROW_TILE
    z = []
    for jr in range(DFT_RADIX):
        rows = slice(jr * inner + r0, jr * inner + r0 + rt)
        yr = yr_scr[rows, lanes]
        yi = yi_scr[rows, lanes]
        if jr == 0:
            z.append((yr, yi))
        else:
            tc = twc_ref[rows, :]
            ts = tws_ref[rows, :]
            z.append((yr * tc + yi * ts, yi * tc - yr * ts))
    ev = _dft4(z[0], z[2], z[4], z[6])
    od = _dft4(z[1], z[3], z[5], z[7])
    for k in range(4):
        w = _mul_w8(k, od[k])
        lo = _cadd(ev[k], w)
        hi = _csub(ev[k], w)
        rows_lo = slice(k * inner + r0, k * inner + r0 + rt)
        rows_hi = slice((k + 4) * inner + r0, (k + 4) * inner + r0 + rt)
        yr_scr[rows_lo, lanes] = lo[0]
        yi_scr[rows_lo, lanes] = lo[1]
        yr_scr[rows_hi, lanes] = hi[0]
        yi_scr[rows_hi, lanes] = hi[1]


def _fnet_kernel(a_ref, cs_ref, twc_ref, tws_ref, cdsd_ref, map_ref, y_ref, yr_scr, yi_scr):
    seq = y_ref.shape[0]
    inner = seq // DFT_RADIX

    for c0 in range(0, A_WIDTH, FNET_COLS):
        cols = slice(c0, c0 + FNET_COLS)
        for jr in range(DFT_RADIX):
            yy = jnp.dot(cs_ref[...], a_ref[jr, :, cols], preferred_element_type=_F32)
            yr_scr[jr * inner:(jr + 1) * inner, cols] = yy[:inner]
            yi_scr[jr * inner:(jr + 1) * inner, cols] = yy[inner:]

        for r0 in range(0, inner, BFLY_ROW_TILE):
            for l0 in range(c0, c0 + FNET_COLS, LANES):
                _bfly_unit(r0, slice(l0, l0 + LANES), inner, twc_ref, tws_ref, yr_scr, yi_scr)

        for hd in range(c0 // HEAD_DIM, (c0 + FNET_COLS) // HEAD_DIM):
            lanes = slice(hd * HEAD_DIM, (hd + 1) * HEAD_DIM)
            lhs = jnp.concatenate([yr_scr[:, lanes].astype(_BF16),
                                   yi_scr[:, lanes].astype(_BF16)], axis=1)
            f = jnp.dot(lhs, cdsd_ref[...], preferred_element_type=_F32)
            ya = jnp.dot(f.astype(_BF16), map_ref[hd].astype(_BF16), preferred_element_type=_F32)
            y_ref[:, lanes] = ya.astype(_BF16)


def _fnet(a_perm, consts, fmap):
    bsz, _, inner, _ = a_perm.shape
    seq = inner * DFT_RADIX
    cs, twc, tws, cdsd = consts
    return pl.pallas_call(
        _fnet_kernel,
        grid=(bsz,),
        in_specs=[
            pl.BlockSpec((None, DFT_RADIX, inner, A_WIDTH), lambda b: (b, 0, 0, 0)),
            _resident(cs.shape), _resident(twc.shape), _resident(tws.shape),
            _resident(cdsd.shape), _resident(fmap.shape),
        ],
        out_specs=pl.BlockSpec((None, seq, A_WIDTH), lambda b: (b, 0, 0)),
        out_shape=jax.ShapeDtypeStruct((bsz, seq, A_WIDTH), _BF16),
        scratch_shapes=[
            pltpu.VMEM((seq, A_WIDTH), _F32),
            pltpu.VMEM((seq, A_WIDTH), _F32),
        ],
        compiler_params=pltpu.CompilerParams(
            dimension_semantics=("arbitrary",),
            vmem_limit_bytes=V7X_VMEM_LIMIT_BYTES),
        name="fnet",
    )(a_perm, cs, twc, tws, cdsd, fmap)


def _dft_constants(seq):
    inner = seq // DFT_RADIX
    k = np.arange(inner, dtype=np.float64)
    ang = 2.0 * np.pi * np.outer(k, k) / inner
    cs = np.concatenate([np.cos(ang), -np.sin(ang)], axis=0)
    jr = np.arange(DFT_RADIX, dtype=np.float64)[:, None]
    tw = 2.0 * np.pi * (jr * k[None, :]) / seq
    twc = np.repeat(np.cos(tw).reshape(seq, 1), LANES, axis=1)
    tws = np.repeat(np.sin(tw).reshape(seq, 1), LANES, axis=1)
    d = np.arange(HEAD_DIM, dtype=np.float64)
    angd = 2.0 * np.pi * np.outer(d, d) / HEAD_DIM
    scale = 1.0 / math.sqrt(seq * HEAD_DIM)
    cdsd = np.concatenate([np.cos(angd), np.sin(angd)], axis=0) * scale
    return (jnp.asarray(cs, _F32).astype(_BF16), jnp.asarray(twc, _F32), jnp.asarray(tws, _F32),
            jnp.asarray(cdsd, _F32).astype(_BF16))


def _conv_fill_window(tile, tiles_per_seq, main_ref, prev_ref, next_ref, win_scr):
    tm = main_ref.shape[0]
    halo = CONV_HALO
    pos = jnp.zeros((halo, 1), jnp.int32) + tile % tiles_per_seq
    win_scr[0:halo, :] = jnp.where(pos == 0, 0.0, prev_ref[...])
    win_scr[halo:halo + tm, :] = main_ref[...]
    win_scr[halo + tm:, :] = jnp.where(pos == tiles_per_seq - 1, 0.0, next_ref[...])


def _conv_unit(rc, lt, win_scr, cw_ref, cb_ref, lg_ref, lb_ref, out_ref, ct=CONV_ROW_TILE):
    halo = CONV_HALO
    r0 = rc * ct
    lanes = slice(lt * LANES, (lt + 1) * LANES)
    first = halo - CONV_PAD
    span = ct + 2 * halo
    win = win_scr[r0:r0 + span, lanes]
    acc = None
    for s in range(SUBLANES):
        rot = win if s == 0 else pltpu.roll(win, span - s, axis=0)
        for q in range((first + CONV_WIDTH - 1) // SUBLANES + 1):
            k = SUBLANES * q + s - first
            if 0 <= k < CONV_WIDTH:
                term = rot[SUBLANES * q:SUBLANES * q + ct] * cw_ref[k:k + 1, lanes]
                acc = term if acc is None else acc + term
    cv = acc + cb_ref[:, lanes]
    mu = jnp.mean(cv, axis=-1, keepdims=True)
    dv = cv - mu
    var = jnp.mean(dv * dv, axis=-1, keepdims=True)
    half_yn = (dv * jax.lax.rsqrt(var + LN_EPS)) * (lg_ref[:, lanes] * 0.5) + lb_ref[:, lanes] * 0.5
    y = half_yn * _one_plus_tanh(half_yn)
    out_ref[r0:r0 + ct, lanes] = y.astype(_BF16)


def _conv_units(tm, ct=CONV_ROW_TILE):
    return [(rc, lt) for rc in range(tm // ct) for lt in range(B_WIDTH // LANES)]


def _conv_scratch(tm):
    return [pltpu.VMEM((tm + 2 * CONV_HALO, B_WIDTH), _F32)]


def _conv_first_kernel(main_ref, next_ref, cw_ref, cb_ref, lg_ref, lb_ref, out_ref, win_scr):
    _conv_fill_window(0, 2, main_ref, next_ref, next_ref, win_scr)
    for rc, lt in _conv_units(main_ref.shape[0], CONV_FIRST_ROW_TILE):
        _conv_unit(rc, lt, win_scr, cw_ref, cb_ref, lg_ref, lb_ref, out_ref, CONV_FIRST_ROW_TILE)


def _conv_first(u2d, conv_w, conv_b, ln_g, ln_b):
    tm = TOKEN_TILE
    return pl.pallas_call(
        _conv_first_kernel,
        grid=(1,),
        in_specs=[
            pl.BlockSpec((tm, B_WIDTH), lambda i: (0, 0)),
            pl.BlockSpec((CONV_HALO, B_WIDTH), lambda i: (tm // CONV_HALO, 0)),
            _resident(conv_w.shape), _resident(conv_b.shape), _resident(ln_g.shape),
            _resident(ln_b.shape),
        ],
        out_specs=pl.BlockSpec((tm, B_WIDTH), lambda i: (0, 0)),
        out_shape=jax.ShapeDtypeStruct((tm, B_WIDTH), _BF16),
        scratch_shapes=_conv_scratch(tm),
        compiler_params=pltpu.CompilerParams(
            dimension_semantics=("arbitrary",),
            vmem_limit_bytes=V7X_VMEM_LIMIT_BYTES),
        name="conv_first",
    )(u2d, u2d, conv_w, conv_b, ln_g, ln_b)


def _pool_fill_units(tile, tiles_per_seq, x_ref, prev_ref, next_ref, hp_scr):
    tm = x_ref.shape[0]
    seq = tm * tiles_per_seq
    rt = POOL_ROW_TILE

    def halos():
        start = (tile % tiles_per_seq) * tm
        halo_iota = jax.lax.broadcasted_iota(jnp.int32, (POOL_HALO, 1), 0)
        prev_ok = (start - POOL_HALO + halo_iota) >= 0
        next_ok = (start + tm + halo_iota) < seq
        hp_scr[0:POOL_HALO, :] = jnp.where(prev_ok, _rms_scale(prev_ref[...]), 0.0)
        hp_scr[POOL_HALO + tm:, :] = jnp.where(next_ok, _rms_scale(next_ref[...]), 0.0)

    def rows(r0):
        hp_scr[POOL_HALO + r0:POOL_HALO + r0 + rt, :] = _rms_scale(x_ref[r0:r0 + rt, :])

    return [halos] + [functools.partial(rows, r0) for r0 in range(0, tm, rt)]


def _pool_units(tm):
    return [(rc, gi) for rc in range(tm // POOL_ROW_TILE) for gi in range(len(POOL_WINDOWS))]


def _pool_unit(rc, gi, tile, tiles_per_seq, hp_scr, out_ref):
    rt = POOL_ROW_TILE
    w = POOL_WINDOWS[gi]
    half = w // 2
    tm = hp_scr.shape[0] - 2 * POOL_HALO
    gd = hp_scr.shape[1] // len(POOL_WINDOWS)
    seq = tm * tiles_per_seq
    lanes = slice(gi * gd, (gi + 1) * gd)
    r0 = rc * rt
    n = rt + 2 * POOL_HALO
    fwd = hp_scr[r0:r0 + n, lanes]
    span = 1
    while span < half:
        fwd = fwd + pltpu.roll(fwd, n - span, axis=0)
        span *= 2
    centred = fwd + pltpu.roll(fwd, half, axis=0)
    win = centred[POOL_HALO:POOL_HALO + rt]

    def edge_mean(e0):
        pos = ((tile % tiles_per_seq) * tm + r0 + e0
               + jax.lax.broadcasted_iota(jnp.int32, (POOL_HALO, LANES), 0))
        cnt = jnp.minimum(pos + (w - half), seq) - jnp.maximum(pos - half, 0)
        inv = 1.0 / cnt.astype(_F32)
        return win[e0:e0 + POOL_HALO] * jnp.concatenate([inv] * (gd // LANES), axis=1)

    lo = POOL_HALO if rc == 0 else 0
    hi = rt - POOL_HALO if rc == tm // rt - 1 else rt
    parts = [win[lo:hi] * (1.0 / w)]
    if lo:
        parts.insert(0, edge_mean(0))
    if hi < rt:
        parts.append(edge_mean(hi))
    mean = jnp.concatenate(parts, axis=0)
    pgv = mean - hp_scr[POOL_HALO + r0:POOL_HALO + r0 + rt, lanes]
    out_ref[r0:r0 + rt, lanes] = pgv.astype(_BF16)
    return pgv[rt - SUBLANES:, :LANES]


def _derived_zero(dep):
    bits = pltpu.bitcast(dep, jnp.uint32)
    bits = jax.lax.shift_right_logical(jax.lax.shift_right_logical(bits, jnp.uint32(16)), jnp.uint32(16))
    zero = pltpu.bitcast(bits, _F32)
    return jnp.concatenate([zero, zero], axis=0)


def _pool_first_kernel(x_ref, next_ref, out_ref, hp_scr):
    for unit in _pool_fill_units(0, 2, x_ref, next_ref, next_ref, hp_scr):
        unit()
    for rc, gi in _pool_units(x_ref.shape[0]):
        _pool_unit(rc, gi, 0, 2, hp_scr, out_ref)


def _pool_first(x2d):
    tm = TOKEN_TILE
    d = x2d.shape[1]
    return pl.pallas_call(
        _pool_first_kernel,
        grid=(1,),
        in_specs=[
            pl.BlockSpec((tm, d), lambda i: (0, 0)),
            pl.BlockSpec((POOL_HALO, d), lambda i: (tm // POOL_HALO, 0)),
        ],
        out_specs=pl.BlockSpec((tm, d), lambda i: (0, 0)),
        out_shape=jax.ShapeDtypeStruct((tm, d), _BF16),
        scratch_shapes=[pltpu.VMEM((tm + 2 * POOL_HALO, d), _F32)],
        compiler_params=pltpu.CompilerParams(
            dimension_semantics=("arbitrary",),
            vmem_limit_bytes=V7X_VMEM_LIMIT_BYTES),
        name="pool_first",
    )(x2d, x2d)


def _ffn_kernel(*refs, has_mixer, has_pool, has_final, has_prep, tiles_per_seq, n_chunks):
    it = iter(refs)
    x_ref = next(it)
    if has_mixer:
        ya_ref = next(it)
        yb0_ref = next(it)
        u_refs = (next(it), next(it), next(it))
        conv_refs = (next(it), next(it), next(it), next(it))
        wo_ref = next(it)
    if has_pool:
        pg0_ref = next(it)
        xn_refs = (next(it), next(it), next(it))
        gcol_ref, pm_ref, ps_ref = next(it), next(it), next(it)
    wgu_ref = next(it)
    wd_ref = next(it)
    if has_prep:
        prep_in = (next(it), next(it), next(it), next(it))
    if has_final:
        fg_ref = next(it)
    o_ref = next(it)
    if has_prep:
        prep_out = (next(it), next(it))
    h_scr = next(it)
    act_scr = next(it)
    if has_mixer:
        yb_scr = next(it)
        win_scr = next(it)
    if has_pool:
        pg_scr = next(it)
        hp_scr = next(it)
        pm_scr = next(it)

    i = pl.program_id(0)
    n = pl.num_programs(0)
    slot = i % 2
    nxt = jnp.minimum(i + 1, n - 1)
    if has_prep:
        @pl.when(i < n_chunks)
        def _():
            _prep_ffn_weights(*prep_in, *prep_out)

    tm, d = x_ref.shape
    xv = x_ref[...]
    units = []
    if has_mixer:
        @pl.when(i == 0)
        def _():
            yb_scr[0] = yb0_ref[...]

        yb = yb_scr[slot]
        _conv_fill_window(nxt, tiles_per_seq, *u_refs, win_scr)
        units = [functools.partial(_conv_unit, rc, lt, win_scr, *conv_refs, yb_scr.at[1 - slot])
                 for rc, lt in _conv_units(tm)]
    if has_pool:
        gd = d // len(POOL_WINDOWS)

        @pl.when(i == 0)
        def _():
            pg_scr[0] = pg0_ref[...]
            for gi in range(len(POOL_WINDOWS)):
                pm_scr[gi] = (pm_ref[gi] * gcol_ref[gi * gd:(gi + 1) * gd, :]).astype(_BF16)

        pg = pg_scr[slot]
        units = _pool_fill_units(nxt, tiles_per_seq, *xn_refs, hp_scr) + [
            functools.partial(_pool_unit, rc, gi, nxt, tiles_per_seq, hp_scr, pg_scr.at[1 - slot])
            for rc, gi in _pool_units(tm)]
    pending = iter(units)
    units_per_dot = len(units) // n_chunks

    def side_work(n_units):
        dep = None
        for _ in range(n_units):
            out = next(pending)()
            dep = dep if out is None else out
        return dep

    side_work(len(units) - n_chunks * units_per_dot)
    if has_mixer:
        yab = jnp.concatenate([ya_ref[...], yb], axis=1)
        xv = xv + jnp.dot(yab, wo_ref[...], preferred_element_type=_F32)
    if has_pool:
        ys = [jnp.dot(pg[:, gi * gd:(gi + 1) * gd], pm_scr[gi], preferred_element_type=_F32)
              for gi in range(len(POOL_WINDOWS))]
        xv = xv + jnp.concatenate(ys, axis=1) * ps_ref[...]
    h = _rms_scale(xv)
    hb = h.astype(_BF16)
    if has_pool:
        h_scr[...] = hb
    h_tile = h[:2 * SUBLANES, :LANES]

    gu_width = wgu_ref.shape[2]
    for c in range(n_chunks):
        dep = side_work(units_per_dot)
        if dep is not None:
            h_scr[:2 * SUBLANES, :LANES] = (h_tile + _derived_zero(dep)).astype(_BF16)
            hb = h_scr[...]
        gu = jnp.dot(hb, wgu_ref[c], preferred_element_type=_F32)
        for b0 in range(0, gu_width, 2 * LANES):
            half_gate = gu[:, b0:b0 + LANES]
            up = gu[:, b0 + LANES:b0 + 2 * LANES]
            f0 = (c * gu_width + b0) // 2
            act = (half_gate * up) * _one_plus_tanh(half_gate)
            act_scr[:, f0:f0 + LANES] = act.astype(_BF16)
    acc = xv + jnp.dot(act_scr[...], wd_ref[...], preferred_element_type=_F32)
    if has_final:
        acc = _rmsnorm(acc, fg_ref[...])
    o_ref[...] = acc


def _ffn(x2d, wgu, wd, mixer=None, pool=None, final_g=None, prep=None, seq=None,
         tile_rows=TOKEN_TILE):
    tokens, d = x2d.shape
    n_chunks, _, _ = wgu.shape
    ff = wd.shape[0]
    tm = tile_rows
    tiles_per_seq = seq // tm
    n_tiles = tokens // tm
    has_mixer = mixer is not None
    has_pool = pool is not None
    has_final = final_g is not None
    has_prep = prep is not None
    assert n_tiles >= n_chunks
    args = [x2d]
    specs = [pl.BlockSpec((tm, d), lambda i: (i, 0))]
    out_specs = [pl.BlockSpec((tm, d), lambda i: (i, 0))]
    out_shape = [jax.ShapeDtypeStruct((tokens, d), _F32)]
    scratch = [pltpu.VMEM((tm, d), _BF16), pltpu.VMEM((tm, ff), _BF16)]
    if has_mixer:
        ya2d, yb0, u2d, conv_w, conv_b, ln_g, ln_b, w_out = mixer
        hb = tm // CONV_HALO
        n_halo = tokens // CONV_HALO
        nxt = lambda i: jnp.minimum(i + 1, n_tiles - 1)
        args += [ya2d, yb0, u2d, u2d, u2d, conv_w, conv_b, ln_g, ln_b, w_out]
        specs += [
            pl.BlockSpec((tm, A_WIDTH), lambda i: (i, 0)),
            _resident(yb0.shape),
            pl.BlockSpec((tm, B_WIDTH), lambda i: (nxt(i), 0)),
            pl.BlockSpec((CONV_HALO, B_WIDTH), lambda i: (jnp.maximum(nxt(i) * hb - 1, 0), 0)),
            pl.BlockSpec((CONV_HALO, B_WIDTH),
                         lambda i: (jnp.minimum((nxt(i) + 1) * hb, n_halo - 1), 0)),
            _resident(conv_w.shape), _resident(conv_b.shape), _resident(ln_g.shape),
            _resident(ln_b.shape), _resident(w_out.shape),
        ]
        scratch += [pltpu.VMEM((2, tm, B_WIDTH), _BF16)] + _conv_scratch(tm)
    if has_pool:
        pg0, g_mix, pool_map, pool_scale = pool
        hb = tm // POOL_HALO
        n_halo = tokens // POOL_HALO
        nxt = lambda i: jnp.minimum(i + 1, n_tiles - 1)
        args += [pg0, x2d, x2d, x2d, g_mix, pool_map, pool_scale]
        specs += [
            _resident(pg0.shape),
            pl.BlockSpec((tm, d), lambda i: (nxt(i), 0)),
            pl.BlockSpec((POOL_HALO, d), lambda i: (jnp.maximum(nxt(i) * hb - 1, 0), 0)),
            pl.BlockSpec((POOL_HALO, d), lambda i: (jnp.minimum((nxt(i) + 1) * hb, n_halo - 1), 0)),
            _resident(g_mix.shape), _resident(pool_map.shape), _resident(pool_scale.shape),
        ]
        scratch += [pltpu.VMEM((2, tm, d), _BF16), pltpu.VMEM((tm + 2 * POOL_HALO, d), _F32),
                    pltpu.VMEM(pool_map.shape, _BF16)]
    args += [wgu, wd]
    specs += [_resident(wgu.shape), _resident(wd.shape)]
    if has_prep:
        prep_in, prep_out, prep_shape = _prep_specs(
            *prep, lambda i: jnp.minimum(i, n_chunks - 1))
        args += list(prep[:4])
        specs += prep_in
        out_specs += prep_out
        out_shape += prep_shape
    if has_final:
        args.append(final_g)
        specs.append(_resident(final_g.shape))
    outs = pl.pallas_call(
        functools.partial(_ffn_kernel, has_mixer=has_mixer, has_pool=has_pool, has_final=has_final,
                          has_prep=has_prep, tiles_per_seq=tiles_per_seq, n_chunks=n_chunks),
        grid=(n_tiles,),
        in_specs=specs,
        out_specs=out_specs,
        out_shape=out_shape,
        scratch_shapes=scratch,
        compiler_params=pltpu.CompilerParams(
            dimension_semantics=("arbitrary",),
            vmem_limit_bytes=V7X_VMEM_LIMIT_BYTES),
        name="ffn_mixer" if has_mixer else "ffn_final",
    )(*args)
    return outs if has_prep else outs[0]


def kernel(x, norm_mix_g, norm_ffn_g, w_in_ab, fnet_map, conv_w, conv_b, conv_ln_g, conv_ln_b,
           w_out_ab, pool_map, pool_scale, ffn_w_gate, ffn_w_up, ffn_w_down, final_g):
    bsz, seq, d = x.shape
    tokens = bsz * seq
    row = lambda v: v.reshape(1, -1)
    col = lambda v: v.reshape(-1, 1)

    ffn_weights = (ffn_w_gate, ffn_w_up, ffn_w_down)
    a_perm, u, w_out, wgu0, wd0 = _in_proj(x, col(norm_mix_g[0]), w_in_ab[0], w_out_ab[0],
                                           col(norm_ffn_g[0]), ffn_weights)
    ya = _fnet(a_perm, _dft_constants(seq), fnet_map[0])
    u2d = u.reshape(tokens, B_WIDTH)
    conv_p = (conv_w[0], row(conv_b[0]), row(conv_ln_g[0]), row(conv_ln_b[0]))
    mixer = (ya.reshape(tokens, A_WIDTH), _conv_first(u2d, *conv_p), u2d, *conv_p, w_out)
    x2, wgu1, wd1 = _ffn(x.reshape(tokens, d), wgu0, wd0, mixer=mixer,
                         prep=(col(norm_ffn_g[1]), *ffn_weights, 1), seq=seq)

    pool = (_pool_first(x2), col(norm_mix_g[1]), pool_map[0], row(pool_scale[0]))
    out = _ffn(x2, wgu1, wd1, pool=pool, final_g=row(final_g), seq=seq, tile_rows=FFN_FINAL_TILE)
    return out.reshape(bsz, seq, d)
```

```python
import functools
import math

import jax
import jax.numpy as jnp
import numpy as np
from jax.experimental import pallas as pl
from jax.experimental.pallas import tpu as pltpu

RMS_EPS = 1e-6
LN_EPS = 1e-5

A_HEADS = 4
HEAD_DIM = 128
A_WIDTH = A_HEADS * HEAD_DIM
B_WIDTH = 512
CONV_WIDTH = 31
CONV_PAD = CONV_WIDTH // 2
POOL_WINDOWS = (2, 4, 8, 16)
POOL_HALO = 8

LANES = 128
SUBLANES = 8

DFT_RADIX = 8

V7X_VMEM_LIMIT_BYTES = 56 * 1024 * 1024

TOKEN_TILE = 512
IN_PROJ_TILE = 1024
FF_PREP_CHUNK = 256
BFLY_ROW_TILE = 16
FNET_COLS = 256
CONV_HALO = 16
CONV_ROW_TILE = 16
CONV_FIRST_ROW_TILE = 64
POOL_ROW_TILE = 32

_F32 = jnp.float32
_BF16 = jnp.bfloat16


def _resident(shape):
    nd = len(shape)
    return pl.BlockSpec(shape, lambda *_: (0,) * nd, pipeline_mode=pl.Buffered(1))


def _rms_scale(xv):
    return xv * jax.lax.rsqrt(jnp.mean(xv * xv, axis=-1, keepdims=True) + RMS_EPS)


def _rmsnorm(xv, g):
    return _rms_scale(xv) * g


def _one_plus_tanh(half_v):
    return 1.0 + jnp.tanh(half_v)


def _prep_ffn_weights(gcol_ref, wg_ref, wu_ref, wd_ref, wgu_dst, wd_dst):
    cw = wg_ref.shape[1]
    gain = gcol_ref[...]
    half_gain = gain * 0.5
    for b in range(cw // LANES):
        src = slice(b * LANES, (b + 1) * LANES)
        wgu_dst[:, 2 * b * LANES:(2 * b + 1) * LANES] = (wg_ref[:, src] * half_gain).astype(_BF16)
        wgu_dst[:, (2 * b + 1) * LANES:(2 * b + 2) * LANES] = (wu_ref[:, src] * gain).astype(_BF16)
    wd_dst[...] = wd_ref[...].astype(_BF16)


def _prep_specs(gcol, wg_all, wu_all, wd_all, layer, chunk_of):
    _, d, ff = wg_all.shape
    cw = FF_PREP_CHUNK
    in_specs = [
        _resident(gcol.shape),
        pl.BlockSpec((None, d, cw), lambda *idx: (layer, 0, chunk_of(*idx))),
        pl.BlockSpec((None, d, cw), lambda *idx: (layer, 0, chunk_of(*idx))),
        pl.BlockSpec((None, cw, d), lambda *idx: (layer, chunk_of(*idx), 0)),
    ]
    out_specs = [
        pl.BlockSpec((None, d, 2 * cw), lambda *idx: (chunk_of(*idx), 0, 0)),
        pl.BlockSpec((cw, d), lambda *idx: (chunk_of(*idx), 0)),
    ]
    out_shape = [jax.ShapeDtypeStruct((ff // cw, d, 2 * cw), _BF16),
                 jax.ShapeDtypeStruct((ff, d), _BF16)]
    return in_specs, out_specs, out_shape


def _in_proj_kernel(x_ref, gcol_ref, w_ref, wo_ref, fgcol_ref, wg_ref, wu_ref, wd_ref,
                    a_ref, u_ref, wo_out, wgu_out, wd_out, a_scr, w_scr, *, n_chunks):
    step = pl.program_id(0) * pl.num_programs(1) + pl.program_id(1)

    @pl.when(step < n_chunks)
    def _():
        _prep_ffn_weights(fgcol_ref, wg_ref, wu_ref, wd_ref, wgu_out, wd_out)

    @pl.when(step == 0)
    def _():
        gain = gcol_ref[...]
        w_scr[:, :A_WIDTH] = (w_ref[:, :A_WIDTH] * gain).astype(_BF16)
        w_scr[:, A_WIDTH:] = (w_ref[:, A_WIDTH:] * (gain * 0.5)).astype(_BF16)
        wo_out[...] = wo_ref[...].astype(_BF16)

    h = _rms_scale(x_ref[...]).astype(_BF16)
    pa = jnp.dot(h, w_scr[:, :A_WIDTH], preferred_element_type=_F32)
    rows = a_scr.shape[1] // DFT_RADIX
    for lt in range(A_WIDTH // LANES):
        lanes = slice(lt * LANES, (lt + 1) * LANES)
        a_scr[lt] = pa[:, lanes]
        for jr in range(DFT_RADIX):
            a_ref[jr, :, lanes] = a_scr[lt, pl.ds(jr, rows, stride=DFT_RADIX), :].astype(_BF16)
    p = jnp.dot(h, w_scr[:, A_WIDTH:], preferred_element_type=_F32)
    u_ref[...] = p[:, :B_WIDTH] * _one_plus_tanh(p[:, B_WIDTH:])


def _in_proj(x, gcol, w_in, w_out, ffn_gcol, ffn_weights):
    bsz, seq, d = x.shape
    tm = IN_PROJ_TILE
    inner = seq // DFT_RADIX
    steps_per_batch = seq // tm
    n_chunks = ffn_weights[0].shape[2] // FF_PREP_CHUNK
    assert bsz * steps_per_batch >= n_chunks
    chunk_of = lambda b, i: jnp.minimum(b * steps_per_batch + i, n_chunks - 1)
    prep_in, prep_out, prep_shape = _prep_specs(ffn_gcol, *ffn_weights, 0, chunk_of)
    return pl.pallas_call(
        functools.partial(_in_proj_kernel, n_chunks=n_chunks),
        grid=(bsz, steps_per_batch),
        in_specs=[
            pl.BlockSpec((None, tm, d), lambda b, i: (b, i, 0)),
            _resident(gcol.shape),
            _resident(w_in.shape),
            _resident(w_out.shape),
        ] + prep_in,
        out_specs=[
            pl.BlockSpec((None, DFT_RADIX, tm // DFT_RADIX, A_WIDTH), lambda b, i: (b, 0, i, 0)),
            pl.BlockSpec((None, tm, B_WIDTH), lambda b, i: (b, i, 0)),
            _resident(w_out.shape),
        ] + prep_out,
        out_shape=[
            jax.ShapeDtypeStruct((bsz, DFT_RADIX, inner, A_WIDTH), _BF16),
            jax.ShapeDtypeStruct((bsz, seq, B_WIDTH), _F32),
            jax.ShapeDtypeStruct(w_out.shape, _BF16),
        ] + prep_shape,
        scratch_shapes=[pltpu.VMEM((A_WIDTH // LANES, tm, LANES), _F32),
                        pltpu.VMEM(w_in.shape, _BF16)],
        compiler_params=pltpu.CompilerParams(
            dimension_semantics=("arbitrary", "arbitrary"),
            vmem_limit_bytes=V7X_VMEM_LIMIT_BYTES),
        name="in_proj",
    )(x, gcol, w_in, w_out, ffn_gcol, *ffn_weights)


def _cadd(a, b):
    return (a[0] + b[0], a[1] + b[1])


def _csub(a, b):
    return (a[0] - b[0], a[1] - b[1])


def _dft4(a0, a1, a2, a3):
    s0, s1 = _cadd(a0, a2), _csub(a0, a2)
    s2, s3 = _cadd(a1, a3), _csub(a1, a3)
    return (_cadd(s0, s2), (s1[0] + s3[1], s1[1] - s3[0]),
            _csub(s0, s2), (s1[0] - s3[1], s1[1] + s3[0]))


def _mul_w8(k, z):
    r, i = z
    h = math.sqrt(0.5)
    if k == 0:
        return z
    if k == 1:
        return (h * (r + i), h * (i - r))
    if k == 2:
        return (i, -r)
    return (h * (i - r), -h * (r + i))


def _bfly_unit(r0, lanes, inner, twc_ref, tws_ref, yr_scr, yi_scr):
    rt = BFLY_ROW_TILE
    z = []
    for jr in range(DFT_RADIX):
        rows = slice(jr * inner + r0, jr * inner + r0 + rt)
        yr = yr_scr[rows, lanes]
        yi = yi_scr[rows, lanes]
        if jr == 0:
            z.append((yr, yi))
        else:
            tc = twc_ref[rows, :]
            ts = tws_ref[rows, :]
            z.append((yr * tc + yi * ts, yi * tc - yr * ts))
    ev = _dft4(z[0], z[2], z[4], z[6])
    od = _dft4(z[1], z[3], z[5], z[7])
    for k in range(4):
        w = _mul_w8(k, od[k])
        lo = _cadd(ev[k], w)
        hi = _csub(ev[k], w)
        rows_lo = slice(k * inner + r0, k * inner + r0 + rt)
        rows_hi = slice((k + 4) * inner + r0, (k + 4) * inner + r0 + rt)
        yr_scr[rows_lo, lanes] = lo[0]
        yi_scr[rows_lo, lanes] = lo[1]
        yr_scr[rows_hi, lanes] = hi[0]
        yi_scr[rows_hi, lanes] = hi[1]


def _fnet_kernel(a_ref, cs_ref, twc_ref, tws_ref, cdsd_ref, map_ref, y_ref, yr_scr, yi_scr):
    seq = y_ref.shape[0]
    inner = seq // DFT_RADIX

    for c0 in range(0, A_WIDTH, FNET_COLS):
        cols = slice(c0, c0 + FNET_COLS)
        for jr in range(DFT_RADIX):
            yy = jnp.dot(cs_ref[...], a_ref[jr, :, cols], preferred_element_type=_F32)
            yr_scr[jr * inner:(jr + 1) * inner, cols] = yy[:inner]
            yi_scr[jr * inner:(jr + 1) * inner, cols] = yy[inner:]

        for r0 in range(0, inner, BFLY_ROW_TILE):
            for l0 in range(c0, c0 + FNET_COLS, LANES):
                _bfly_unit(r0, slice(l0, l0 + LANES), inner, twc_ref, tws_ref, yr_scr, yi_scr)

        for hd in range(c0 // HEAD_DIM, (c0 + FNET_COLS) // HEAD_DIM):
            lanes = slice(hd * HEAD_DIM, (hd + 1) * HEAD_DIM)
            lhs = jnp.concatenate([yr_scr[:, lanes].astype(_BF16),
                                   yi_scr[:, lanes].astype(_BF16)], axis=1)
            f = jnp.dot(lhs, cdsd_ref[...], preferred_element_type=_F32)
            ya = jnp.dot(f.astype(_BF16), map_ref[hd].astype(_BF16), preferred_element_type=_F32)
            y_ref[:, lanes] = ya.astype(_BF16)


def _fnet(a_perm, consts, fmap):
    bsz, _, inner, _ = a_perm.shape
    seq = inner * DFT_RADIX
    cs, twc, tws, cdsd = consts
    return pl.pallas_call(
        _fnet_kernel,
        grid=(bsz,),
        in_specs=[
            pl.BlockSpec((None, DFT_RADIX, inner, A_WIDTH), lambda b: (b, 0, 0, 0)),
            _resident(cs.shape), _resident(twc.shape), _resident(tws.shape),
            _resident(cdsd.shape), _resident(fmap.shape),
        ],
        out_specs=pl.BlockSpec((None, seq, A_WIDTH), lambda b: (b, 0, 0)),
        out_shape=jax.ShapeDtypeStruct((bsz, seq, A_WIDTH), _BF16),
        scratch_shapes=[
            pltpu.VMEM((seq, A_WIDTH), _F32),
            pltpu.VMEM((seq, A_WIDTH), _F32),
        ],
        compiler_params=pltpu.CompilerParams(
            dimension_semantics=("arbitrary",),
            vmem_limit_bytes=V7X_VMEM_LIMIT_BYTES),
        name="fnet",
    )(a_perm, cs, twc, tws, cdsd, fmap)


def _dft_constants(seq):
    inner = seq // DFT_RADIX
    k = np.arange(inner, dtype=np.float64)
    ang = 2.0 * np.pi * np.outer(k, k) / inner
    cs = np.concatenate([np.cos(ang), -np.sin(ang)], axis=0)
    jr = np.arange(DFT_RADIX, dtype=np.float64)[:, None]
    tw = 2.0 * np.pi * (jr * k[None, :]) / seq
    twc = np.repeat(np.cos(tw).reshape(seq, 1), LANES, axis=1)
    tws = np.repeat(np.sin(tw).reshape(seq, 1), LANES, axis=1)
    d = np.arange(HEAD_DIM, dtype=np.float64)
    angd = 2.0 * np.pi * np.outer(d, d) / HEAD_DIM
    scale = 1.0 / math.sqrt(seq * HEAD_DIM)
    cdsd = np.concatenate([np.cos(angd), np.sin(angd)], axis=0) * scale
    return (jnp.asarray(cs, _F32).astype(_BF16), jnp.asarray(twc, _F32), jnp.asarray(tws, _F32),
            jnp.asarray(cdsd, _F32).astype(_BF16))


def _conv_fill_window(tile, tiles_per_seq, main_ref, prev_ref, next_ref, win_scr):
    tm = main_ref.shape[0]
    halo = CONV_HALO
    pos = jnp.zeros((halo, 1), jnp.int32) + tile % tiles_per_seq
    win_scr[0:halo, :] = jnp.where(pos == 0, 0.0, prev_ref[...])
    win_scr[halo:halo + tm, :] = main_ref[...]
    win_scr[halo + tm:, :] = jnp.where(pos == tiles_per_seq - 1, 0.0, next_ref[...])


def _conv_unit(rc, lt, win_scr, cw_ref, cb_ref, lg_ref, lb_ref, out_ref, ct=CONV_ROW_TILE):
    halo = CONV_HALO
    r0 = rc * ct
    lanes = slice(lt * LANES, (lt + 1) * LANES)
    first = halo - CONV_PAD
    span = ct + 2 * halo
    win = win_scr[r0:r0 + span, lanes]
    acc = None
    for s in range(SUBLANES):
        rot = win if s == 0 else pltpu.roll(win, span - s, axis=0)
        for q in range((first + CONV_WIDTH - 1) // SUBLANES + 1):
            k = SUBLANES * q + s - first
            if 0 <= k < CONV_WIDTH:
                term = rot[SUBLANES * q:SUBLANES * q + ct] * cw_ref[k:k + 1, lanes]
                acc = term if acc is None else acc + term
    cv = acc + cb_ref[:, lanes]
    mu = jnp.mean(cv, axis=-1, keepdims=True)
    dv = cv - mu
    var = jnp.mean(dv * dv, axis=-1, keepdims=True)
    half_yn = (dv * jax.lax.rsqrt(var + LN_EPS)) * (lg_ref[:, lanes] * 0.5) + lb_ref[:, lanes] * 0.5
    y = half_yn * _one_plus_tanh(half_yn)
    out_ref[r0:r0 + ct, lanes] = y.astype(_BF16)


def _conv_units(tm, ct=CONV_ROW_TILE):
    return [(rc, lt) for rc in range(tm // ct) for lt in range(B_WIDTH // LANES)]


def _conv_scratch(tm):
    return [pltpu.VMEM((tm + 2 * CONV_HALO, B_WIDTH), _F32)]


def _conv_first_kernel(main_ref, next_ref, cw_ref, cb_ref, lg_ref, lb_ref, out_ref, win_scr):
    _conv_fill_window(0, 2, main_ref, next_ref, next_ref, win_scr)
    for rc, lt in _conv_units(main_ref.shape[0], CONV_FIRST_ROW_TILE):
        _conv_unit(rc, lt, win_scr, cw_ref, cb_ref, lg_ref, lb_ref, out_ref, CONV_FIRST_ROW_TILE)


def _conv_first(u2d, conv_w, conv_b, ln_g, ln_b):
    tm = TOKEN_TILE
    return pl.pallas_call(
        _conv_first_kernel,
        grid=(1,),
        in_specs=[
            pl.BlockSpec((tm, B_WIDTH), lambda i: (0, 0)),
            pl.BlockSpec((CONV_HALO, B_WIDTH), lambda i: (tm // CONV_HALO, 0)),
            _resident(conv_w.shape), _resident(conv_b.shape), _resident(ln_g.shape),
            _resident(ln_b.shape),
        ],
        out_specs=pl.BlockSpec((tm, B_WIDTH), lambda i: (0, 0)),
        out_shape=jax.ShapeDtypeStruct((tm, B_WIDTH), _BF16),
        scratch_shapes=_conv_scratch(tm),
        compiler_params=pltpu.CompilerParams(
            dimension_semantics=("arbitrary",),
            vmem_limit_bytes=V7X_VMEM_LIMIT_BYTES),
        name="conv_first",
    )(u2d, u2d, conv_w, conv_b, ln_g, ln_b)


def _pool_fill_units(tile, tiles_per_seq, x_ref, prev_ref, next_ref, hp_scr):
    tm = x_ref.shape[0]
    seq = tm * tiles_per_seq
    rt = POOL_ROW_TILE

    def halos():
        start = (tile % tiles_per_seq) * tm
        halo_iota = jax.lax.broadcasted_iota(jnp.int32, (POOL_HALO, 1), 0)
        prev_ok = (start - POOL_HALO + halo_iota) >= 0
        next_ok = (start + tm + halo_iota) < seq
        hp_scr[0:POOL_HALO, :] = jnp.where(prev_ok, _rms_scale(prev_ref[...]), 0.0)
        hp_scr[POOL_HALO + tm:, :] = jnp.where(next_ok, _rms_scale(next_ref[...]), 0.0)

    def rows(r0):
        hp_scr[POOL_HALO + r0:POOL_HALO + r0 + rt, :] = _rms_scale(x_ref[r0:r0 + rt, :])

    return [halos] + [functools.partial(rows, r0) for r0 in range(0, tm, rt)]


def _pool_units(tm):
    return [(rc, gi) for rc in range(tm // POOL_ROW_TILE) for gi in range(len(POOL_WINDOWS))]


def _pool_unit(rc, gi, tile, tiles_per_seq, hp_scr, out_ref):
    rt = POOL_ROW_TILE
    w = POOL_WINDOWS[gi]
    half = w // 2
    tm = hp_scr.shape[0] - 2 * POOL_HALO
    gd = hp_scr.shape[1] // len(POOL_WINDOWS)
    seq = tm * tiles_per_seq
    lanes = slice(gi * gd, (gi + 1) * gd)
    r0 = rc * rt
    n = rt + 2 * POOL_HALO
    fwd = hp_scr[r0:r0 + n, lanes]
    span = 1
    while span < half:
        fwd = fwd + pltpu.roll(fwd, n - span, axis=0)
        span *= 2
    centred = fwd + pltpu.roll(fwd, half, axis=0)
    win = centred[POOL_HALO:POOL_HALO + rt]

    def edge_mean(e0):
        pos = ((tile % tiles_per_seq) * tm + r0 + e0
               + jax.lax.broadcasted_iota(jnp.int32, (POOL_HALO, LANES), 0))
        cnt = jnp.minimum(pos + (w - half), seq) - jnp.maximum(pos - half, 0)
        inv = 1.0 / cnt.astype(_F32)
        return win[e0:e0 + POOL_HALO] * jnp.concatenate([inv] * (gd // LANES), axis=1)

    lo = POOL_HALO if rc == 0 else 0
    hi = rt - POOL_HALO if rc == tm // rt - 1 else rt
    parts = [win[lo:hi] * (1.0 / w)]
    if lo:
        parts.insert(0, edge_mean(0))
    if hi < rt:
        parts.append(edge_mean(hi))
    mean = jnp.concatenate(parts, axis=0)
    pgv = mean - hp_scr[POOL_HALO + r0:POOL_HALO + r0 + rt, lanes]
    out_ref[r0:r0 + rt, lanes] = pgv.astype(_BF16)
    return pgv[rt - SUBLANES:, :LANES]


def _derived_zero(dep):
    bits = pltpu.bitcast(dep, jnp.uint32)
    bits = jax.lax.shift_right_logical(jax.lax.shift_right_logical(bits, jnp.uint32(16)), jnp.uint32(16))
    zero = pltpu.bitcast(bits, _F32)
    return jnp.concatenate([zero, zero], axis=0)


def _pool_first_kernel(x_ref, next_ref, out_ref, hp_scr):
    for unit in _pool_fill_units(0, 2, x_ref, next_ref, next_ref, hp_scr):
        unit()
    for rc, gi in _pool_units(x_ref.shape[0]):
        _pool_unit(rc, gi, 0, 2, hp_scr, out_ref)


def _pool_first(x2d):
    tm = TOKEN_TILE
    d = x2d.shape[1]
    return pl.pallas_call(
        _pool_first_kernel,
        grid=(1,),
        in_specs=[
            pl.BlockSpec((tm, d), lambda i: (0, 0)),
            pl.BlockSpec((POOL_HALO, d), lambda i: (tm // POOL_HALO, 0)),
        ],
        out_specs=pl.BlockSpec((tm, d), lambda i: (0, 0)),
        out_shape=jax.ShapeDtypeStruct((tm, d), _BF16),
        scratch_shapes=[pltpu.VMEM((tm + 2 * POOL_HALO, d), _F32)],
        compiler_params=pltpu.CompilerParams(
            dimension_semantics=("arbitrary",),
            vmem_limit_bytes=V7X_VMEM_LIMIT_BYTES),
        name="pool_first",
    )(x2d, x2d)


def _ffn_kernel(*refs, has_mixer, has_pool, has_final, has_prep, tiles_per_seq, n_chunks):
    it = iter(refs)
    x_ref = next(it)
    if has_mixer:
        ya_ref = next(it)
        yb0_ref = next(it)
        u_refs = (next(it), next(it), next(it))
        conv_refs = (next(it), next(it), next(it), next(it))
        wo_ref = next(it)
    if has_pool:
        pg0_ref = next(it)
        xn_refs = (next(it), next(it), next(it))
        gcol_ref, pm_ref, ps_ref = next(it), next(it), next(it)
    wgu_ref = next(it)
    wd_ref = next(it)
    if has_prep:
        prep_in = (next(it), next(it), next(it), next(it))
    if has_final:
        fg_ref = next(it)
    o_ref = next(it)
    if has_prep:
        prep_out = (next(it), next(it))
    h_scr = next(it)
    act_scr = next(it)
    if has_mixer:
        yb_scr = next(it)
        win_scr = next(it)
    if has_pool:
        pg_scr = next(it)
        hp_scr = next(it)
        pm_scr = next(it)
        xv_scr = next(it)
        hn_scr = next(it)

    i = pl.program_id(0)
    n = pl.num_programs(0)
    slot = i % 2
    nxt = jnp.minimum(i + 1, n - 1)
    if has_prep:
        @pl.when(i < n_chunks)
        def _():
            _prep_ffn_weights(*prep_in, *prep_out)

    tm, d = x_ref.shape
    xv = x_ref[...]
    units = []
    if has_mixer:
        @pl.when(i == 0)
        def _():
            yb_scr[0] = yb0_ref[...]

        yb = yb_scr[slot]
        _conv_fill_window(nxt, tiles_per_seq, *u_refs, win_scr)
        units = [functools.partial(_conv_unit, rc, lt, win_scr, *conv_refs, yb_scr.at[1 - slot])
                 for rc, lt in _conv_units(tm)]
    if has_pool:
        gd = d // len(POOL_WINDOWS)

        @pl.when(i == 0)
        def _():
            pg_scr[0] = pg0_ref[...]
            for gi in range(len(POOL_WINDOWS)):
                pm_scr[gi] = (pm_ref[gi] * gcol_ref[gi * gd:(gi + 1) * gd, :]).astype(_BF16)

        def front(x_val, pg_val):
            ys = [jnp.dot(pg_val[:, gi * gd:(gi + 1) * gd], pm_scr[gi], preferred_element_type=_F32)
                  for gi in range(len(POOL_WINDOWS))]
            xf = x_val + jnp.concatenate(ys, axis=1) * ps_ref[...]
            return xf, _rms_scale(xf).astype(_BF16)

        @pl.when(i == 0)
        def _():
            xv_scr[0], hn_scr[0] = front(x_ref[...], pg0_ref[...])

        units = _pool_fill_units(nxt, tiles_per_seq, *xn_refs, hp_scr) + [
            functools.partial(_pool_unit, rc, gi, nxt, tiles_per_seq, hp_scr, pg_scr.at[1 - slot])
            for rc, gi in _pool_units(tm)]
    pending = iter(units)
    units_per_dot = len(units) // n_chunks

    def side_work(n_units):
        dep = None
        for _ in range(n_units):
            out = next(pending)()
            dep = dep if out is None else out
        return dep

    side_work(len(units) - n_chunks * units_per_dot)
    if has_mixer:
        yab = jnp.concatenate([ya_ref[...], yb], axis=1)
        xv = xv + jnp.dot(yab, wo_ref[...], preferred_element_type=_F32)
    if has_pool:
        h_scr = hn_scr.at[slot]
        xv = xv_scr[slot]
        hb = h_scr[...]
        h_tile = hb[:2 * SUBLANES, :LANES].astype(_F32)
    else:
        hb = _rms_scale(xv).astype(_BF16)

    gu_width = wgu_ref.shape[2]
    for c in range(n_chunks):
        dep = side_work(units_per_dot)
        if dep is not None:
            h_scr[:2 * SUBLANES, :LANES] = (h_tile + _derived_zero(dep)).astype(_BF16)
            hb = h_scr[...]
        gu = jnp.dot(hb, wgu_ref[c], preferred_element_type=_F32)
        for b0 in range(0, gu_width, 2 * LANES):
            half_gate = gu[:, b0:b0 + LANES]
            up = gu[:, b0 + LANES:b0 + 2 * LANES]
            f0 = (c * gu_width + b0) // 2
            act = (half_gate * up) * _one_plus_tanh(half_gate)
            act_scr[:, f0:f0 + LANES] = act.astype(_BF16)
    acc = xv + jnp.dot(act_scr[...], wd_ref[...], preferred_element_type=_F32)
    if has_final:
        acc = _rmsnorm(acc, fg_ref[...])
    o_ref[...] = acc
    if has_pool:
        xv_scr[1 - slot], hn_scr[1 - slot] = front(xn_refs[0][...], pg_scr[1 - slot])


def _ffn(x2d, wgu, wd, mixer=None, pool=None, final_g=None, prep=None, seq=None):
    tokens, d = x2d.shape
    n_chunks, _, _ = wgu.shape
    ff = wd.shape[0]
    tm = TOKEN_TILE
    tiles_per_seq = seq // tm
    n_tiles = tokens // tm
    has_mixer = mixer is not None
    has_pool = pool is not None
    has_final = final_g is not None
    has_prep = prep is not None
    assert n_tiles >= n_chunks
    args = [x2d]
    specs = [pl.BlockSpec((tm, d), lambda i: (i, 0))]
    out_specs = [pl.BlockSpec((tm, d), lambda i: (i, 0))]
    out_shape = [jax.ShapeDtypeStruct((tokens, d), _F32)]
    scratch = [pltpu.VMEM((tm, d), _BF16), pltpu.VMEM((tm, ff), _BF16)]
    if has_mixer:
        ya2d, yb0, u2d, conv_w, conv_b, ln_g, ln_b, w_out = mixer
        hb = tm // CONV_HALO
        n_halo = tokens // CONV_HALO
        nxt = lambda i: jnp.minimum(i + 1, n_tiles - 1)
        args += [ya2d, yb0, u2d, u2d, u2d, conv_w, conv_b, ln_g, ln_b, w_out]
        specs += [
            pl.BlockSpec((tm, A_WIDTH), lambda i: (i, 0)),
            _resident(yb0.shape),
            pl.BlockSpec((tm, B_WIDTH), lambda i: (nxt(i), 0)),
            pl.BlockSpec((CONV_HALO, B_WIDTH), lambda i: (jnp.maximum(nxt(i) * hb - 1, 0), 0)),
            pl.BlockSpec((CONV_HALO, B_WIDTH),
                         lambda i: (jnp.minimum((nxt(i) + 1) * hb, n_halo - 1), 0)),
            _resident(conv_w.shape), _resident(conv_b.shape), _resident(ln_g.shape),
            _resident(ln_b.shape), _resident(w_out.shape),
        ]
        scratch += [pltpu.VMEM((2, tm, B_WIDTH), _BF16)] + _conv_scratch(tm)
    if has_pool:
        pg0, g_mix, pool_map, pool_scale = pool
        hb = tm // POOL_HALO
        n_halo = tokens // POOL_HALO
        nxt = lambda i: jnp.minimum(i + 1, n_tiles - 1)
        args += [pg0, x2d, x2d, x2d, g_mix, pool_map, pool_scale]
        specs += [
            _resident(pg0.shape),
            pl.BlockSpec((tm, d), lambda i: (nxt(i), 0)),
            pl.BlockSpec((POOL_HALO, d), lambda i: (jnp.maximum(nxt(i) * hb - 1, 0), 0)),
            pl.BlockSpec((POOL_HALO, d), lambda i: (jnp.minimum((nxt(i) + 1) * hb, n_halo - 1), 0)),
            _resident(g_mix.shape), _resident(pool_map.shape), _resident(pool_scale.shape),
        ]
        scratch += [pltpu.VMEM((2, tm, d), _BF16), pltpu.VMEM((tm + 2 * POOL_HALO, d), _F32),
                    pltpu.VMEM(pool_map.shape, _BF16), pltpu.VMEM((2, tm, d), _F32),
                    pltpu.VMEM((2, tm, d), _BF16)]
    args += [wgu, wd]
    specs += [_resident(wgu.shape), _resident(wd.shape)]
    if has_prep:
        prep_in, prep_out, prep_shape = _prep_specs(
            *prep, lambda i: jnp.minimum(i, n_chunks - 1))
        args += list(prep[:4])
        specs += prep_in
        out_specs += prep_out
        out_shape += prep_shape
    if has_final:
        args.append(final_g)
        specs.append(_resident(final_g.shape))
    outs = pl.pallas_call(
        functools.partial(_ffn_kernel, has_mixer=has_mixer, has_pool=has_pool, has_final=has_final,
                          has_prep=has_prep, tiles_per_seq=tiles_per_seq, n_chunks=n_chunks),
        grid=(n_tiles,),
        in_specs=specs,
        out_specs=out_specs,
        out_shape=out_shape,
        scratch_shapes=scratch,
        compiler_params=pltpu.CompilerParams(
            dimension_semantics=("arbitrary",),
            vmem_limit_bytes=V7X_VMEM_LIMIT_BYTES),
        name="ffn_mixer" if has_mixer else "ffn_final",
    )(*args)
    return outs if has_prep else outs[0]


def kernel(x, norm_mix_g, norm_ffn_g, w_in_ab, fnet_map, conv_w, conv_b, conv_ln_g, conv_ln_b,
           w_out_ab, pool_map, pool_scale, ffn_w_gate, ffn_w_up, ffn_w_down, final_g):
    bsz, seq, d = x.shape
    tokens = bsz * seq
    row = lambda v: v.reshape(1, -1)
    col = lambda v: v.reshape(-1, 1)

    ffn_weights = (ffn_w_gate, ffn_w_up, ffn_w_down)
    a_perm, u, w_out, wgu0, wd0 = _in_proj(x, col(norm_mix_g[0]), w_in_ab[0], w_out_ab[0],
                                           col(norm_ffn_g[0]), ffn_weights)
    ya = _fnet(a_perm, _dft_constants(seq), fnet_map[0])
    u2d = u.reshape(tokens, B_WIDTH)
    conv_p = (conv_w[0], row(conv_b[0]), row(conv_ln_g[0]), row(conv_ln_b[0]))
    mixer = (ya.reshape(tokens, A_WIDTH), _conv_first(u2d, *conv_p), u2d, *conv_p, w_out)
    x2, wgu1, wd1 = _ffn(x.reshape(tokens, d), wgu0, wd0, mixer=mixer,
                         prep=(col(norm_ffn_g[1]), *ffn_weights, 1), seq=seq)

    pool = (_pool_first(x2), col(norm_mix_g[1]), pool_map[0], row(pool_scale[0]))
    out = _ffn(x2, wgu1, wd1, pool=pool, final_g=row(final_g), seq=seq)
    return out.reshape(bsz, seq, d)
```

```python
import functools
import math

import jax
import jax.numpy as jnp
import numpy as np
from jax.experimental import pallas as pl
from jax.experimental.pallas import tpu as pltpu

RMS_EPS = 1e-6
LN_EPS = 1e-5

A_HEADS = 4
HEAD_DIM = 128
A_WIDTH = A_HEADS * HEAD_DIM
B_WIDTH = 512
CONV_WIDTH = 31
CONV_PAD = CONV_WIDTH // 2
POOL_WINDOWS = (2, 4, 8, 16)
POOL_HALO = 8

LANES = 128
SUBLANES = 8

DFT_RADIX = 8

V7X_VMEM_LIMIT_BYTES = 56 * 1024 * 1024

TOKEN_TILE = 512
IN_PROJ_TILE = 1024
X_RING_SLOTS = 3
FF_PREP_CHUNK = 256
BFLY_ROW_TILE = 16
FNET_COLS = 256
CONV_HALO = 16
CONV_ROW_TILE = 16
CONV_FIRST_ROW_TILE = 64
POOL_ROW_TILE = 32

_F32 = jnp.float32
_BF16 = jnp.bfloat16


def _resident(shape):
    nd = len(shape)
    return pl.BlockSpec(shape, lambda *_: (0,) * nd, pipeline_mode=pl.Buffered(1))


def _rms_scale(xv):
    return xv * jax.lax.rsqrt(jnp.mean(xv * xv, axis=-1, keepdims=True) + RMS_EPS)


def _rmsnorm(xv, g):
    return _rms_scale(xv) * g


def _one_plus_tanh(half_v):
    return 1.0 + jnp.tanh(half_v)


def _prep_ffn_weights(gcol_ref, wg_ref, wu_ref, wd_ref, wgu_dst, wd_dst):
    cw = wg_ref.shape[1]
    gain = gcol_ref[...]
    half_gain = gain * 0.5
    for b in range(cw // LANES):
        src = slice(b * LANES, (b + 1) * LANES)
        wgu_dst[:, 2 * b * LANES:(2 * b + 1) * LANES] = (wg_ref[:, src] * half_gain).astype(_BF16)
        wgu_dst[:, (2 * b + 1) * LANES:(2 * b + 2) * LANES] = (wu_ref[:, src] * gain).astype(_BF16)
    wd_dst[...] = wd_ref[...].astype(_BF16)


def _prep_specs(gcol, wg_all, wu_all, wd_all, layer, chunk_of):
    _, d, ff = wg_all.shape
    cw = FF_PREP_CHUNK
    in_specs = [
        _resident(gcol.shape),
        pl.BlockSpec((None, d, cw), lambda *idx: (layer, 0, chunk_of(*idx))),
        pl.BlockSpec((None, d, cw), lambda *idx: (layer, 0, chunk_of(*idx))),
        pl.BlockSpec((None, cw, d), lambda *idx: (layer, chunk_of(*idx), 0)),
    ]
    out_specs = [
        pl.BlockSpec((None, d, 2 * cw), lambda *idx: (chunk_of(*idx), 0, 0)),
        pl.BlockSpec((cw, d), lambda *idx: (chunk_of(*idx), 0)),
    ]
    out_shape = [jax.ShapeDtypeStruct((ff // cw, d, 2 * cw), _BF16),
                 jax.ShapeDtypeStruct((ff, d), _BF16)]
    return in_specs, out_specs, out_shape


def _in_proj_kernel(x_hbm, gcol_ref, w_ref, wo_ref, fgcol_ref, wg_ref, wu_ref, wd_ref,
                    a_ref, u_ref, wo_out, wgu_out, wd_out, a_scr, w_scr, x_ring, x_sem,
                    *, n_chunks):
    step = pl.program_id(0) * pl.num_programs(1) + pl.program_id(1)
    n_steps = pl.num_programs(0) * pl.num_programs(1)

    def x_copy(s):
        slot = s % X_RING_SLOTS
        return pltpu.make_async_copy(x_hbm.at[s], x_ring.at[slot], x_sem.at[slot])

    @pl.when(step == 0)
    def _():
        for s in range(X_RING_SLOTS - 1):
            x_copy(s).start()

    @pl.when(step + (X_RING_SLOTS - 1) < n_steps)
    def _():
        x_copy(step + (X_RING_SLOTS - 1)).start()

    @pl.when(step < n_chunks)
    def _():
        _prep_ffn_weights(fgcol_ref, wg_ref, wu_ref, wd_ref, wgu_out, wd_out)

    @pl.when(step == 0)
    def _():
        gain = gcol_ref[...]
        w_scr[:, :A_WIDTH] = (w_ref[:, :A_WIDTH] * gain).astype(_BF16)
        w_scr[:, A_WIDTH:] = (w_ref[:, A_WIDTH:] * (gain * 0.5)).astype(_BF16)
        wo_out[...] = wo_ref[...].astype(_BF16)

    x_copy(step).wait()
    h = _rms_scale(x_ring[step % X_RING_SLOTS]).astype(_BF16)
    pa = jnp.dot(h, w_scr[:, :A_WIDTH], preferred_element_type=_F32)
    rows = a_scr.shape[1] // DFT_RADIX
    for lt in range(A_WIDTH // LANES):
        lanes = slice(lt * LANES, (lt + 1) * LANES)
        a_scr[lt] = pa[:, lanes]
        for jr in range(DFT_RADIX):
            a_ref[jr, :, lanes] = a_scr[lt, pl.ds(jr, rows, stride=DFT_RADIX), :].astype(_BF16)
    p = jnp.dot(h, w_scr[:, A_WIDTH:], preferred_element_type=_F32)
    u_ref[...] = p[:, :B_WIDTH] * _one_plus_tanh(p[:, B_WIDTH:])


def _in_proj(x, gcol, w_in, w_out, ffn_gcol, ffn_weights):
    bsz, seq, d = x.shape
    tm = IN_PROJ_TILE
    inner = seq // DFT_RADIX
    steps_per_batch = seq // tm
    n_chunks = ffn_weights[0].shape[2] // FF_PREP_CHUNK
    n_steps = bsz * steps_per_batch
    assert n_steps >= max(n_chunks, X_RING_SLOTS - 1)
    chunk_of = lambda b, i: jnp.minimum(b * steps_per_batch + i, n_chunks - 1)
    prep_in, prep_out, prep_shape = _prep_specs(ffn_gcol, *ffn_weights, 0, chunk_of)
    return pl.pallas_call(
        functools.partial(_in_proj_kernel, n_chunks=n_chunks),
        grid=(bsz, steps_per_batch),
        in_specs=[
            pl.BlockSpec(memory_space=pl.ANY),
            _resident(gcol.shape),
            _resident(w_in.shape),
            _resident(w_out.shape),
        ] + prep_in,
        out_specs=[
            pl.BlockSpec((None, DFT_RADIX, tm // DFT_RADIX, A_WIDTH), lambda b, i: (b, 0, i, 0)),
            pl.BlockSpec((None, tm, B_WIDTH), lambda b, i: (b, i, 0)),
            _resident(w_out.shape),
        ] + prep_out,
        out_shape=[
            jax.ShapeDtypeStruct((bsz, DFT_RADIX, inner, A_WIDTH), _BF16),
            jax.ShapeDtypeStruct((bsz, seq, B_WIDTH), _F32),
            jax.ShapeDtypeStruct(w_out.shape, _BF16),
        ] + prep_shape,
        scratch_shapes=[pltpu.VMEM((A_WIDTH // LANES, tm, LANES), _F32),
                        pltpu.VMEM(w_in.shape, _BF16),
                        pltpu.VMEM((X_RING_SLOTS, tm, d), _F32),
                        pltpu.SemaphoreType.DMA((X_RING_SLOTS,))],
        compiler_params=pltpu.CompilerParams(
            dimension_semantics=("arbitrary", "arbitrary"),
            vmem_limit_bytes=V7X_VMEM_LIMIT_BYTES),
        name="in_proj",
    )(x.reshape(n_steps, tm, d), gcol, w_in, w_out, ffn_gcol, *ffn_weights)


def _cadd(a, b):
    return (a[0] + b[0], a[1] + b[1])


def _csub(a, b):
    return (a[0] - b[0], a[1] - b[1])


def _dft4(a0, a1, a2, a3):
    s0, s1 = _cadd(a0, a2), _csub(a0, a2)
    s2, s3 = _cadd(a1, a3), _csub(a1, a3)
    return (_cadd(s0, s2), (s1[0] + s3[1], s1[1] - s3[0]),
            _csub(s0, s2), (s1[0] - s3[1], s1[1] + s3[0]))


def _mul_w8(k, z):
    r, i = z
    h = math.sqrt(0.5)
    if k == 0:
        return z
    if k == 1:
        return (h * (r + i), h * (i - r))
    if k == 2:
        return (i, -r)
    return (h * (i - r), -h * (r + i))


def _bfly_unit(r0, lanes, inner, twc_ref, tws_ref, yr_scr, yi_scr):
    rt = BFLY_ROW_TILE
    z = []
    for jr in range(DFT_RADIX):
        rows = slice(jr * inner + r0, jr * inner + r0 + rt)
        yr = yr_scr[rows, lanes]
        yi = yi_scr[rows, lanes]
        if jr == 0:
            z.append((yr, yi))
        else:
            tc = twc_ref[rows, :]
            ts = tws_ref[rows, :]
            z.append((yr * tc + yi * ts, yi * tc - yr * ts))
    ev = _dft4(z[0], z[2], z[4], z[6])
    od = _dft4(z[1], z[3], z[5], z[7])
    for k in range(4):
        w = _mul_w8(k, od[k])
        lo = _cadd(ev[k], w)
        hi = _csub(ev[k], w)
        rows_lo = slice(k * inner + r0, k * inner + r0 + rt)
        rows_hi = slice((k + 4) * inner + r0, (k + 4) * inner + r0 + rt)
        yr_scr[rows_lo, lanes] = lo[0]
        yi_scr[rows_lo, lanes] = lo[1]
        yr_scr[rows_hi, lanes] = hi[0]
        yi_scr[rows_hi, lanes] = hi[1]


def _fnet_kernel(a_ref, cs_ref, twc_ref, tws_ref, cdsd_ref, map_ref, y_ref, yr_scr, yi_scr):
    seq = y_ref.shape[0]
    inner = seq // DFT_RADIX

    for c0 in range(0, A_WIDTH, FNET_COLS):
        cols = slice(c0, c0 + FNET_COLS)
        for jr in range(DFT_RADIX):
            yy = jnp.dot(cs_ref[...], a_ref[jr, :, cols], preferred_element_type=_F32)
            yr_scr[jr * inner:(jr + 1) * inner, cols] = yy[:inner]
            yi_scr[jr * inner:(jr + 1) * inner, cols] = yy[inner:]

        for r0 in range(0, inner, BFLY_ROW_TILE):
            for l0 in range(c0, c0 + FNET_COLS, LANES):
                _bfly_unit(r0, slice(l0, l0 + LANES), inner, twc_ref, tws_ref, yr_scr, yi_scr)

        for hd in range(c0 // HEAD_DIM, (c0 + FNET_COLS) // HEAD_DIM):
            lanes = slice(hd * HEAD_DIM, (hd + 1) * HEAD_DIM)
            lhs = jnp.concatenate([yr_scr[:, lanes].astype(_BF16),
                                   yi_scr[:, lanes].astype(_BF16)], axis=1)
            f = jnp.dot(lhs, cdsd_ref[...], preferred_element_type=_F32)
            ya = jnp.dot(f.astype(_BF16), map_ref[hd].astype(_BF16), preferred_element_type=_F32)
            y_ref[:, lanes] = ya.astype(_BF16)


def _fnet(a_perm, consts, fmap):
    bsz, _, inner, _ = a_perm.shape
    seq = inner * DFT_RADIX
    cs, twc, tws, cdsd = consts
    return pl.pallas_call(
        _fnet_kernel,
        grid=(bsz,),
        in_specs=[
            pl.BlockSpec((None, DFT_RADIX, inner, A_WIDTH), lambda b: (b, 0, 0, 0)),
            _resident(cs.shape), _resident(twc.shape), _resident(tws.shape),
            _resident(cdsd.shape), _resident(fmap.shape),
        ],
        out_specs=pl.BlockSpec((None, seq, A_WIDTH), lambda b: (b, 0, 0)),
        out_shape=jax.ShapeDtypeStruct((bsz, seq, A_WIDTH), _BF16),
        scratch_shapes=[
            pltpu.VMEM((seq, A_WIDTH), _F32),
            pltpu.VMEM((seq, A_WIDTH), _F32),
        ],
        compiler_params=pltpu.CompilerParams(
            dimension_semantics=("arbitrary",),
            vmem_limit_bytes=V7X_VMEM_LIMIT_BYTES),
        name="fnet",
    )(a_perm, cs, twc, tws, cdsd, fmap)


def _dft_constants(seq):
    inner = seq // DFT_RADIX
    k = np.arange(inner, dtype=np.float64)
    ang = 2.0 * np.pi * np.outer(k, k) / inner
    cs = np.concatenate([np.cos(ang), -np.sin(ang)], axis=0)
    jr = np.arange(DFT_RADIX, dtype=np.float64)[:, None]
    tw = 2.0 * np.pi * (jr * k[None, :]) / seq
    twc = np.repeat(np.cos(tw).reshape(seq, 1), LANES, axis=1)
    tws = np.repeat(np.sin(tw).reshape(seq, 1), LANES, axis=1)
    d = np.arange(HEAD_DIM, dtype=np.float64)
    angd = 2.0 * np.pi * np.outer(d, d) / HEAD_DIM
    scale = 1.0 / math.sqrt(seq * HEAD_DIM)
    cdsd = np.concatenate([np.cos(angd), np.sin(angd)], axis=0) * scale
    return (jnp.asarray(cs, _F32).astype(_BF16), jnp.asarray(twc, _F32), jnp.asarray(tws, _F32),
            jnp.asarray(cdsd, _F32).astype(_BF16))


def _conv_fill_window(tile, tiles_per_seq, main_ref, prev_ref, next_ref, win_scr):
    tm = main_ref.shape[0]
    halo = CONV_HALO
    pos = jnp.zeros((halo, 1), jnp.int32) + tile % tiles_per_seq
    win_scr[0:halo, :] = jnp.where(pos == 0, 0.0, prev_ref[...])
    win_scr[halo:halo + tm, :] = main_ref[...]
    win_scr[halo + tm:, :] = jnp.where(pos == tiles_per_seq - 1, 0.0, next_ref[...])


def _conv_unit(rc, lt, win_scr, cw_ref, cb_ref, lg_ref, lb_ref, out_ref, ct=CONV_ROW_TILE):
    halo = CONV_HALO
    r0 = rc * ct
    lanes = slice(lt * LANES, (lt + 1) * LANES)
    first = halo - CONV_PAD
    span = ct + 2 * halo
    win = win_scr[r0:r0 + span, lanes]
    acc = None
    for s in range(SUBLANES):
        rot = win if s == 0 else pltpu.roll(win, span - s, axis=0)
        for q in range((first + CONV_WIDTH - 1) // SUBLANES + 1):
            k = SUBLANES * q + s - first
            if 0 <= k < CONV_WIDTH:
                term = rot[SUBLANES * q:SUBLANES * q + ct] * cw_ref[k:k + 1, lanes]
                acc = term if acc is None else acc + term
    cv = acc + cb_ref[:, lanes]
    mu = jnp.mean(cv, axis=-1, keepdims=True)
    dv = cv - mu
    var = jnp.mean(dv * dv, axis=-1, keepdims=True)
    half_yn = (dv * jax.lax.rsqrt(var + LN_EPS)) * (lg_ref[:, lanes] * 0.5) + lb_ref[:, lanes] * 0.5
    y = half_yn * _one_plus_tanh(half_yn)
    out_ref[r0:r0 + ct, lanes] = y.astype(_BF16)


def _conv_units(tm, ct=CONV_ROW_TILE):
    return [(rc, lt) for rc in range(tm // ct) for lt in range(B_WIDTH // LANES)]


def _conv_scratch(tm):
    return [pltpu.VMEM((tm + 2 * CONV_HALO, B_WIDTH), _F32)]


def _conv_first_kernel(main_ref, next_ref, cw_ref, cb_ref, lg_ref, lb_ref, out_ref, win_scr):
    _conv_fill_window(0, 2, main_ref, next_ref, next_ref, win_scr)
    for rc, lt in _conv_units(main_ref.shape[0], CONV_FIRST_ROW_TILE):
        _conv_unit(rc, lt, win_scr, cw_ref, cb_ref, lg_ref, lb_ref, out_ref, CONV_FIRST_ROW_TILE)


def _conv_first(u2d, conv_w, conv_b, ln_g, ln_b):
    tm = TOKEN_TILE
    return pl.pallas_call(
        _conv_first_kernel,
        grid=(1,),
        in_specs=[
            pl.BlockSpec((tm, B_WIDTH), lambda i: (0, 0)),
            pl.BlockSpec((CONV_HALO, B_WIDTH), lambda i: (tm // CONV_HALO, 0)),
            _resident(conv_w.shape), _resident(conv_b.shape), _resident(ln_g.shape),
            _resident(ln_b.shape),
        ],
        out_specs=pl.BlockSpec((tm, B_WIDTH), lambda i: (0, 0)),
        out_shape=jax.ShapeDtypeStruct((tm, B_WIDTH), _BF16),
        scratch_shapes=_conv_scratch(tm),
        compiler_params=pltpu.CompilerParams(
            dimension_semantics=("arbitrary",),
            vmem_limit_bytes=V7X_VMEM_LIMIT_BYTES),
        name="conv_first",
    )(u2d, u2d, conv_w, conv_b, ln_g, ln_b)


def _pool_fill_units(tile, tiles_per_seq, x_ref, prev_ref, next_ref, hp_scr):
    tm = x_ref.shape[0]
    seq = tm * tiles_per_seq
    rt = POOL_ROW_TILE

    def halos():
        start = (tile % tiles_per_seq) * tm
        halo_iota = jax.lax.broadcasted_iota(jnp.int32, (POOL_HALO, 1), 0)
        prev_ok = (start - POOL_HALO + halo_iota) >= 0
        next_ok = (start + tm + halo_iota) < seq
        hp_scr[0:POOL_HALO, :] = jnp.where(prev_ok, _rms_scale(prev_ref[...]), 0.0)
        hp_scr[POOL_HALO + tm:, :] = jnp.where(next_ok, _rms_scale(next_ref[...]), 0.0)

    def rows(r0):
        hp_scr[POOL_HALO + r0:POOL_HALO + r0 + rt, :] = _rms_scale(x_ref[r0:r0 + rt, :])

    return [halos] + [functools.partial(rows, r0) for r0 in range(0, tm, rt)]


def _pool_units(tm):
    return [(rc, gi) for rc in range(tm // POOL_ROW_TILE) for gi in range(len(POOL_WINDOWS))]


def _pool_unit(rc, gi, tile, tiles_per_seq, hp_scr, out_ref):
    rt = POOL_ROW_TILE
    w = POOL_WINDOWS[gi]
    half = w // 2
    tm = hp_scr.shape[0] - 2 * POOL_HALO
    gd = hp_scr.shape[1] // len(POOL_WINDOWS)
    seq = tm * tiles_per_seq
    lanes = slice(gi * gd, (gi + 1) * gd)
    r0 = rc * rt
    n = rt + 2 * POOL_HALO
    fwd = hp_scr[r0:r0 + n, lanes]
    span = 1
    while span < half:
        fwd = fwd + pltpu.roll(fwd, n - span, axis=0)
        span *= 2
    centred = fwd + pltpu.roll(fwd, half, axis=0)
    win = centred[POOL_HALO:POOL_HALO + rt]

    def edge_mean(e0):
        pos = ((tile % tiles_per_seq) * tm + r0 + e0
               + jax.lax.broadcasted_iota(jnp.int32, (POOL_HALO, LANES), 0))
        cnt = jnp.minimum(pos + (w - half), seq) - jnp.maximum(pos - half, 0)
        inv = 1.0 / cnt.astype(_F32)
        return win[e0:e0 + POOL_HALO] * jnp.concatenate([inv] * (gd // LANES), axis=1)

    lo = POOL_HALO if rc == 0 else 0
    hi = rt - POOL_HALO if rc == tm // rt - 1 else rt
    parts = [win[lo:hi] * (1.0 / w)]
    if lo:
        parts.insert(0, edge_mean(0))
    if hi < rt:
        parts.append(edge_mean(hi))
    mean = jnp.concatenate(parts, axis=0)
    pgv = mean - hp_scr[POOL_HALO + r0:POOL_HALO + r0 + rt, lanes]
    out_ref[r0:r0 + rt, lanes] = pgv.astype(_BF16)
    return pgv[rt - SUBLANES:, :LANES]


def _derived_zero(dep):
    bits = pltpu.bitcast(dep, jnp.uint32)
    bits = jax.lax.shift_right_logical(jax.lax.shift_right_logical(bits, jnp.uint32(16)), jnp.uint32(16))
    zero = pltpu.bitcast(bits, _F32)
    return jnp.concatenate([zero, zero], axis=0)


def _pool_first_kernel(x_ref, next_ref, out_ref, hp_scr):
    for unit in _pool_fill_units(0, 2, x_ref, next_ref, next_ref, hp_scr):
        unit()
    for rc, gi in _pool_units(x_ref.shape[0]):
        _pool_unit(rc, gi, 0, 2, hp_scr, out_ref)


def _pool_first(x2d):
    tm = TOKEN_TILE
    d = x2d.shape[1]
    return pl.pallas_call(
        _pool_first_kernel,
        grid=(1,),
        in_specs=[
            pl.BlockSpec((tm, d), lambda i: (0, 0)),
            pl.BlockSpec((POOL_HALO, d), lambda i: (tm // POOL_HALO, 0)),
        ],
        out_specs=pl.BlockSpec((tm, d), lambda i: (0, 0)),
        out_shape=jax.ShapeDtypeStruct((tm, d), _BF16),
        scratch_shapes=[pltpu.VMEM((tm + 2 * POOL_HALO, d), _F32)],
        compiler_params=pltpu.CompilerParams(
            dimension_semantics=("arbitrary",),
            vmem_limit_bytes=V7X_VMEM_LIMIT_BYTES),
        name="pool_first",
    )(x2d, x2d)


def _ffn_kernel(*refs, has_mixer, has_pool, has_final, has_prep, tiles_per_seq, n_chunks):
    it = iter(refs)
    x_ref = next(it)
    if has_mixer:
        ya_ref = next(it)
        yb0_ref = next(it)
        u_refs = (next(it), next(it), next(it))
        conv_refs = (next(it), next(it), next(it), next(it))
        wo_ref = next(it)
    if has_pool:
        pg0_ref = next(it)
        xn_refs = (next(it), next(it), next(it))
        gcol_ref, pm_ref, ps_ref = next(it), next(it), next(it)
    wgu_ref = next(it)
    wd_ref = next(it)
    if has_prep:
        prep_in = (next(it), next(it), next(it), next(it))
    if has_final:
        fg_ref = next(it)
    o_ref = next(it)
    if has_prep:
        prep_out = (next(it), next(it))
    h_scr = next(it)
    act_scr = next(it)
    if has_mixer:
        yb_scr = next(it)
        win_scr = next(it)
    if has_pool:
        pg_scr = next(it)
        hp_scr = next(it)
        pm_scr = next(it)
        xv_scr = next(it)
        hn_scr = next(it)

    i = pl.program_id(0)
    n = pl.num_programs(0)
    slot = i % 2
    nxt = jnp.minimum(i + 1, n - 1)
    if has_prep:
        @pl.when(i < n_chunks)
        def _():
            _prep_ffn_weights(*prep_in, *prep_out)

    tm, d = x_ref.shape
    xv = x_ref[...]
    units = []
    if has_mixer:
        @pl.when(i == 0)
        def _():
            yb_scr[0] = yb0_ref[...]

        yb = yb_scr[slot]
        _conv_fill_window(nxt, tiles_per_seq, *u_refs, win_scr)
        units = [functools.partial(_conv_unit, rc, lt, win_scr, *conv_refs, yb_scr.at[1 - slot])
                 for rc, lt in _conv_units(tm)]
    if has_pool:
        gd = d // len(POOL_WINDOWS)

        @pl.when(i == 0)
        def _():
            pg_scr[0] = pg0_ref[...]
            for gi in range(len(POOL_WINDOWS)):
                pm_scr[gi] = (pm_ref[gi] * gcol_ref[gi * gd:(gi + 1) * gd, :]).astype(_BF16)

        def front(x_val, pg_val):
            ys = [jnp.dot(pg_val[:, gi * gd:(gi + 1) * gd], pm_scr[gi], preferred_element_type=_F32)
                  for gi in range(len(POOL_WINDOWS))]
            xf = x_val + jnp.concatenate(ys, axis=1) * ps_ref[...]
            return xf, _rms_scale(xf).astype(_BF16)

        @pl.when(i == 0)
        def _():
            xv_scr[0], hn_scr[0] = front(x_ref[...], pg0_ref[...])

        units = _pool_fill_units(nxt, tiles_per_seq, *xn_refs, hp_scr) + [
            functools.partial(_pool_unit, rc, gi, nxt, tiles_per_seq, hp_scr, pg_scr.at[1 - slot])
            for rc, gi in _pool_units(tm)]
    pending = iter(units)
    units_per_dot = len(units) // n_chunks

    def side_work(n_units):
        dep = None
        for _ in range(n_units):
            out = next(pending)()
            dep = dep if out is None else out
        return dep

    side_work(len(units) - n_chunks * units_per_dot)
    if has_mixer:
        yab = jnp.concatenate([ya_ref[...], yb], axis=1)
        xv = xv + jnp.dot(yab, wo_ref[...], preferred_element_type=_F32)
    if has_pool:
        h_scr = hn_scr.at[slot]
        xv = xv_scr[slot]
        hb = h_scr[...]
        h_tile = hb[:2 * SUBLANES, :LANES].astype(_F32)
    else:
        hb = _rms_scale(xv).astype(_BF16)

    gu_width = wgu_ref.shape[2]
    for c in range(n_chunks):
        dep = side_work(units_per_dot)
        if dep is not None:
            h_scr[:2 * SUBLANES, :LANES] = (h_tile + _derived_zero(dep)).astype(_BF16)
            hb = h_scr[...]
        gu = jnp.dot(hb, wgu_ref[c], preferred_element_type=_F32)
        for b0 in range(0, gu_width, 2 * LANES):
            half_gate = gu[:, b0:b0 + LANES]
            up = gu[:, b0 + LANES:b0 + 2 * LANES]
            f0 = (c * gu_width + b0) // 2
            act = (half_gate * up) * _one_plus_tanh(half_gate)
            act_scr[:, f0:f0 + LANES] = act.astype(_BF16)
    acc = xv + jnp.dot(act_scr[...], wd_ref[...], preferred_element_type=_F32)
    if has_final:
        acc = _rmsnorm(acc, fg_ref[...])
    o_ref[...] = acc
    if has_pool:
        xv_scr[1 - slot], hn_scr[1 - slot] = front(xn_refs[0][...], pg_scr[1 - slot])


def _ffn(x2d, wgu, wd, mixer=None, pool=None, final_g=None, prep=None, seq=None):
    tokens, d = x2d.shape
    n_chunks, _, _ = wgu.shape
    ff = wd.shape[0]
    tm = TOKEN_TILE
    tiles_per_seq = seq // tm
    n_tiles = tokens // tm
    has_mixer = mixer is not None
    has_pool = pool is not None
    has_final = final_g is not None
    has_prep = prep is not None
    assert n_tiles >= n_chunks
    args = [x2d]
    specs = [pl.BlockSpec((tm, d), lambda i: (i, 0))]
    out_specs = [pl.BlockSpec((tm, d), lambda i: (i, 0))]
    out_shape = [jax.ShapeDtypeStruct((tokens, d), _F32)]
    scratch = [pltpu.VMEM((tm, d), _BF16), pltpu.VMEM((tm, ff), _BF16)]
    if has_mixer:
        ya2d, yb0, u2d, conv_w, conv_b, ln_g, ln_b, w_out = mixer
        hb = tm // CONV_HALO
        n_halo = tokens // CONV_HALO
        nxt = lambda i: jnp.minimum(i + 1, n_tiles - 1)
        args += [ya2d, yb0, u2d, u2d, u2d, conv_w, conv_b, ln_g, ln_b, w_out]
        specs += [
            pl.BlockSpec((tm, A_WIDTH), lambda i: (i, 0)),
            _resident(yb0.shape),
            pl.BlockSpec((tm, B_WIDTH), lambda i: (nxt(i), 0)),
            pl.BlockSpec((CONV_HALO, B_WIDTH), lambda i: (jnp.maximum(nxt(i) * hb - 1, 0), 0)),
            pl.BlockSpec((CONV_HALO, B_WIDTH),
                         lambda i: (jnp.minimum((nxt(i) + 1) * hb, n_halo - 1), 0)),
            _resident(conv_w.shape), _resident(conv_b.shape), _resident(ln_g.shape),
            _resident(ln_b.shape), _resident(w_out.shape),
        ]
        scratch += [pltpu.VMEM((2, tm, B_WIDTH), _BF16)] + _conv_scratch(tm)
    if has_pool:
        pg0, g_mix, pool_map, pool_scale = pool
        hb = tm // POOL_HALO
        n_halo = tokens // POOL_HALO
        nxt = lambda i: jnp.minimum(i + 1, n_tiles - 1)
        args += [pg0, x2d, x2d, x2d, g_mix, pool_map, pool_scale]
        specs += [
            _resident(pg0.shape),
            pl.BlockSpec((tm, d), lambda i: (nxt(i), 0)),
            pl.BlockSpec((POOL_HALO, d), lambda i: (jnp.maximum(nxt(i) * hb - 1, 0), 0)),
            pl.BlockSpec((POOL_HALO, d), lambda i: (jnp.minimum((nxt(i) + 1) * hb, n_halo - 1), 0)),
            _resident(g_mix.shape), _resident(pool_map.shape), _resident(pool_scale.shape),
        ]
        scratch += [pltpu.VMEM((2, tm, d), _BF16), pltpu.VMEM((tm + 2 * POOL_HALO, d), _F32),
                    pltpu.VMEM(pool_map.shape, _BF16), pltpu.VMEM((2, tm, d), _F32),
                    pltpu.VMEM((2, tm, d), _BF16)]
    args += [wgu, wd]
    specs += [_resident(wgu.shape), _resident(wd.shape)]
    if has_prep:
        prep_in, prep_out, prep_shape = _prep_specs(
            *prep, lambda i: jnp.minimum(i, n_chunks - 1))
        args += list(prep[:4])
        specs += prep_in
        out_specs += prep_out
        out_shape += prep_shape
    if has_final:
        args.append(final_g)
        specs.append(_resident(final_g.shape))
    outs = pl.pallas_call(
        functools.partial(_ffn_kernel, has_mixer=has_mixer, has_pool=has_pool, has_final=has_final,
                          has_prep=has_prep, tiles_per_seq=tiles_per_seq, n_chunks=n_chunks),
        grid=(n_tiles,),
        in_specs=specs,
        out_specs=out_specs,
        out_shape=out_shape,
        scratch_shapes=scratch,
        compiler_params=pltpu.CompilerParams(
            dimension_semantics=("arbitrary",),
            vmem_limit_bytes=V7X_VMEM_LIMIT_BYTES),
        name="ffn_mixer" if has_mixer else "ffn_final",
    )(*args)
    return outs if has_prep else outs[0]


def kernel(x, norm_mix_g, norm_ffn_g, w_in_ab, fnet_map, conv_w, conv_b, conv_ln_g, conv_ln_b,
           w_out_ab, pool_map, pool_scale, ffn_w_gate, ffn_w_up, ffn_w_down, final_g):
    bsz, seq, d = x.shape
    tokens = bsz * seq
    row = lambda v: v.reshape(1, -1)
    col = lambda v: v.reshape(-1, 1)

    ffn_weights = (ffn_w_gate, ffn_w_up, ffn_w_down)
    a_perm, u, w_out, wgu0, wd0 = _in_proj(x, col(norm_mix_g[0]), w_in_ab[0], w_out_ab[0],
                                           col(norm_ffn_g[0]), ffn_weights)
    ya = _fnet(a_perm, _dft_constants(seq), fnet_map[0])
    u2d = u.reshape(tokens, B_WIDTH)
    conv_p = (conv_w[0], row(conv_b[0]), row(conv_ln_g[0]), row(conv_ln_b[0]))
    mixer = (ya.reshape(tokens, A_WIDTH), _conv_first(u2d, *conv_p), u2d, *conv_p, w_out)
    x2, wgu1, wd1 = _ffn(x.reshape(tokens, d), wgu0, wd0, mixer=mixer,
                         prep=(col(norm_ffn_g[1]), *ffn_weights, 1), seq=seq)

    pool = (_pool_first(x2), col(norm_mix_g[1]), pool_map[0], row(pool_scale[0]))
    out = _ffn(x2, wgu1, wd1, pool=pool, final_g=row(final_g), seq=seq)
    return out.reshape(bsz, seq, d)
```

```python
import functools
import math

import jax
import jax.numpy as jnp
import numpy as np
from jax.experimental import pallas as pl
from jax.experimental.pallas import tpu as pltpu

RMS_EPS = 1e-6
LN_EPS = 1e-5

A_HEADS = 4
HEAD_DIM = 128
A_WIDTH = A_HEADS * HEAD_DIM
B_WIDTH = 512
CONV_WIDTH = 31
CONV_PAD = CONV_WIDTH // 2
POOL_WINDOWS = (2, 4, 8, 16)
POOL_HALO = 8

LANES = 128
SUBLANES = 8

DFT_RADIX = 8

V7X_VMEM_LIMIT_BYTES = 56 * 1024 * 1024

TOKEN_TILE = 512
IN_PROJ_TILE = 1024
X_RING_SLOTS = 3
FF_PREP_CHUNK = 256
BFLY_ROW_TILE = 16
FNET_COLS = 256
CONV_HALO = 16
CONV_ROW_TILE = 16
CONV_FIRST_ROW_TILE = 64
POOL_ROW_TILE = 32

_F32 = jnp.float32
_BF16 = jnp.bfloat16


def _resident(shape):
    nd = len(shape)
    return pl.BlockSpec(shape, lambda *_: (0,) * nd, pipeline_mode=pl.Buffered(1))


def _rms_scale(xv):
    return xv * jax.lax.rsqrt(jnp.mean(xv * xv, axis=-1, keepdims=True) + RMS_EPS)


def _rmsnorm(xv, g):
    return _rms_scale(xv) * g


def _one_plus_tanh(half_v):
    return 1.0 + jnp.tanh(half_v)


def _gain_column(grow_ref):
    return jnp.broadcast_to(grow_ref[...], (LANES, grow_ref.shape[1])).T[:, :1]


def _prep_ffn_weights(gcol_ref, wg_ref, wu_ref, wd_ref, wgu_dst, wd_dst):
    cw = wg_ref.shape[1]
    gain = gcol_ref[...]
    half_gain = gain * 0.5
    for b in range(cw // LANES):
        src = slice(b * LANES, (b + 1) * LANES)
        wgu_dst[:, 2 * b * LANES:(2 * b + 1) * LANES] = (wg_ref[:, src] * half_gain).astype(_BF16)
        wgu_dst[:, (2 * b + 1) * LANES:(2 * b + 2) * LANES] = (wu_ref[:, src] * gain).astype(_BF16)
    wd_dst[...] = wd_ref[...].astype(_BF16)


def _prep_specs(gcol, wg_all, wu_all, wd_all, layer, chunk_of):
    _, d, ff = wg_all.shape
    cw = FF_PREP_CHUNK
    in_specs = [
        _resident(gcol.shape),
        pl.BlockSpec((None, d, cw), lambda *idx: (layer, 0, chunk_of(*idx))),
        pl.BlockSpec((None, d, cw), lambda *idx: (layer, 0, chunk_of(*idx))),
        pl.BlockSpec((None, cw, d), lambda *idx: (layer, chunk_of(*idx), 0)),
    ]
    out_specs = [
        pl.BlockSpec((None, d, 2 * cw), lambda *idx: (chunk_of(*idx), 0, 0)),
        pl.BlockSpec((cw, d), lambda *idx: (chunk_of(*idx), 0)),
    ]
    out_shape = [jax.ShapeDtypeStruct((ff // cw, d, 2 * cw), _BF16),
                 jax.ShapeDtypeStruct((ff, d), _BF16)]
    return in_specs, out_specs, out_shape


def _in_proj_kernel(x_hbm, gcol_ref, w_ref, wo_ref, fgcol_ref, wg_ref, wu_ref, wd_ref,
                    a_ref, u_ref, wo_out, wgu_out, wd_out, a_scr, w_scr, x_ring, x_sem,
                    fgcol_scr, *, n_chunks):
    step = pl.program_id(0) * pl.num_programs(1) + pl.program_id(1)
    n_steps = pl.num_programs(0) * pl.num_programs(1)

    def x_copy(s):
        slot = s % X_RING_SLOTS
        return pltpu.make_async_copy(x_hbm.at[s], x_ring.at[slot], x_sem.at[slot])

    @pl.when(step == 0)
    def _():
        for s in range(X_RING_SLOTS - 1):
            x_copy(s).start()

    @pl.when(step + (X_RING_SLOTS - 1) < n_steps)
    def _():
        x_copy(step + (X_RING_SLOTS - 1)).start()

    @pl.when(step == 0)
    def _():
        fgcol_scr[...] = _gain_column(fgcol_ref)

    @pl.when(step < n_chunks)
    def _():
        _prep_ffn_weights(fgcol_scr, wg_ref, wu_ref, wd_ref, wgu_out, wd_out)

    @pl.when(step == 0)
    def _():
        gain = _gain_column(gcol_ref)
        w_scr[:, :A_WIDTH] = (w_ref[:, :A_WIDTH] * gain).astype(_BF16)
        w_scr[:, A_WIDTH:] = (w_ref[:, A_WIDTH:] * (gain * 0.5)).astype(_BF16)
        wo_out[...] = wo_ref[...].astype(_BF16)

    x_copy(step).wait()
    h = _rms_scale(x_ring[step % X_RING_SLOTS]).astype(_BF16)
    pa = jnp.dot(h, w_scr[:, :A_WIDTH], preferred_element_type=_F32)
    rows = a_scr.shape[1] // DFT_RADIX
    for lt in range(A_WIDTH // LANES):
        lanes = slice(lt * LANES, (lt + 1) * LANES)
        a_scr[lt] = pa[:, lanes]
        for jr in range(DFT_RADIX):
            a_ref[jr, :, lanes] = a_scr[lt, pl.ds(jr, rows, stride=DFT_RADIX), :].astype(_BF16)
    p = jnp.dot(h, w_scr[:, A_WIDTH:], preferred_element_type=_F32)
    u_ref[...] = p[:, :B_WIDTH] * _one_plus_tanh(p[:, B_WIDTH:])


def _in_proj(x, gcol, w_in, w_out, ffn_gcol, ffn_weights):
    bsz, seq, d = x.shape
    tm = IN_PROJ_TILE
    inner = seq // DFT_RADIX
    steps_per_batch = seq // tm
    n_chunks = ffn_weights[0].shape[2] // FF_PREP_CHUNK
    n_steps = bsz * steps_per_batch
    assert n_steps >= max(n_chunks, X_RING_SLOTS - 1)
    chunk_of = lambda b, i: jnp.minimum(b * steps_per_batch + i, n_chunks - 1)
    prep_in, prep_out, prep_shape = _prep_specs(ffn_gcol, *ffn_weights, 0, chunk_of)
    return pl.pallas_call(
        functools.partial(_in_proj_kernel, n_chunks=n_chunks),
        grid=(bsz, steps_per_batch),
        in_specs=[
            pl.BlockSpec(memory_space=pl.ANY),
            _resident(gcol.shape),
            _resident(w_in.shape),
            _resident(w_out.shape),
        ] + prep_in,
        out_specs=[
            pl.BlockSpec((None, DFT_RADIX, tm // DFT_RADIX, A_WIDTH), lambda b, i: (b, 0, i, 0)),
            pl.BlockSpec((None, tm, B_WIDTH), lambda b, i: (b, i, 0)),
            _resident(w_out.shape),
        ] + prep_out,
        out_shape=[
            jax.ShapeDtypeStruct((bsz, DFT_RADIX, inner, A_WIDTH), _BF16),
            jax.ShapeDtypeStruct((bsz, seq, B_WIDTH), _F32),
            jax.ShapeDtypeStruct(w_out.shape, _BF16),
        ] + prep_shape,
        scratch_shapes=[pltpu.VMEM((A_WIDTH // LANES, tm, LANES), _F32),
                        pltpu.VMEM(w_in.shape, _BF16),
                        pltpu.VMEM((X_RING_SLOTS, tm, d), _F32),
                        pltpu.SemaphoreType.DMA((X_RING_SLOTS,)),
                        pltpu.VMEM((d, 1), _F32)],
        compiler_params=pltpu.CompilerParams(
            dimension_semantics=("arbitrary", "arbitrary"),
            vmem_limit_bytes=V7X_VMEM_LIMIT_BYTES),
        name="in_proj",
    )(x.reshape(n_steps, tm, d), gcol, w_in, w_out, ffn_gcol, *ffn_weights)


def _cadd(a, b):
    return (a[0] + b[0], a[1] + b[1])


def _csub(a, b):
    return (a[0] - b[0], a[1] - b[1])


def _dft4(a0, a1, a2, a3):
    s0, s1 = _cadd(a0, a2), _csub(a0, a2)
    s2, s3 = _cadd(a1, a3), _csub(a1, a3)
    return (_cadd(s0, s2), (s1[0] + s3[1], s1[1] - s3[0]),
            _csub(s0, s2), (s1[0] - s3[1], s1[1] + s3[0]))


def _mul_w8(k, z):
    r, i = z
    h = math.sqrt(0.5)
    if k == 0:
        return z
    if k == 1:
        return (h * (r + i), h * (i - r))
    if k == 2:
        return (i, -r)
    return (h * (i - r), -h * (r + i))


def _bfly_unit(r0, lanes, inner, twc_ref, tws_ref, yr_scr, yi_scr):
    rt = BFLY_ROW_TILE
    z = []
    for jr in range(DFT_RADIX):
        rows = slice(jr * inner + r0, jr * inner + r0 + rt)
        yr = yr_scr[rows, lanes]
        yi = yi_scr[rows, lanes]
        if jr == 0:
            z.append((yr, yi))
        else:
            tc = twc_ref[rows, :]
            ts = tws_ref[rows, :]
            z.append((yr * tc + yi * ts, yi * tc - yr * ts))
    ev = _dft4(z[0], z[2], z[4], z[6])
    od = _dft4(z[1], z[3], z[5], z[7])
    for k in range(4):
        w = _mul_w8(k, od[k])
        lo = _cadd(ev[k], w)
        hi = _csub(ev[k], w)
        rows_lo = slice(k * inner + r0, k * inner + r0 + rt)
        rows_hi = slice((k + 4) * inner + r0, (k + 4) * inner + r0 + rt)
        yr_scr[rows_lo, lanes] = lo[0]
        yi_scr[rows_lo, lanes] = lo[1]
        yr_scr[rows_hi, lanes] = hi[0]
        yi_scr[rows_hi, lanes] = hi[1]


def _fnet_kernel(a_ref, cs_ref, twc_ref, tws_ref, cdsd_ref, map_ref, y_ref, yr_scr, yi_scr):
    seq = y_ref.shape[0]
    inner = seq // DFT_RADIX

    for c0 in range(0, A_WIDTH, FNET_COLS):
        cols = slice(c0, c0 + FNET_COLS)
        for jr in range(DFT_RADIX):
            yy = jnp.dot(cs_ref[...], a_ref[jr, :, cols], preferred_element_type=_F32)
            yr_scr[jr * inner:(jr + 1) * inner, cols] = yy[:inner]
            yi_scr[jr * inner:(jr + 1) * inner, cols] = yy[inner:]

        for r0 in range(0, inner, BFLY_ROW_TILE):
            for l0 in range(c0, c0 + FNET_COLS, LANES):
                _bfly_unit(r0, slice(l0, l0 + LANES), inner, twc_ref, tws_ref, yr_scr, yi_scr)

        for hd in range(c0 // HEAD_DIM, (c0 + FNET_COLS) // HEAD_DIM):
            lanes = slice(hd * HEAD_DIM, (hd + 1) * HEAD_DIM)
            lhs = jnp.concatenate([yr_scr[:, lanes].astype(_BF16),
                                   yi_scr[:, lanes].astype(_BF16)], axis=1)
            f = jnp.dot(lhs, cdsd_ref[...], preferred_element_type=_F32)
            ya = jnp.dot(f.astype(_BF16), map_ref[hd].astype(_BF16), preferred_element_type=_F32)
            y_ref[:, lanes] = ya.astype(_BF16)


def _fnet(a_perm, consts, fmap):
    bsz, _, inner, _ = a_perm.shape
    seq = inner * DFT_RADIX
    cs, twc, tws, cdsd = consts
    return pl.pallas_call(
        _fnet_kernel,
        grid=(bsz,),
        in_specs=[
            pl.BlockSpec((None, DFT_RADIX, inner, A_WIDTH), lambda b: (b, 0, 0, 0)),
            _resident(cs.shape), _resident(twc.shape), _resident(tws.shape),
            _resident(cdsd.shape), _resident(fmap.shape),
        ],
        out_specs=pl.BlockSpec((None, seq, A_WIDTH), lambda b: (b, 0, 0)),
        out_shape=jax.ShapeDtypeStruct((bsz, seq, A_WIDTH), _BF16),
        scratch_shapes=[
            pltpu.VMEM((seq, A_WIDTH), _F32),
            pltpu.VMEM((seq, A_WIDTH), _F32),
        ],
        compiler_params=pltpu.CompilerParams(
            dimension_semantics=("arbitrary",),
            vmem_limit_bytes=V7X_VMEM_LIMIT_BYTES),
        name="fnet",
    )(a_perm, cs, twc, tws, cdsd, fmap)


def _dft_constants(seq):
    inner = seq // DFT_RADIX
    k = np.arange(inner, dtype=np.float64)
    ang = 2.0 * np.pi * np.outer(k, k) / inner
    cs = np.concatenate([np.cos(ang), -np.sin(ang)], axis=0)
    jr = np.arange(DFT_RADIX, dtype=np.float64)[:, None]
    tw = 2.0 * np.pi * (jr * k[None, :]) / seq
    twc = np.repeat(np.cos(tw).reshape(seq, 1), LANES, axis=1)
    tws = np.repeat(np.sin(tw).reshape(seq, 1), LANES, axis=1)
    d = np.arange(HEAD_DIM, dtype=np.float64)
    angd = 2.0 * np.pi * np.outer(d, d) / HEAD_DIM
    scale = 1.0 / math.sqrt(seq * HEAD_DIM)
    cdsd = np.concatenate([np.cos(angd), np.sin(angd)], axis=0) * scale
    return (jnp.asarray(cs, _F32).astype(_BF16), jnp.asarray(twc, _F32), jnp.asarray(tws, _F32),
            jnp.asarray(cdsd, _F32).astype(_BF16))


def _conv_fill_window(tile, tiles_per_seq, main_ref, prev_ref, next_ref, win_scr):
    tm = main_ref.shape[0]
    halo = CONV_HALO
    pos = jnp.zeros((halo, 1), jnp.int32) + tile % tiles_per_seq
    win_scr[0:halo, :] = jnp.where(pos == 0, 0.0, prev_ref[...])
    win_scr[halo:halo + tm, :] = main_ref[...]
    win_scr[halo + tm:, :] = jnp.where(pos == tiles_per_seq - 1, 0.0, next_ref[...])


def _conv_unit(rc, lt, win_scr, cw_ref, cb_ref, lg_ref, lb_ref, out_ref, ct=CONV_ROW_TILE):
    halo = CONV_HALO
    r0 = rc * ct
    lanes = slice(lt * LANES, (lt + 1) * LANES)
    first = halo - CONV_PAD
    span = ct + 2 * halo
    win = win_scr[r0:r0 + span, lanes]
    acc = None
    for s in range(SUBLANES):
        rot = win if s == 0 else pltpu.roll(win, span - s, axis=0)
        for q in range((first + CONV_WIDTH - 1) // SUBLANES + 1):
            k = SUBLANES * q + s - first
            if 0 <= k < CONV_WIDTH:
                term = rot[SUBLANES * q:SUBLANES * q + ct] * cw_ref[k:k + 1, lanes]
                acc = term if acc is None else acc + term
    cv = acc + cb_ref[:, lanes]
    mu = jnp.mean(cv, axis=-1, keepdims=True)
    dv = cv - mu
    var = jnp.mean(dv * dv, axis=-1, keepdims=True)
    half_yn = (dv * jax.lax.rsqrt(var + LN_EPS)) * (lg_ref[:, lanes] * 0.5) + lb_ref[:, lanes] * 0.5
    y = half_yn * _one_plus_tanh(half_yn)
    out_ref[r0:r0 + ct, lanes] = y.astype(_BF16)


def _conv_units(tm, ct=CONV_ROW_TILE):
    return [(rc, lt) for rc in range(tm // ct) for lt in range(B_WIDTH // LANES)]


def _conv_scratch(tm):
    return [pltpu.VMEM((tm + 2 * CONV_HALO, B_WIDTH), _F32)]


def _conv_first_kernel(main_ref, next_ref, cw_ref, cb_ref, lg_ref, lb_ref, out_ref, win_scr):
    _conv_fill_window(0, 2, main_ref, next_ref, next_ref, win_scr)
    for rc, lt in _conv_units(main_ref.shape[0], CONV_FIRST_ROW_TILE):
        _conv_unit(rc, lt, win_scr, cw_ref, cb_ref, lg_ref, lb_ref, out_ref, CONV_FIRST_ROW_TILE)


def _conv_first(u2d, conv_w, conv_b, ln_g, ln_b):
    tm = TOKEN_TILE
    return pl.pallas_call(
        _conv_first_kernel,
        grid=(1,),
        in_specs=[
            pl.BlockSpec((tm, B_WIDTH), lambda i: (0, 0)),
            pl.BlockSpec((CONV_HALO, B_WIDTH), lambda i: (tm // CONV_HALO, 0)),
            _resident(conv_w.shape), _resident(conv_b.shape), _resident(ln_g.shape),
            _resident(ln_b.shape),
        ],
        out_specs=pl.BlockSpec((tm, B_WIDTH), lambda i: (0, 0)),
        out_shape=jax.ShapeDtypeStruct((tm, B_WIDTH), _BF16),
        scratch_shapes=_conv_scratch(tm),
        compiler_params=pltpu.CompilerParams(
            dimension_semantics=("arbitrary",),
            vmem_limit_bytes=V7X_VMEM_LIMIT_BYTES),
        name="conv_first",
    )(u2d, u2d, conv_w, conv_b, ln_g, ln_b)


def _pool_fill_units(tile, tiles_per_seq, x_ref, prev_ref, next_ref, hp_scr):
    tm = x_ref.shape[0]
    seq = tm * tiles_per_seq
    rt = POOL_ROW_TILE

    def halos():
        start = (tile % tiles_per_seq) * tm
        halo_iota = jax.lax.broadcasted_iota(jnp.int32, (POOL_HALO, 1), 0)
        prev_ok = (start - POOL_HALO + halo_iota) >= 0
        next_ok = (start + tm + halo_iota) < seq
        hp_scr[0:POOL_HALO, :] = jnp.where(prev_ok, _rms_scale(prev_ref[...]), 0.0)
        hp_scr[POOL_HALO + tm:, :] = jnp.where(next_ok, _rms_scale(next_ref[...]), 0.0)

    def rows(r0):
        hp_scr[POOL_HALO + r0:POOL_HALO + r0 + rt, :] = _rms_scale(x_ref[r0:r0 + rt, :])

    return [halos] + [functools.partial(rows, r0) for r0 in range(0, tm, rt)]


def _pool_units(tm):
    return [(rc, gi) for rc in range(tm // POOL_ROW_TILE) for gi in range(len(POOL_WINDOWS))]


def _pool_unit(rc, gi, tile, tiles_per_seq, hp_scr, out_ref):
    rt = POOL_ROW_TILE
    w = POOL_WINDOWS[gi]
    half = w // 2
    tm = hp_scr.shape[0] - 2 * POOL_HALO
    gd = hp_scr.shape[1] // len(POOL_WINDOWS)
    seq = tm * tiles_per_seq
    lanes = slice(gi * gd, (gi + 1) * gd)
    r0 = rc * rt
    n = rt + 2 * POOL_HALO
    fwd = hp_scr[r0:r0 + n, lanes]
    span = 1
    while span < half:
        fwd = fwd + pltpu.roll(fwd, n - span, axis=0)
        span *= 2
    centred = fwd + pltpu.roll(fwd, half, axis=0)
    win = centred[POOL_HALO:POOL_HALO + rt]

    def edge_mean(e0):
        pos = ((tile % tiles_per_seq) * tm + r0 + e0
               + jax.lax.broadcasted_iota(jnp.int32, (POOL_HALO, LANES), 0))
        cnt = jnp.minimum(pos + (w - half), seq) - jnp.maximum(pos - half, 0)
        inv = 1.0 / cnt.astype(_F32)
        return win[e0:e0 + POOL_HALO] * jnp.concatenate([inv] * (gd // LANES), axis=1)

    lo = POOL_HALO if rc == 0 else 0
    hi = rt - POOL_HALO if rc == tm // rt - 1 else rt
    parts = [win[lo:hi] * (1.0 / w)]
    if lo:
        parts.insert(0, edge_mean(0))
    if hi < rt:
        parts.append(edge_mean(hi))
    mean = jnp.concatenate(parts, axis=0)
    pgv = mean - hp_scr[POOL_HALO + r0:POOL_HALO + r0 + rt, lanes]
    out_ref[r0:r0 + rt, lanes] = pgv.astype(_BF16)
    return pgv[rt - SUBLANES:, :LANES]


def _derived_zero(dep):
    bits = pltpu.bitcast(dep, jnp.uint32)
    bits = jax.lax.shift_right_logical(jax.lax.shift_right_logical(bits, jnp.uint32(16)), jnp.uint32(16))
    zero = pltpu.bitcast(bits, _F32)
    return jnp.concatenate([zero, zero], axis=0)


def _pool_first_kernel(x_ref, next_ref, out_ref, hp_scr):
    for unit in _pool_fill_units(0, 2, x_ref, next_ref, next_ref, hp_scr):
        unit()
    for rc, gi in _pool_units(x_ref.shape[0]):
        _pool_unit(rc, gi, 0, 2, hp_scr, out_ref)


def _pool_first(x2d):
    tm = TOKEN_TILE
    d = x2d.shape[1]
    return pl.pallas_call(
        _pool_first_kernel,
        grid=(1,),
        in_specs=[
            pl.BlockSpec((tm, d), lambda i: (0, 0)),
            pl.BlockSpec((POOL_HALO, d), lambda i: (tm // POOL_HALO, 0)),
        ],
        out_specs=pl.BlockSpec((tm, d), lambda i: (0, 0)),
        out_shape=jax.ShapeDtypeStruct((tm, d), _BF16),
        scratch_shapes=[pltpu.VMEM((tm + 2 * POOL_HALO, d), _F32)],
        compiler_params=pltpu.CompilerParams(
            dimension_semantics=("arbitrary",),
            vmem_limit_bytes=V7X_VMEM_LIMIT_BYTES),
        name="pool_first",
    )(x2d, x2d)


def _ffn_kernel(*refs, has_mixer, has_pool, has_final, has_prep, tiles_per_seq, n_chunks):
    it = iter(refs)
    x_ref = next(it)
    if has_mixer:
        ya_ref = next(it)
        yb0_ref = next(it)
        u_refs = (next(it), next(it), next(it))
        conv_refs = (next(it), next(it), next(it), next(it))
        wo_ref = next(it)
    if has_pool:
        pg0_ref = next(it)
        xn_refs = (next(it), next(it), next(it))
        gcol_ref, pm_ref, ps_ref = next(it), next(it), next(it)
    wgu_ref = next(it)
    wd_ref = next(it)
    if has_prep:
        prep_in = (next(it), next(it), next(it), next(it))
    if has_final:
        fg_ref = next(it)
    o_ref = next(it)
    if has_prep:
        prep_out = (next(it), next(it))
    h_scr = next(it)
    act_scr = next(it)
    if has_mixer:
        yb_scr = next(it)
        win_scr = next(it)
    if has_pool:
        pg_scr = next(it)
        hp_scr = next(it)
        pm_scr = next(it)
        xv_scr = next(it)
        hn_scr = next(it)
    if has_prep:
        gcol_scr = next(it)

    i = pl.program_id(0)
    n = pl.num_programs(0)
    slot = i % 2
    nxt = jnp.minimum(i + 1, n - 1)
    if has_prep:
        @pl.when(i == 0)
        def _():
            gcol_scr[...] = _gain_column(prep_in[0])

        @pl.when(i < n_chunks)
        def _():
            _prep_ffn_weights(gcol_scr, *prep_in[1:], *prep_out)

    tm, d = x_ref.shape
    xv = x_ref[...]
    units = []
    if has_mixer:
        @pl.when(i == 0)
        def _():
            yb_scr[0] = yb0_ref[...]

        yb = yb_scr[slot]
        _conv_fill_window(nxt, tiles_per_seq, *u_refs, win_scr)
        units = [functools.partial(_conv_unit, rc, lt, win_scr, *conv_refs, yb_scr.at[1 - slot])
                 for rc, lt in _conv_units(tm)]
    if has_pool:
        gd = d // len(POOL_WINDOWS)

        @pl.when(i == 0)
        def _():
            pg_scr[0] = pg0_ref[...]
            gain = _gain_column(gcol_ref)
            for gi in range(len(POOL_WINDOWS)):
                pm_scr[gi] = (pm_ref[gi] * gain[gi * gd:(gi + 1) * gd, :]).astype(_BF16)

        def front(x_val, pg_val):
            ys = [jnp.dot(pg_val[:, gi * gd:(gi + 1) * gd], pm_scr[gi], preferred_element_type=_F32)
                  for gi in range(len(POOL_WINDOWS))]
            xf = x_val + jnp.concatenate(ys, axis=1) * ps_ref[...]
            return xf, _rms_scale(xf).astype(_BF16)

        @pl.when(i == 0)
        def _():
            xv_scr[0], hn_scr[0] = front(x_ref[...], pg0_ref[...])

        units = _pool_fill_units(nxt, tiles_per_seq, *xn_refs, hp_scr) + [
            functools.partial(_pool_unit, rc, gi, nxt, tiles_per_seq, hp_scr, pg_scr.at[1 - slot])
            for rc, gi in _pool_units(tm)]
    pending = iter(units)
    units_per_dot = len(units) // n_chunks

    def side_work(n_units):
        dep = None
        for _ in range(n_units):
            out = next(pending)()
            dep = dep if out is None else out
        return dep

    side_work(len(units) - n_chunks * units_per_dot)
    if has_mixer:
        yab = jnp.concatenate([ya_ref[...], yb], axis=1)
        xv = xv + jnp.dot(yab, wo_ref[...], preferred_element_type=_F32)
    if has_pool:
        h_scr = hn_scr.at[slot]
        xv = xv_scr[slot]
        hb = h_scr[...]
        h_tile = hb[:2 * SUBLANES, :LANES].astype(_F32)
    else:
        hb = _rms_scale(xv).astype(_BF16)

    gu_width = wgu_ref.shape[2]
    for c in range(n_chunks):
        dep = side_work(units_per_dot)
        if dep is not None:
            h_scr[:2 * SUBLANES, :LANES] = (h_tile + _derived_zero(dep)).astype(_BF16)
            hb = h_scr[...]
        gu = jnp.dot(hb, wgu_ref[c], preferred_element_type=_F32)
        for b0 in range(0, gu_width, 2 * LANES):
            half_gate = gu[:, b0:b0 + LANES]
            up = gu[:, b0 + LANES:b0 + 2 * LANES]
            f0 = (c * gu_width + b0) // 2
            act = (half_gate * up) * _one_plus_tanh(half_gate)
            act_scr[:, f0:f0 + LANES] = act.astype(_BF16)
    acc = xv + jnp.dot(act_scr[...], wd_ref[...], preferred_element_type=_F32)
    if has_final:
        acc = _rmsnorm(acc, fg_ref[...])
    o_ref[...] = acc
    if has_pool:
        xv_scr[1 - slot], hn_scr[1 - slot] = front(xn_refs[0][...], pg_scr[1 - slot])


def _ffn(x2d, wgu, wd, mixer=None, pool=None, final_g=None, prep=None, seq=None):
    tokens, d = x2d.shape
    n_chunks, _, _ = wgu.shape
    ff = wd.shape[0]
    tm = TOKEN_TILE
    tiles_per_seq = seq // tm
    n_tiles = tokens // tm
    has_mixer = mixer is not None
    has_pool = pool is not None
    has_final = final_g is not None
    has_prep = prep is not None
    assert n_tiles >= n_chunks
    args = [x2d]
    specs = [pl.BlockSpec((tm, d), lambda i: (i, 0))]
    out_specs = [pl.BlockSpec((tm, d), lambda i: (i, 0))]
    out_shape = [jax.ShapeDtypeStruct((tokens, d), _F32)]
    scratch = [pltpu.VMEM((tm, d), _BF16), pltpu.VMEM((tm, ff), _BF16)]
    if has_mixer:
        ya2d, yb0, u2d, conv_w, conv_b, ln_g, ln_b, w_out = mixer
        hb = tm // CONV_HALO
        n_halo = tokens // CONV_HALO
        nxt = lambda i: jnp.minimum(i + 1, n_tiles - 1)
        args += [ya2d, yb0, u2d, u2d, u2d, conv_w, conv_b, ln_g, ln_b, w_out]
        specs += [
            pl.BlockSpec((tm, A_WIDTH), lambda i: (i, 0)),
            _resident(yb0.shape),
            pl.BlockSpec((tm, B_WIDTH), lambda i: (nxt(i), 0)),
            pl.BlockSpec((CONV_HALO, B_WIDTH), lambda i: (jnp.maximum(nxt(i) * hb - 1, 0), 0)),
            pl.BlockSpec((CONV_HALO, B_WIDTH),
                         lambda i: (jnp.minimum((nxt(i) + 1) * hb, n_halo - 1), 0)),
            _resident(conv_w.shape), _resident(conv_b.shape), _resident(ln_g.shape),
            _resident(ln_b.shape), _resident(w_out.shape),
        ]
        scratch += [pltpu.VMEM((2, tm, B_WIDTH), _BF16)] + _conv_scratch(tm)
    if has_pool:
        pg0, g_mix, pool_map, pool_scale = pool
        hb = tm // POOL_HALO
        n_halo = tokens // POOL_HALO
        nxt = lambda i: jnp.minimum(i + 1, n_tiles - 1)
        args += [pg0, x2d, x2d, x2d, g_mix, pool_map, pool_scale]
        specs += [
            _resident(pg0.shape),
            pl.BlockSpec((tm, d), lambda i: (nxt(i), 0)),
            pl.BlockSpec((POOL_HALO, d), lambda i: (jnp.maximum(nxt(i) * hb - 1, 0), 0)),
            pl.BlockSpec((POOL_HALO, d), lambda i: (jnp.minimum((nxt(i) + 1) * hb, n_halo - 1), 0)),
            _resident(g_mix.shape), _resident(pool_map.shape), _resident(pool_scale.shape),
        ]
        scratch += [pltpu.VMEM((2, tm, d), _BF16), pltpu.VMEM((tm + 2 * POOL_HALO, d), _F32),
                    pltpu.VMEM(pool_map.shape, _BF16), pltpu.VMEM((2, tm, d), _F32),
                    pltpu.VMEM((2, tm, d), _BF16)]
    args += [wgu, wd]
    specs += [_resident(wgu.shape), _resident(wd.shape)]
    if has_prep:
        prep_in, prep_out, prep_shape = _prep_specs(
            *prep, lambda i: jnp.minimum(i, n_chunks - 1))
        args += list(prep[:4])
        specs += prep_in
        out_specs += prep_out
        out_shape += prep_shape
        scratch.append(pltpu.VMEM((d, 1), _F32))
    if has_final:
        args.append(final_g)
        specs.append(_resident(final_g.shape))
    outs = pl.pallas_call(
        functools.partial(_ffn_kernel, has_mixer=has_mixer, has_pool=has_pool, has_final=has_final,
                          has_prep=has_prep, tiles_per_seq=tiles_per_seq, n_chunks=n_chunks),
        grid=(n_tiles,),
        in_specs=specs,
        out_specs=out_specs,
        out_shape=out_shape,
        scratch_shapes=scratch,
        compiler_params=pltpu.CompilerParams(
            dimension_semantics=("arbitrary",),
            vmem_limit_bytes=V7X_VMEM_LIMIT_BYTES),
        name="ffn_mixer" if has_mixer else "ffn_final",
    )(*args)
    return outs if has_prep else outs[0]


def kernel(x, norm_mix_g, norm_ffn_g, w_in_ab, fnet_map, conv_w, conv_b, conv_ln_g, conv_ln_b,
           w_out_ab, pool_map, pool_scale, ffn_w_gate, ffn_w_up, ffn_w_down, final_g):
    bsz, seq, d = x.shape
    tokens = bsz * seq
    row = lambda v: v.reshape(1, -1)

    ffn_weights = (ffn_w_gate, ffn_w_up, ffn_w_down)
    a_perm, u, w_out, wgu0, wd0 = _in_proj(x, row(norm_mix_g[0]), w_in_ab[0], w_out_ab[0],
                                           row(norm_ffn_g[0]), ffn_weights)
    ya = _fnet(a_perm, _dft_constants(seq), fnet_map[0])
    u2d = u.reshape(tokens, B_WIDTH)
    conv_p = (conv_w[0], row(conv_b[0]), row(conv_ln_g[0]), row(conv_ln_b[0]))
    mixer = (ya.reshape(tokens, A_WIDTH), _conv_first(u2d, *conv_p), u2d, *conv_p, w_out)
    x2, wgu1, wd1 = _ffn(x.reshape(tokens, d), wgu0, wd0, mixer=mixer,
                         prep=(row(norm_ffn_g[1]), *ffn_weights, 1), seq=seq)

    pool = (_pool_first(x2), row(norm_mix_g[1]), pool_map[0], row(pool_scale[0]))
    out = _ffn(x2, wgu1, wd1, pool=pool, final_g=row(final_g), seq=seq)
    return out.reshape(bsz, seq, d)
```

```python
import functools
import math

import jax
import jax.numpy as jnp
import numpy as np
from jax.experimental import pallas as pl
from jax.experimental.pallas import tpu as pltpu

RMS_EPS = 1e-6
LN_EPS = 1e-5

A_HEADS = 4
HEAD_DIM = 128
A_WIDTH = A_HEADS * HEAD_DIM
B_WIDTH = 512
CONV_WIDTH = 31
CONV_PAD = CONV_WIDTH // 2
POOL_WINDOWS = (2, 4, 8, 16)
POOL_HALO = 8

LANES = 128
SUBLANES = 8

DFT_RADIX = 8

V7X_VMEM_LIMIT_BYTES = 56 * 1024 * 1024

TOKEN_TILE = 512
IN_PROJ_TILE = 1024
X_RING_SLOTS = 3
FF_PREP_CHUNK = 256
BFLY_ROW_TILE = 16
FNET_COLS = 256
CONV_HALO = 16
CONV_ROW_TILE = 16
CONV_FIRST_ROW_TILE = 64
POOL_ROW_TILE = 32

_F32 = jnp.float32
_BF16 = jnp.bfloat16


def _resident(shape):
    nd = len(shape)
    return pl.BlockSpec(shape, lambda *_: (0,) * nd, pipeline_mode=pl.Buffered(1))


def _layer_row(arr, layer):
    return arr, _resident(arr.shape), layer


def _rms_scale(xv):
    return xv * jax.lax.rsqrt(jnp.mean(xv * xv, axis=-1, keepdims=True) + RMS_EPS)


def _rmsnorm(xv, g):
    return _rms_scale(xv) * g


def _one_plus_tanh(half_v):
    return 1.0 + jnp.tanh(half_v)


def _gain_column(g_ref, layer):
    g_row = g_ref[layer:layer + 1, :]
    return jnp.broadcast_to(g_row, (LANES, g_row.shape[1])).T[:, :1]


def _prep_ffn_weights(gcol_ref, wg_ref, wu_ref, wd_ref, wgu_dst, wd_dst):
    cw = wg_ref.shape[1]
    gain = gcol_ref[...]
    half_gain = gain * 0.5
    for b in range(cw // LANES):
        src = slice(b * LANES, (b + 1) * LANES)
        wgu_dst[:, 2 * b * LANES:(2 * b + 1) * LANES] = (wg_ref[:, src] * half_gain).astype(_BF16)
        wgu_dst[:, (2 * b + 1) * LANES:(2 * b + 2) * LANES] = (wu_ref[:, src] * gain).astype(_BF16)
    wd_dst[...] = wd_ref[...].astype(_BF16)


def _prep_specs(gcol, wg_all, wu_all, wd_all, layer, chunk_of):
    _, d, ff = wg_all.shape
    cw = FF_PREP_CHUNK
    in_specs = [
        gcol[1],
        pl.BlockSpec((None, d, cw), lambda *idx: (layer, 0, chunk_of(*idx))),
        pl.BlockSpec((None, d, cw), lambda *idx: (layer, 0, chunk_of(*idx))),
        pl.BlockSpec((None, cw, d), lambda *idx: (layer, chunk_of(*idx), 0)),
    ]
    out_specs = [
        pl.BlockSpec((None, d, 2 * cw), lambda *idx: (chunk_of(*idx), 0, 0)),
        pl.BlockSpec((cw, d), lambda *idx: (chunk_of(*idx), 0)),
    ]
    out_shape = [jax.ShapeDtypeStruct((ff // cw, d, 2 * cw), _BF16),
                 jax.ShapeDtypeStruct((ff, d), _BF16)]
    return in_specs, out_specs, out_shape


def _in_proj_kernel(x_hbm, gcol_ref, w_ref, wo_ref, fgcol_ref, wg_ref, wu_ref, wd_ref,
                    a_ref, u_ref, wo_out, wgu_out, wd_out, a_scr, w_scr, x_ring, x_sem,
                    fgcol_scr, *, n_chunks, g_layer, fg_layer):
    step = pl.program_id(0) * pl.num_programs(1) + pl.program_id(1)
    n_steps = pl.num_programs(0) * pl.num_programs(1)

    def x_copy(s):
        slot = s % X_RING_SLOTS
        return pltpu.make_async_copy(x_hbm.at[s], x_ring.at[slot], x_sem.at[slot])

    @pl.when(step == 0)
    def _():
        for s in range(X_RING_SLOTS - 1):
            x_copy(s).start()

    @pl.when(step + (X_RING_SLOTS - 1) < n_steps)
    def _():
        x_copy(step + (X_RING_SLOTS - 1)).start()

    @pl.when(step == 0)
    def _():
        fgcol_scr[...] = _gain_column(fgcol_ref, fg_layer)

    @pl.when(step < n_chunks)
    def _():
        _prep_ffn_weights(fgcol_scr, wg_ref, wu_ref, wd_ref, wgu_out, wd_out)

    @pl.when(step == 0)
    def _():
        gain = _gain_column(gcol_ref, g_layer)
        w_scr[:, :A_WIDTH] = (w_ref[:, :A_WIDTH] * gain).astype(_BF16)
        w_scr[:, A_WIDTH:] = (w_ref[:, A_WIDTH:] * (gain * 0.5)).astype(_BF16)
        wo_out[...] = wo_ref[...].astype(_BF16)

    x_copy(step).wait()
    h = _rms_scale(x_ring[step % X_RING_SLOTS]).astype(_BF16)
    pa = jnp.dot(h, w_scr[:, :A_WIDTH], preferred_element_type=_F32)
    rows = a_scr.shape[1] // DFT_RADIX
    for lt in range(A_WIDTH // LANES):
        lanes = slice(lt * LANES, (lt + 1) * LANES)
        a_scr[lt] = pa[:, lanes]
        for jr in range(DFT_RADIX):
            a_ref[jr, :, lanes] = a_scr[lt, pl.ds(jr, rows, stride=DFT_RADIX), :].astype(_BF16)
    p = jnp.dot(h, w_scr[:, A_WIDTH:], preferred_element_type=_F32)
    u_ref[...] = p[:, :B_WIDTH] * _one_plus_tanh(p[:, B_WIDTH:])


def _in_proj(x, gcol, w_in, w_out, ffn_gcol, ffn_weights):
    bsz, seq, d = x.shape
    tm = IN_PROJ_TILE
    inner = seq // DFT_RADIX
    steps_per_batch = seq // tm
    n_chunks = ffn_weights[0].shape[2] // FF_PREP_CHUNK
    n_steps = bsz * steps_per_batch
    assert n_steps >= max(n_chunks, X_RING_SLOTS - 1)
    chunk_of = lambda b, i: jnp.minimum(b * steps_per_batch + i, n_chunks - 1)
    prep_in, prep_out, prep_shape = _prep_specs(ffn_gcol, *ffn_weights, 0, chunk_of)
    return pl.pallas_call(
        functools.partial(_in_proj_kernel, n_chunks=n_chunks, g_layer=gcol[2],
                          fg_layer=ffn_gcol[2]),
        grid=(bsz, steps_per_batch),
        in_specs=[
            pl.BlockSpec(memory_space=pl.ANY),
            gcol[1],
            _resident(w_in.shape),
            _resident(w_out.shape),
        ] + prep_in,
        out_specs=[
            pl.BlockSpec((None, DFT_RADIX, tm // DFT_RADIX, A_WIDTH), lambda b, i: (b, 0, i, 0)),
            pl.BlockSpec((None, tm, B_WIDTH), lambda b, i: (b, i, 0)),
            _resident(w_out.shape),
        ] + prep_out,
        out_shape=[
            jax.ShapeDtypeStruct((bsz, DFT_RADIX, inner, A_WIDTH), _BF16),
            jax.ShapeDtypeStruct((bsz, seq, B_WIDTH), _F32),
            jax.ShapeDtypeStruct(w_out.shape, _BF16),
        ] + prep_shape,
        scratch_shapes=[pltpu.VMEM((A_WIDTH // LANES, tm, LANES), _F32),
                        pltpu.VMEM(w_in.shape, _BF16),
                        pltpu.VMEM((X_RING_SLOTS, tm, d), _F32),
                        pltpu.SemaphoreType.DMA((X_RING_SLOTS,)),
                        pltpu.VMEM((d, 1), _F32)],
        compiler_params=pltpu.CompilerParams(
            dimension_semantics=("arbitrary", "arbitrary"),
            vmem_limit_bytes=V7X_VMEM_LIMIT_BYTES),
        name="in_proj",
    )(x.reshape(n_steps, tm, d), gcol[0], w_in, w_out, ffn_gcol[0], *ffn_weights)


def _cadd(a, b):
    return (a[0] + b[0], a[1] + b[1])


def _csub(a, b):
    return (a[0] - b[0], a[1] - b[1])


def _dft4(a0, a1, a2, a3):
    s0, s1 = _cadd(a0, a2), _csub(a0, a2)
    s2, s3 = _cadd(a1, a3), _csub(a1, a3)
    return (_cadd(s0, s2), (s1[0] + s3[1], s1[1] - s3[0]),
            _csub(s0, s2), (s1[0] - s3[1], s1[1] + s3[0]))


def _mul_w8(k, z):
    r, i = z
    h = math.sqrt(0.5)
    if k == 0:
        return z
    if k == 1:
        return (h * (r + i), h * (i - r))
    if k == 2:
        return (i, -r)
    return (h * (i - r), -h * (r + i))


def _bfly_unit(r0, lanes, inner, twc_ref, tws_ref, yr_scr, yi_scr):
    rt = BFLY_ROW_TILE
    z = []
    for jr in range(DFT_RADIX):
        rows = slice(jr * inner + r0, jr * inner + r0 + rt)
        yr = yr_scr[rows, lanes]
        yi = yi_scr[rows, lanes]
        if jr == 0:
            z.append((yr, yi))
        else:
            tc = twc_ref[rows, :]
            ts = tws_ref[rows, :]
            z.append((yr * tc + yi * ts, yi * tc - yr * ts))
    ev = _dft4(z[0], z[2], z[4], z[6])
    od = _dft4(z[1], z[3], z[5], z[7])
    for k in range(4):
        w = _mul_w8(k, od[k])
        lo = _cadd(ev[k], w)
        hi = _csub(ev[k], w)
        rows_lo = slice(k * inner + r0, k * inner + r0 + rt)
        rows_hi = slice((k + 4) * inner + r0, (k + 4) * inner + r0 + rt)
        yr_scr[rows_lo, lanes] = lo[0]
        yi_scr[rows_lo, lanes] = lo[1]
        yr_scr[rows_hi, lanes] = hi[0]
        yi_scr[rows_hi, lanes] = hi[1]


def _fnet_kernel(a_ref, cs_ref, twc_ref, tws_ref, cdsd_ref, map_ref, y_ref, yr_scr, yi_scr):
    seq = y_ref.shape[0]
    inner = seq // DFT_RADIX

    for c0 in range(0, A_WIDTH, FNET_COLS):
        cols = slice(c0, c0 + FNET_COLS)
        for jr in range(DFT_RADIX):
            yy = jnp.dot(cs_ref[...], a_ref[jr, :, cols], preferred_element_type=_F32)
            yr_scr[jr * inner:(jr + 1) * inner, cols] = yy[:inner]
            yi_scr[jr * inner:(jr + 1) * inner, cols] = yy[inner:]

        for r0 in range(0, inner, BFLY_ROW_TILE):
            for l0 in range(c0, c0 + FNET_COLS, LANES):
                _bfly_unit(r0, slice(l0, l0 + LANES), inner, twc_ref, tws_ref, yr_scr, yi_scr)

        for hd in range(c0 // HEAD_DIM, (c0 + FNET_COLS) // HEAD_DIM):
            lanes = slice(hd * HEAD_DIM, (hd + 1) * HEAD_DIM)
            lhs = jnp.concatenate([yr_scr[:, lanes].astype(_BF16),
                                   yi_scr[:, lanes].astype(_BF16)], axis=1)
            f = jnp.dot(lhs, cdsd_ref[...], preferred_element_type=_F32)
            ya = jnp.dot(f.astype(_BF16), map_ref[hd].astype(_BF16), preferred_element_type=_F32)
            y_ref[:, lanes] = ya.astype(_BF16)


def _fnet(a_perm, consts, fmap):
    bsz, _, inner, _ = a_perm.shape
    seq = inner * DFT_RADIX
    cs, twc, tws, cdsd = consts
    return pl.pallas_call(
        _fnet_kernel,
        grid=(bsz,),
        in_specs=[
            pl.BlockSpec((None, DFT_RADIX, inner, A_WIDTH), lambda b: (b, 0, 0, 0)),
            _resident(cs.shape), _resident(twc.shape), _resident(tws.shape),
            _resident(cdsd.shape), _resident(fmap.shape),
        ],
        out_specs=pl.BlockSpec((None, seq, A_WIDTH), lambda b: (b, 0, 0)),
        out_shape=jax.ShapeDtypeStruct((bsz, seq, A_WIDTH), _BF16),
        scratch_shapes=[
            pltpu.VMEM((seq, A_WIDTH), _F32),
            pltpu.VMEM((seq, A_WIDTH), _F32),
        ],
        compiler_params=pltpu.CompilerParams(
            dimension_semantics=("arbitrary",),
            vmem_limit_bytes=V7X_VMEM_LIMIT_BYTES),
        name="fnet",
    )(a_perm, cs, twc, tws, cdsd, fmap)


def _dft_constants(seq):
    inner = seq // DFT_RADIX
    k = np.arange(inner, dtype=np.float64)
    ang = 2.0 * np.pi * np.outer(k, k) / inner
    cs = np.concatenate([np.cos(ang), -np.sin(ang)], axis=0)
    jr = np.arange(DFT_RADIX, dtype=np.float64)[:, None]
    tw = 2.0 * np.pi * (jr * k[None, :]) / seq
    twc = np.repeat(np.cos(tw).reshape(seq, 1), LANES, axis=1)
    tws = np.repeat(np.sin(tw).reshape(seq, 1), LANES, axis=1)
    d = np.arange(HEAD_DIM, dtype=np.float64)
    angd = 2.0 * np.pi * np.outer(d, d) / HEAD_DIM
    scale = 1.0 / math.sqrt(seq * HEAD_DIM)
    cdsd = np.concatenate([np.cos(angd), np.sin(angd)], axis=0) * scale
    return (jnp.asarray(cs, _F32).astype(_BF16), jnp.asarray(twc, _F32), jnp.asarray(tws, _F32),
            jnp.asarray(cdsd, _F32).astype(_BF16))


def _conv_fill_window(tile, tiles_per_seq, main_ref, prev_ref, next_ref, win_scr):
    tm = main_ref.shape[0]
    halo = CONV_HALO
    pos = jnp.zeros((halo, 1), jnp.int32) + tile % tiles_per_seq
    win_scr[0:halo, :] = jnp.where(pos == 0, 0.0, prev_ref[...])
    win_scr[halo:halo + tm, :] = main_ref[...]
    win_scr[halo + tm:, :] = jnp.where(pos == tiles_per_seq - 1, 0.0, next_ref[...])


def _conv_unit(rc, lt, win_scr, cw_ref, cb_ref, lg_ref, lb_ref, out_ref, ct=CONV_ROW_TILE):
    halo = CONV_HALO
    r0 = rc * ct
    lanes = slice(lt * LANES, (lt + 1) * LANES)
    first = halo - CONV_PAD
    span = ct + 2 * halo
    win = win_scr[r0:r0 + span, lanes]
    acc = None
    for s in range(SUBLANES):
        rot = win if s == 0 else pltpu.roll(win, span - s, axis=0)
        for q in range((first + CONV_WIDTH - 1) // SUBLANES + 1):
            k = SUBLANES * q + s - first
            if 0 <= k < CONV_WIDTH:
                term = rot[SUBLANES * q:SUBLANES * q + ct] * cw_ref[k:k + 1, lanes]
                acc = term if acc is None else acc + term
    cv = acc + cb_ref[:, lanes]
    mu = jnp.mean(cv, axis=-1, keepdims=True)
    dv = cv - mu
    var = jnp.mean(dv * dv, axis=-1, keepdims=True)
    half_yn = (dv * jax.lax.rsqrt(var + LN_EPS)) * (lg_ref[:, lanes] * 0.5) + lb_ref[:, lanes] * 0.5
    y = half_yn * _one_plus_tanh(half_yn)
    out_ref[r0:r0 + ct, lanes] = y.astype(_BF16)


def _conv_units(tm, ct=CONV_ROW_TILE):
    return [(rc, lt) for rc in range(tm // ct) for lt in range(B_WIDTH // LANES)]


def _conv_scratch(tm):
    return [pltpu.VMEM((tm + 2 * CONV_HALO, B_WIDTH), _F32)]


def _conv_first_kernel(main_ref, next_ref, cw_ref, cb_ref, lg_ref, lb_ref, out_ref, win_scr):
    _conv_fill_window(0, 2, main_ref, next_ref, next_ref, win_scr)
    for rc, lt in _conv_units(main_ref.shape[0], CONV_FIRST_ROW_TILE):
        _conv_unit(rc, lt, win_scr, cw_ref, cb_ref, lg_ref, lb_ref, out_ref, CONV_FIRST_ROW_TILE)


def _conv_first(u2d, conv_w, conv_b, ln_g, ln_b):
    tm = TOKEN_TILE
    return pl.pallas_call(
        _conv_first_kernel,
        grid=(1,),
        in_specs=[
            pl.BlockSpec((tm, B_WIDTH), lambda i: (0, 0)),
            pl.BlockSpec((CONV_HALO, B_WIDTH), lambda i: (tm // CONV_HALO, 0)),
            _resident(conv_w.shape), _resident(conv_b.shape), _resident(ln_g.shape),
            _resident(ln_b.shape),
        ],
        out_specs=pl.BlockSpec((tm, B_WIDTH), lambda i: (0, 0)),
        out_shape=jax.ShapeDtypeStruct((tm, B_WIDTH), _BF16),
        scratch_shapes=_conv_scratch(tm),
        compiler_params=pltpu.CompilerParams(
            dimension_semantics=("arbitrary",),
            vmem_limit_bytes=V7X_VMEM_LIMIT_BYTES),
        name="conv_first",
    )(u2d, u2d, conv_w, conv_b, ln_g, ln_b)


def _pool_fill_units(tile, tiles_per_seq, x_ref, prev_ref, next_ref, hp_scr):
    tm = x_ref.shape[0]
    seq = tm * tiles_per_seq
    rt = POOL_ROW_TILE

    def halos():
        start = (tile % tiles_per_seq) * tm
        halo_iota = jax.lax.broadcasted_iota(jnp.int32, (POOL_HALO, 1), 0)
        prev_ok = (start - POOL_HALO + halo_iota) >= 0
        next_ok = (start + tm + halo_iota) < seq
        hp_scr[0:POOL_HALO, :] = jnp.where(prev_ok, _rms_scale(prev_ref[...]), 0.0)
        hp_scr[POOL_HALO + tm:, :] = jnp.where(next_ok, _rms_scale(next_ref[...]), 0.0)

    def rows(r0):
        hp_scr[POOL_HALO + r0:POOL_HALO + r0 + rt, :] = _rms_scale(x_ref[r0:r0 + rt, :])

    return [halos] + [functools.partial(rows, r0) for r0 in range(0, tm, rt)]


def _pool_units(tm):
    return [(rc, gi) for rc in range(tm // POOL_ROW_TILE) for gi in range(len(POOL_WINDOWS))]


def _pool_unit(rc, gi, tile, tiles_per_seq, hp_scr, out_ref):
    rt = POOL_ROW_TILE
    w = POOL_WINDOWS[gi]
    half = w // 2
    tm = hp_scr.shape[0] - 2 * POOL_HALO
    gd = hp_scr.shape[1] // len(POOL_WINDOWS)
    seq = tm * tiles_per_seq
    lanes = slice(gi * gd, (gi + 1) * gd)
    r0 = rc * rt
    n = rt + 2 * POOL_HALO
    fwd = hp_scr[r0:r0 + n, lanes]
    span = 1
    while span < half:
        fwd = fwd + pltpu.roll(fwd, n - span, axis=0)
        span *= 2
    centred = fwd + pltpu.roll(fwd, half, axis=0)
    win = centred[POOL_HALO:POOL_HALO + rt]

    def edge_mean(e0):
        pos = ((tile % tiles_per_seq) * tm + r0 + e0
               + jax.lax.broadcasted_iota(jnp.int32, (POOL_HALO, LANES), 0))
        cnt = jnp.minimum(pos + (w - half), seq) - jnp.maximum(pos - half, 0)
        inv = 1.0 / cnt.astype(_F32)
        return win[e0:e0 + POOL_HALO] * jnp.concatenate([inv] * (gd // LANES), axis=1)

    lo = POOL_HALO if rc == 0 else 0
    hi = rt - POOL_HALO if rc == tm // rt - 1 else rt
    parts = [win[lo:hi] * (1.0 / w)]
    if lo:
        parts.insert(0, edge_mean(0))
    if hi < rt:
        parts.append(edge_mean(hi))
    mean = jnp.concatenate(parts, axis=0)
    pgv = mean - hp_scr[POOL_HALO + r0:POOL_HALO + r0 + rt, lanes]
    out_ref[r0:r0 + rt, lanes] = pgv.astype(_BF16)
    return pgv[rt - SUBLANES:, :LANES]


def _derived_zero(dep):
    bits = pltpu.bitcast(dep, jnp.uint32)
    bits = jax.lax.shift_right_logical(jax.lax.shift_right_logical(bits, jnp.uint32(16)), jnp.uint32(16))
    zero = pltpu.bitcast(bits, _F32)
    return jnp.concatenate([zero, zero], axis=0)


def _pool_first_kernel(x_ref, next_ref, out_ref, hp_scr):
    for unit in _pool_fill_units(0, 2, x_ref, next_ref, next_ref, hp_scr):
        unit()
    for rc, gi in _pool_units(x_ref.shape[0]):
        _pool_unit(rc, gi, 0, 2, hp_scr, out_ref)


def _pool_first(x2d):
    tm = TOKEN_TILE
    d = x2d.shape[1]
    return pl.pallas_call(
        _pool_first_kernel,
        grid=(1,),
        in_specs=[
            pl.BlockSpec((tm, d), lambda i: (0, 0)),
            pl.BlockSpec((POOL_HALO, d), lambda i: (tm // POOL_HALO, 0)),
        ],
        out_specs=pl.BlockSpec((tm, d), lambda i: (0, 0)),
        out_shape=jax.ShapeDtypeStruct((tm, d), _BF16),
        scratch_shapes=[pltpu.VMEM((tm + 2 * POOL_HALO, d), _F32)],
        compiler_params=pltpu.CompilerParams(
            dimension_semantics=("arbitrary",),
            vmem_limit_bytes=V7X_VMEM_LIMIT_BYTES),
        name="pool_first",
    )(x2d, x2d)


def _ffn_kernel(*refs, has_mixer, has_pool, has_final, has_prep, tiles_per_seq, n_chunks,
                prep_layer, pool_layer):
    it = iter(refs)
    x_ref = next(it)
    if has_mixer:
        ya_ref = next(it)
        yb0_ref = next(it)
        u_refs = (next(it), next(it), next(it))
        conv_refs = (next(it), next(it), next(it), next(it))
        wo_ref = next(it)
    if has_pool:
        pg0_ref = next(it)
        xn_refs = (next(it), next(it), next(it))
        gcol_ref, pm_ref, ps_ref = next(it), next(it), next(it)
    wgu_ref = next(it)
    wd_ref = next(it)
    if has_prep:
        prep_in = (next(it), next(it), next(it), next(it))
    if has_final:
        fg_ref = next(it)
    o_ref = next(it)
    if has_prep:
        prep_out = (next(it), next(it))
    h_scr = next(it)
    act_scr = next(it)
    if has_mixer:
        yb_scr = next(it)
        win_scr = next(it)
    if has_pool:
        pg_scr = next(it)
        hp_scr = next(it)
        pm_scr = next(it)
        xv_scr = next(it)
        hn_scr = next(it)
    if has_prep:
        gcol_scr = next(it)

    i = pl.program_id(0)
    n = pl.num_programs(0)
    slot = i % 2
    nxt = jnp.minimum(i + 1, n - 1)
    if has_prep:
        @pl.when(i == 0)
        def _():
            gcol_scr[...] = _gain_column(prep_in[0], prep_layer)

        @pl.when(i < n_chunks)
        def _():
            _prep_ffn_weights(gcol_scr, *prep_in[1:], *prep_out)

    tm, d = x_ref.shape
    xv = x_ref[...]
    units = []
    if has_mixer:
        @pl.when(i == 0)
        def _():
            yb_scr[0] = yb0_ref[...]

        yb = yb_scr[slot]
        _conv_fill_window(nxt, tiles_per_seq, *u_refs, win_scr)
        units = [functools.partial(_conv_unit, rc, lt, win_scr, *conv_refs, yb_scr.at[1 - slot])
                 for rc, lt in _conv_units(tm)]
    if has_pool:
        gd = d // len(POOL_WINDOWS)

        @pl.when(i == 0)
        def _():
            pg_scr[0] = pg0_ref[...]
            gain = _gain_column(gcol_ref, pool_layer)
            for gi in range(len(POOL_WINDOWS)):
                pm_scr[gi] = (pm_ref[gi] * gain[gi * gd:(gi + 1) * gd, :]).astype(_BF16)

        def front(x_val, pg_val):
            ys = [jnp.dot(pg_val[:, gi * gd:(gi + 1) * gd], pm_scr[gi], preferred_element_type=_F32)
                  for gi in range(len(POOL_WINDOWS))]
            xf = x_val + jnp.concatenate(ys, axis=1) * ps_ref[...]
            return xf, _rms_scale(xf).astype(_BF16)

        @pl.when(i == 0)
        def _():
            xv_scr[0], hn_scr[0] = front(x_ref[...], pg0_ref[...])

        units = _pool_fill_units(nxt, tiles_per_seq, *xn_refs, hp_scr) + [
            functools.partial(_pool_unit, rc, gi, nxt, tiles_per_seq, hp_scr, pg_scr.at[1 - slot])
            for rc, gi in _pool_units(tm)]
    pending = iter(units)
    units_per_dot = len(units) // n_chunks

    def side_work(n_units):
        dep = None
        for _ in range(n_units):
            out = next(pending)()
            dep = dep if out is None else out
        return dep

    side_work(len(units) - n_chunks * units_per_dot)
    if has_mixer:
        yab = jnp.concatenate([ya_ref[...], yb], axis=1)
        xv = xv + jnp.dot(yab, wo_ref[...], preferred_element_type=_F32)
    if has_pool:
        h_scr = hn_scr.at[slot]
        xv = xv_scr[slot]
        hb = h_scr[...]
        h_tile = hb[:2 * SUBLANES, :LANES].astype(_F32)
    else:
        hb = _rms_scale(xv).astype(_BF16)

    gu_width = wgu_ref.shape[2]
    for c in range(n_chunks):
        dep = side_work(units_per_dot)
        if dep is not None:
            h_scr[:2 * SUBLANES, :LANES] = (h_tile + _derived_zero(dep)).astype(_BF16)
            hb = h_scr[...]
        gu = jnp.dot(hb, wgu_ref[c], preferred_element_type=_F32)
        for b0 in range(0, gu_width, 2 * LANES):
            half_gate = gu[:, b0:b0 + LANES]
            up = gu[:, b0 + LANES:b0 + 2 * LANES]
            f0 = (c * gu_width + b0) // 2
            act = (half_gate * up) * _one_plus_tanh(half_gate)
            act_scr[:, f0:f0 + LANES] = act.astype(_BF16)
    acc = xv + jnp.dot(act_scr[...], wd_ref[...], preferred_element_type=_F32)
    if has_final:
        acc = _rmsnorm(acc, fg_ref[...])
    o_ref[...] = acc
    if has_pool:
        xv_scr[1 - slot], hn_scr[1 - slot] = front(xn_refs[0][...], pg_scr[1 - slot])


def _ffn(x2d, wgu, wd, mixer=None, pool=None, final_g=None, prep=None, seq=None):
    tokens, d = x2d.shape
    n_chunks, _, _ = wgu.shape
    ff = wd.shape[0]
    tm = TOKEN_TILE
    tiles_per_seq = seq // tm
    n_tiles = tokens // tm
    has_mixer = mixer is not None
    has_pool = pool is not None
    has_final = final_g is not None
    has_prep = prep is not None
    assert n_tiles >= n_chunks
    args = [x2d]
    specs = [pl.BlockSpec((tm, d), lambda i: (i, 0))]
    out_specs = [pl.BlockSpec((tm, d), lambda i: (i, 0))]
    out_shape = [jax.ShapeDtypeStruct((tokens, d), _F32)]
    scratch = [pltpu.VMEM((tm, d), _BF16), pltpu.VMEM((tm, ff), _BF16)]
    if has_mixer:
        ya2d, yb0, u2d, conv_w, conv_b, ln_g, ln_b, w_out = mixer
        hb = tm // CONV_HALO
        n_halo = tokens // CONV_HALO
        nxt = lambda i: jnp.minimum(i + 1, n_tiles - 1)
        args += [ya2d, yb0, u2d, u2d, u2d, conv_w, conv_b, ln_g, ln_b, w_out]
        specs += [
            pl.BlockSpec((tm, A_WIDTH), lambda i: (i, 0)),
            _resident(yb0.shape),
            pl.BlockSpec((tm, B_WIDTH), lambda i: (nxt(i), 0)),
            pl.BlockSpec((CONV_HALO, B_WIDTH), lambda i: (jnp.maximum(nxt(i) * hb - 1, 0), 0)),
            pl.BlockSpec((CONV_HALO, B_WIDTH),
                         lambda i: (jnp.minimum((nxt(i) + 1) * hb, n_halo - 1), 0)),
            _resident(conv_w.shape), _resident(conv_b.shape), _resident(ln_g.shape),
            _resident(ln_b.shape), _resident(w_out.shape),
        ]
        scratch += [pltpu.VMEM((2, tm, B_WIDTH), _BF16)] + _conv_scratch(tm)
    if has_pool:
        pg0, g_mix, pool_map, pool_scale = pool
        hb = tm // POOL_HALO
        n_halo = tokens // POOL_HALO
        nxt = lambda i: jnp.minimum(i + 1, n_tiles - 1)
        args += [pg0, x2d, x2d, x2d, g_mix[0], pool_map, pool_scale]
        specs += [
            _resident(pg0.shape),
            pl.BlockSpec((tm, d), lambda i: (nxt(i), 0)),
            pl.BlockSpec((POOL_HALO, d), lambda i: (jnp.maximum(nxt(i) * hb - 1, 0), 0)),
            pl.BlockSpec((POOL_HALO, d), lambda i: (jnp.minimum((nxt(i) + 1) * hb, n_halo - 1), 0)),
            g_mix[1], _resident(pool_map.shape), _resident(pool_scale.shape),
        ]
        scratch += [pltpu.VMEM((2, tm, d), _BF16), pltpu.VMEM((tm + 2 * POOL_HALO, d), _F32),
                    pltpu.VMEM(pool_map.shape, _BF16), pltpu.VMEM((2, tm, d), _F32),
                    pltpu.VMEM((2, tm, d), _BF16)]
    args += [wgu, wd]
    specs += [_resident(wgu.shape), _resident(wd.shape)]
    if has_prep:
        prep_in, prep_out, prep_shape = _prep_specs(
            *prep, lambda i: jnp.minimum(i, n_chunks - 1))
        args += [prep[0][0], *prep[1:4]]
        specs += prep_in
        out_specs += prep_out
        out_shape += prep_shape
        scratch.append(pltpu.VMEM((d, 1), _F32))
    if has_final:
        args.append(final_g)
        specs.append(_resident(final_g.shape))
    outs = pl.pallas_call(
        functools.partial(_ffn_kernel, has_mixer=has_mixer, has_pool=has_pool, has_final=has_final,
                          has_prep=has_prep, tiles_per_seq=tiles_per_seq, n_chunks=n_chunks,
                          prep_layer=prep[0][2] if has_prep else None,
                          pool_layer=pool[1][2] if has_pool else None),
        grid=(n_tiles,),
        in_specs=specs,
        out_specs=out_specs,
        out_shape=out_shape,
        scratch_shapes=scratch,
        compiler_params=pltpu.CompilerParams(
            dimension_semantics=("arbitrary",),
            vmem_limit_bytes=V7X_VMEM_LIMIT_BYTES),
        name="ffn_mixer" if has_mixer else "ffn_final",
    )(*args)
    return outs if has_prep else outs[0]


def kernel(x, norm_mix_g, norm_ffn_g, w_in_ab, fnet_map, conv_w, conv_b, conv_ln_g, conv_ln_b,
           w_out_ab, pool_map, pool_scale, ffn_w_gate, ffn_w_up, ffn_w_down, final_g):
    bsz, seq, d = x.shape
    tokens = bsz * seq
    row = lambda v: v.reshape(1, -1)

    ffn_weights = (ffn_w_gate, ffn_w_up, ffn_w_down)
    a_perm, u, w_out, wgu0, wd0 = _in_proj(x, _layer_row(norm_mix_g, 0), w_in_ab[0], w_out_ab[0],
                                           _layer_row(norm_ffn_g, 0), ffn_weights)
    ya = _fnet(a_perm, _dft_constants(seq), fnet_map[0])
    u2d = u.reshape(tokens, B_WIDTH)
    conv_p = (conv_w[0], row(conv_b[0]), row(conv_ln_g[0]), row(conv_ln_b[0]))
    mixer = (ya.reshape(tokens, A_WIDTH), _conv_first(u2d, *conv_p), u2d, *conv_p, w_out)
    x2, wgu1, wd1 = _ffn(x.reshape(tokens, d), wgu0, wd0, mixer=mixer,
                         prep=(_layer_row(norm_ffn_g, 1), *ffn_weights, 1), seq=seq)

    pool = (_pool_first(x2), _layer_row(norm_mix_g, 1), pool_map[0], row(pool_scale[0]))
    out = _ffn(x2, wgu1, wd1, pool=pool, final_g=row(final_g), seq=seq)
    return out.reshape(bsz, seq, d)
```

```python
import functools
import math

import jax
import jax.numpy as jnp
import numpy as np
from jax.experimental import pallas as pl
from jax.experimental.pallas import tpu as pltpu

RMS_EPS = 1e-6
LN_EPS = 1e-5

A_HEADS = 4
HEAD_DIM = 128
A_WIDTH = A_HEADS * HEAD_DIM
B_WIDTH = 512
CONV_WIDTH = 31
CONV_PAD = CONV_WIDTH // 2
POOL_WINDOWS = (2, 4, 8, 16)
POOL_HALO = 8

LANES = 128
SUBLANES = 8

DFT_RADIX = 8

V7X_VMEM_LIMIT_BYTES = 56 * 1024 * 1024

TOKEN_TILE = 512
IN_PROJ_TILE = 1024
X_RING_SLOTS = 3
FF_PREP_CHUNK = 256
BFLY_ROW_TILE = 16
FNET_COLS = 256
CONV_HALO = 16
CONV_ROW_TILE = 16
CONV_FIRST_ROW_TILE = 64
POOL_ROW_TILE = 32

_F32 = jnp.float32
_BF16 = jnp.bfloat16


def _resident(shape):
    nd = len(shape)
    return pl.BlockSpec(shape, lambda *_: (0,) * nd, pipeline_mode=pl.Buffered(1))


def _resident_slab(shape, index=0):
    nd = len(shape)
    return pl.BlockSpec((None,) + tuple(shape[1:]), lambda *_: (index,) + (0,) * (nd - 1),
                        pipeline_mode=pl.Buffered(1))


def _layer_row(arr, layer):
    return arr, _resident(arr.shape), layer


def _rms_scale(xv):
    return xv * jax.lax.rsqrt(jnp.mean(xv * xv, axis=-1, keepdims=True) + RMS_EPS)


def _rmsnorm(xv, g):
    return _rms_scale(xv) * g


def _one_plus_tanh(half_v):
    return 1.0 + jnp.tanh(half_v)


def _gain_column(g_ref, layer):
    g_row = g_ref[layer:layer + 1, :]
    return jnp.broadcast_to(g_row, (LANES, g_row.shape[1])).T[:, :1]


def _prep_ffn_weights(gcol_ref, wg_ref, wu_ref, wd_ref, wgu_dst, wd_dst):
    cw = wg_ref.shape[1]
    gain = gcol_ref[...]
    half_gain = gain * 0.5
    for b in range(cw // LANES):
        src = slice(b * LANES, (b + 1) * LANES)
        wgu_dst[:, 2 * b * LANES:(2 * b + 1) * LANES] = (wg_ref[:, src] * half_gain).astype(_BF16)
        wgu_dst[:, (2 * b + 1) * LANES:(2 * b + 2) * LANES] = (wu_ref[:, src] * gain).astype(_BF16)
    wd_dst[...] = wd_ref[...].astype(_BF16)


def _prep_specs(gcol, wg_all, wu_all, wd_all, layer, chunk_of):
    _, d, ff = wg_all.shape
    cw = FF_PREP_CHUNK
    in_specs = [
        gcol[1],
        pl.BlockSpec((None, d, cw), lambda *idx: (layer, 0, chunk_of(*idx))),
        pl.BlockSpec((None, d, cw), lambda *idx: (layer, 0, chunk_of(*idx))),
        pl.BlockSpec((None, cw, d), lambda *idx: (layer, chunk_of(*idx), 0)),
    ]
    out_specs = [
        pl.BlockSpec((None, d, 2 * cw), lambda *idx: (chunk_of(*idx), 0, 0)),
        pl.BlockSpec((cw, d), lambda *idx: (chunk_of(*idx), 0)),
    ]
    out_shape = [jax.ShapeDtypeStruct((ff // cw, d, 2 * cw), _BF16),
                 jax.ShapeDtypeStruct((ff, d), _BF16)]
    return in_specs, out_specs, out_shape


def _in_proj_kernel(x_hbm, gcol_ref, w_ref, wo_ref, fgcol_ref, wg_ref, wu_ref, wd_ref,
                    a_ref, u_ref, wo_out, wgu_out, wd_out, a_scr, w_scr, x_ring, x_sem,
                    fgcol_scr, *, n_chunks, g_layer, fg_layer):
    step = pl.program_id(0) * pl.num_programs(1) + pl.program_id(1)
    n_steps = pl.num_programs(0) * pl.num_programs(1)

    def x_copy(s):
        slot = s % X_RING_SLOTS
        return pltpu.make_async_copy(x_hbm.at[s], x_ring.at[slot], x_sem.at[slot])

    @pl.when(step == 0)
    def _():
        for s in range(X_RING_SLOTS - 1):
            x_copy(s).start()

    @pl.when(step + (X_RING_SLOTS - 1) < n_steps)
    def _():
        x_copy(step + (X_RING_SLOTS - 1)).start()

    @pl.when(step == 0)
    def _():
        fgcol_scr[...] = _gain_column(fgcol_ref, fg_layer)

    @pl.when(step < n_chunks)
    def _():
        _prep_ffn_weights(fgcol_scr, wg_ref, wu_ref, wd_ref, wgu_out, wd_out)

    @pl.when(step == 0)
    def _():
        gain = _gain_column(gcol_ref, g_layer)
        w_scr[:, :A_WIDTH] = (w_ref[:, :A_WIDTH] * gain).astype(_BF16)
        w_scr[:, A_WIDTH:] = (w_ref[:, A_WIDTH:] * (gain * 0.5)).astype(_BF16)
        wo_out[...] = wo_ref[...].astype(_BF16)

    x_copy(step).wait()
    h = _rms_scale(x_ring[step % X_RING_SLOTS]).astype(_BF16)
    pa = jnp.dot(h, w_scr[:, :A_WIDTH], preferred_element_type=_F32)
    rows = a_scr.shape[1] // DFT_RADIX
    for lt in range(A_WIDTH // LANES):
        lanes = slice(lt * LANES, (lt + 1) * LANES)
        a_scr[lt] = pa[:, lanes]
        for jr in range(DFT_RADIX):
            a_ref[jr, :, lanes] = a_scr[lt, pl.ds(jr, rows, stride=DFT_RADIX), :].astype(_BF16)
    p = jnp.dot(h, w_scr[:, A_WIDTH:], preferred_element_type=_F32)
    u_ref[...] = p[:, :B_WIDTH] * _one_plus_tanh(p[:, B_WIDTH:])


def _in_proj(x, gcol, w_in, w_out, ffn_gcol, ffn_weights):
    bsz, seq, d = x.shape
    tm = IN_PROJ_TILE
    inner = seq // DFT_RADIX
    steps_per_batch = seq // tm
    n_chunks = ffn_weights[0].shape[2] // FF_PREP_CHUNK
    n_steps = bsz * steps_per_batch
    assert n_steps >= max(n_chunks, X_RING_SLOTS - 1)
    chunk_of = lambda b, i: jnp.minimum(b * steps_per_batch + i, n_chunks - 1)
    prep_in, prep_out, prep_shape = _prep_specs(ffn_gcol, *ffn_weights, 0, chunk_of)
    return pl.pallas_call(
        functools.partial(_in_proj_kernel, n_chunks=n_chunks, g_layer=gcol[2],
                          fg_layer=ffn_gcol[2]),
        grid=(bsz, steps_per_batch),
        in_specs=[
            pl.BlockSpec(memory_space=pl.ANY),
            gcol[1],
            _resident(w_in.shape),
            _resident(w_out.shape),
        ] + prep_in,
        out_specs=[
            pl.BlockSpec((None, DFT_RADIX, tm // DFT_RADIX, A_WIDTH), lambda b, i: (b, 0, i, 0)),
            pl.BlockSpec((None, tm, B_WIDTH), lambda b, i: (b, i, 0)),
            _resident(w_out.shape),
        ] + prep_out,
        out_shape=[
            jax.ShapeDtypeStruct((bsz, DFT_RADIX, inner, A_WIDTH), _BF16),
            jax.ShapeDtypeStruct((bsz, seq, B_WIDTH), _F32),
            jax.ShapeDtypeStruct(w_out.shape, _BF16),
        ] + prep_shape,
        scratch_shapes=[pltpu.VMEM((A_WIDTH // LANES, tm, LANES), _F32),
                        pltpu.VMEM(w_in.shape, _BF16),
                        pltpu.VMEM((X_RING_SLOTS, tm, d), _F32),
                        pltpu.SemaphoreType.DMA((X_RING_SLOTS,)),
                        pltpu.VMEM((d, 1), _F32)],
        compiler_params=pltpu.CompilerParams(
            dimension_semantics=("arbitrary", "arbitrary"),
            vmem_limit_bytes=V7X_VMEM_LIMIT_BYTES),
        name="in_proj",
    )(x.reshape(n_steps, tm, d), gcol[0], w_in, w_out, ffn_gcol[0], *ffn_weights)


def _cadd(a, b):
    return (a[0] + b[0], a[1] + b[1])


def _csub(a, b):
    return (a[0] - b[0], a[1] - b[1])


def _dft4(a0, a1, a2, a3):
    s0, s1 = _cadd(a0, a2), _csub(a0, a2)
    s2, s3 = _cadd(a1, a3), _csub(a1, a3)
    return (_cadd(s0, s2), (s1[0] + s3[1], s1[1] - s3[0]),
            _csub(s0, s2), (s1[0] - s3[1], s1[1] + s3[0]))


def _mul_w8(k, z):
    r, i = z
    h = math.sqrt(0.5)
    if k == 0:
        return z
    if k == 1:
        return (h * (r + i), h * (i - r))
    if k == 2:
        return (i, -r)
    return (h * (i - r), -h * (r + i))


def _bfly_unit(r0, lanes, inner, twc_ref, tws_ref, yr_scr, yi_scr):
    rt = BFLY_ROW_TILE
    z = []
    for jr in range(DFT_RADIX):
        rows = slice(jr * inner + r0, jr * inner + r0 + rt)
        yr = yr_scr[rows, lanes]
        yi = yi_scr[rows, lanes]
        if jr == 0:
            z.append((yr, yi))
        else:
            tc = twc_ref[rows, :]
            ts = tws_ref[rows, :]
            z.append((yr * tc + yi * ts, yi * tc - yr * ts))
    ev = _dft4(z[0], z[2], z[4], z[6])
    od = _dft4(z[1], z[3], z[5], z[7])
    for k in range(4):
        w = _mul_w8(k, od[k])
        lo = _cadd(ev[k], w)
        hi = _csub(ev[k], w)
        rows_lo = slice(k * inner + r0, k * inner + r0 + rt)
        rows_hi = slice((k + 4) * inner + r0, (k + 4) * inner + r0 + rt)
        yr_scr[rows_lo, lanes] = lo[0]
        yi_scr[rows_lo, lanes] = lo[1]
        yr_scr[rows_hi, lanes] = hi[0]
        yi_scr[rows_hi, lanes] = hi[1]


def _fnet_kernel(a_ref, cs_ref, twc_ref, tws_ref, cdsd_ref, map_ref, y_ref, yr_scr, yi_scr):
    seq = y_ref.shape[0]
    inner = seq // DFT_RADIX

    for c0 in range(0, A_WIDTH, FNET_COLS):
        cols = slice(c0, c0 + FNET_COLS)
        for jr in range(DFT_RADIX):
            yy = jnp.dot(cs_ref[...], a_ref[jr, :, cols], preferred_element_type=_F32)
            yr_scr[jr * inner:(jr + 1) * inner, cols] = yy[:inner]
            yi_scr[jr * inner:(jr + 1) * inner, cols] = yy[inner:]

        for r0 in range(0, inner, BFLY_ROW_TILE):
            for l0 in range(c0, c0 + FNET_COLS, LANES):
                _bfly_unit(r0, slice(l0, l0 + LANES), inner, twc_ref, tws_ref, yr_scr, yi_scr)

        for hd in range(c0 // HEAD_DIM, (c0 + FNET_COLS) // HEAD_DIM):
            lanes = slice(hd * HEAD_DIM, (hd + 1) * HEAD_DIM)
            lhs = jnp.concatenate([yr_scr[:, lanes].astype(_BF16),
                                   yi_scr[:, lanes].astype(_BF16)], axis=1)
            f = jnp.dot(lhs, cdsd_ref[...], preferred_element_type=_F32)
            ya = jnp.dot(f.astype(_BF16), map_ref[hd].astype(_BF16), preferred_element_type=_F32)
            y_ref[:, lanes] = ya.astype(_BF16)


def _fnet(a_perm, consts, fmap):
    bsz, _, inner, _ = a_perm.shape
    seq = inner * DFT_RADIX
    cs, twc, tws, cdsd = consts
    return pl.pallas_call(
        _fnet_kernel,
        grid=(bsz,),
        in_specs=[
            pl.BlockSpec((None, DFT_RADIX, inner, A_WIDTH), lambda b: (b, 0, 0, 0)),
            _resident(cs.shape), _resident(twc.shape), _resident(tws.shape),
            _resident(cdsd.shape), _resident(fmap.shape),
        ],
        out_specs=pl.BlockSpec((None, seq, A_WIDTH), lambda b: (b, 0, 0)),
        out_shape=jax.ShapeDtypeStruct((bsz, seq, A_WIDTH), _BF16),
        scratch_shapes=[
            pltpu.VMEM((seq, A_WIDTH), _F32),
            pltpu.VMEM((seq, A_WIDTH), _F32),
        ],
        compiler_params=pltpu.CompilerParams(
            dimension_semantics=("arbitrary",),
            vmem_limit_bytes=V7X_VMEM_LIMIT_BYTES),
        name="fnet",
    )(a_perm, cs, twc, tws, cdsd, fmap)


def _dft_constants(seq):
    inner = seq // DFT_RADIX
    k = np.arange(inner, dtype=np.float64)
    ang = 2.0 * np.pi * np.outer(k, k) / inner
    cs = np.concatenate([np.cos(ang), -np.sin(ang)], axis=0)
    jr = np.arange(DFT_RADIX, dtype=np.float64)[:, None]
    tw = 2.0 * np.pi * (jr * k[None, :]) / seq
    twc = np.repeat(np.cos(tw).reshape(seq, 1), LANES, axis=1)
    tws = np.repeat(np.sin(tw).reshape(seq, 1), LANES, axis=1)
    d = np.arange(HEAD_DIM, dtype=np.float64)
    angd = 2.0 * np.pi * np.outer(d, d) / HEAD_DIM
    scale = 1.0 / math.sqrt(seq * HEAD_DIM)
    cdsd = np.concatenate([np.cos(angd), np.sin(angd)], axis=0) * scale
    return (jnp.asarray(cs, _F32).astype(_BF16), jnp.asarray(twc, _F32), jnp.asarray(tws, _F32),
            jnp.asarray(cdsd, _F32).astype(_BF16))


def _conv_fill_window(tile, tiles_per_seq, main_ref, prev_ref, next_ref, win_scr):
    tm = main_ref.shape[0]
    halo = CONV_HALO
    pos = jnp.zeros((halo, 1), jnp.int32) + tile % tiles_per_seq
    win_scr[0:halo, :] = jnp.where(pos == 0, 0.0, prev_ref[...])
    win_scr[halo:halo + tm, :] = main_ref[...]
    win_scr[halo + tm:, :] = jnp.where(pos == tiles_per_seq - 1, 0.0, next_ref[...])


def _conv_unit(rc, lt, win_scr, cw_ref, cb_ref, lg_ref, lb_ref, out_ref, ct=CONV_ROW_TILE):
    halo = CONV_HALO
    r0 = rc * ct
    lanes = slice(lt * LANES, (lt + 1) * LANES)
    first = halo - CONV_PAD
    span = ct + 2 * halo
    win = win_scr[r0:r0 + span, lanes]
    acc = None
    for s in range(SUBLANES):
        rot = win if s == 0 else pltpu.roll(win, span - s, axis=0)
        for q in range((first + CONV_WIDTH - 1) // SUBLANES + 1):
            k = SUBLANES * q + s - first
            if 0 <= k < CONV_WIDTH:
                term = rot[SUBLANES * q:SUBLANES * q + ct] * cw_ref[k:k + 1, lanes]
                acc = term if acc is None else acc + term
    cv = acc + cb_ref[:, lanes]
    mu = jnp.mean(cv, axis=-1, keepdims=True)
    dv = cv - mu
    var = jnp.mean(dv * dv, axis=-1, keepdims=True)
    half_yn = (dv * jax.lax.rsqrt(var + LN_EPS)) * (lg_ref[:, lanes] * 0.5) + lb_ref[:, lanes] * 0.5
    y = half_yn * _one_plus_tanh(half_yn)
    out_ref[r0:r0 + ct, lanes] = y.astype(_BF16)


def _conv_units(tm, ct=CONV_ROW_TILE):
    return [(rc, lt) for rc in range(tm // ct) for lt in range(B_WIDTH // LANES)]


def _conv_scratch(tm):
    return [pltpu.VMEM((tm + 2 * CONV_HALO, B_WIDTH), _F32)]


def _conv_first_kernel(main_ref, next_ref, cw_ref, cb_ref, lg_ref, lb_ref, out_ref, win_scr):
    _conv_fill_window(0, 2, main_ref, next_ref, next_ref, win_scr)
    for rc, lt in _conv_units(main_ref.shape[0], CONV_FIRST_ROW_TILE):
        _conv_unit(rc, lt, win_scr, cw_ref, cb_ref, lg_ref, lb_ref, out_ref, CONV_FIRST_ROW_TILE)


def _conv_first(u2d, conv_w, conv_b, ln_g, ln_b):
    tm = TOKEN_TILE
    return pl.pallas_call(
        _conv_first_kernel,
        grid=(1,),
        in_specs=[
            pl.BlockSpec((tm, B_WIDTH), lambda i: (0, 0)),
            pl.BlockSpec((CONV_HALO, B_WIDTH), lambda i: (tm // CONV_HALO, 0)),
            _resident_slab(conv_w.shape), _resident(conv_b.shape), _resident(ln_g.shape),
            _resident(ln_b.shape),
        ],
        out_specs=pl.BlockSpec((tm, B_WIDTH), lambda i: (0, 0)),
        out_shape=jax.ShapeDtypeStruct((tm, B_WIDTH), _BF16),
        scratch_shapes=_conv_scratch(tm),
        compiler_params=pltpu.CompilerParams(
            dimension_semantics=("arbitrary",),
            vmem_limit_bytes=V7X_VMEM_LIMIT_BYTES),
        name="conv_first",
    )(u2d, u2d, conv_w, conv_b, ln_g, ln_b)


def _pool_fill_units(tile, tiles_per_seq, x_ref, prev_ref, next_ref, hp_scr):
    tm = x_ref.shape[0]
    seq = tm * tiles_per_seq
    rt = POOL_ROW_TILE

    def halos():
        start = (tile % tiles_per_seq) * tm
        halo_iota = jax.lax.broadcasted_iota(jnp.int32, (POOL_HALO, 1), 0)
        prev_ok = (start - POOL_HALO + halo_iota) >= 0
        next_ok = (start + tm + halo_iota) < seq
        hp_scr[0:POOL_HALO, :] = jnp.where(prev_ok, _rms_scale(prev_ref[...]), 0.0)
        hp_scr[POOL_HALO + tm:, :] = jnp.where(next_ok, _rms_scale(next_ref[...]), 0.0)

    def rows(r0):
        hp_scr[POOL_HALO + r0:POOL_HALO + r0 + rt, :] = _rms_scale(x_ref[r0:r0 + rt, :])

    return [halos] + [functools.partial(rows, r0) for r0 in range(0, tm, rt)]


def _pool_units(tm):
    return [(rc, gi) for rc in range(tm // POOL_ROW_TILE) for gi in range(len(POOL_WINDOWS))]


def _pool_unit(rc, gi, tile, tiles_per_seq, hp_scr, out_ref):
    rt = POOL_ROW_TILE
    w = POOL_WINDOWS[gi]
    half = w // 2
    tm = hp_scr.shape[0] - 2 * POOL_HALO
    gd = hp_scr.shape[1] // len(POOL_WINDOWS)
    seq = tm * tiles_per_seq
    lanes = slice(gi * gd, (gi + 1) * gd)
    r0 = rc * rt
    n = rt + 2 * POOL_HALO
    fwd = hp_scr[r0:r0 + n, lanes]
    span = 1
    while span < half:
        fwd = fwd + pltpu.roll(fwd, n - span, axis=0)
        span *= 2
    centred = fwd + pltpu.roll(fwd, half, axis=0)
    win = centred[POOL_HALO:POOL_HALO + rt]

    def edge_mean(e0):
        pos = ((tile % tiles_per_seq) * tm + r0 + e0
               + jax.lax.broadcasted_iota(jnp.int32, (POOL_HALO, LANES), 0))
        cnt = jnp.minimum(pos + (w - half), seq) - jnp.maximum(pos - half, 0)
        inv = 1.0 / cnt.astype(_F32)
        return win[e0:e0 + POOL_HALO] * jnp.concatenate([inv] * (gd // LANES), axis=1)

    lo = POOL_HALO if rc == 0 else 0
    hi = rt - POOL_HALO if rc == tm // rt - 1 else rt
    parts = [win[lo:hi] * (1.0 / w)]
    if lo:
        parts.insert(0, edge_mean(0))
    if hi < rt:
        parts.append(edge_mean(hi))
    mean = jnp.concatenate(parts, axis=0)
    pgv = mean - hp_scr[POOL_HALO + r0:POOL_HALO + r0 + rt, lanes]
    out_ref[r0:r0 + rt, lanes] = pgv.astype(_BF16)
    return pgv[rt - SUBLANES:, :LANES]


def _derived_zero(dep):
    bits = pltpu.bitcast(dep, jnp.uint32)
    bits = jax.lax.shift_right_logical(jax.lax.shift_right_logical(bits, jnp.uint32(16)), jnp.uint32(16))
    zero = pltpu.bitcast(bits, _F32)
    return jnp.concatenate([zero, zero], axis=0)


def _pool_first_kernel(x_ref, next_ref, out_ref, hp_scr):
    for unit in _pool_fill_units(0, 2, x_ref, next_ref, next_ref, hp_scr):
        unit()
    for rc, gi in _pool_units(x_ref.shape[0]):
        _pool_unit(rc, gi, 0, 2, hp_scr, out_ref)


def _pool_first(x2d):
    tm = TOKEN_TILE
    d = x2d.shape[1]
    return pl.pallas_call(
        _pool_first_kernel,
        grid=(1,),
        in_specs=[
            pl.BlockSpec((tm, d), lambda i: (0, 0)),
            pl.BlockSpec((POOL_HALO, d), lambda i: (tm // POOL_HALO, 0)),
        ],
        out_specs=pl.BlockSpec((tm, d), lambda i: (0, 0)),
        out_shape=jax.ShapeDtypeStruct((tm, d), _BF16),
        scratch_shapes=[pltpu.VMEM((tm + 2 * POOL_HALO, d), _F32)],
        compiler_params=pltpu.CompilerParams(
            dimension_semantics=("arbitrary",),
            vmem_limit_bytes=V7X_VMEM_LIMIT_BYTES),
        name="pool_first",
    )(x2d, x2d)


def _ffn_kernel(*refs, has_mixer, has_pool, has_final, has_prep, tiles_per_seq, n_chunks,
                prep_layer, pool_layer):
    it = iter(refs)
    x_ref = next(it)
    if has_mixer:
        ya_ref = next(it)
        yb0_ref = next(it)
        u_refs = (next(it), next(it), next(it))
        conv_refs = (next(it), next(it), next(it), next(it))
        wo_ref = next(it)
    if has_pool:
        pg0_ref = next(it)
        xn_refs = (next(it), next(it), next(it))
        gcol_ref, pm_ref, ps_ref = next(it), next(it), next(it)
    wgu_ref = next(it)
    wd_ref = next(it)
    if has_prep:
        prep_in = (next(it), next(it), next(it), next(it))
    if has_final:
        fg_ref = next(it)
    o_ref = next(it)
    if has_prep:
        prep_out = (next(it), next(it))
    h_scr = next(it)
    act_scr = next(it)
    if has_mixer:
        yb_scr = next(it)
        win_scr = next(it)
    if has_pool:
        pg_scr = next(it)
        hp_scr = next(it)
        pm_scr = next(it)
        xv_scr = next(it)
        hn_scr = next(it)
    if has_prep:
        gcol_scr = next(it)

    i = pl.program_id(0)
    n = pl.num_programs(0)
    slot = i % 2
    nxt = jnp.minimum(i + 1, n - 1)
    if has_prep:
        @pl.when(i == 0)
        def _():
            gcol_scr[...] = _gain_column(prep_in[0], prep_layer)

        @pl.when(i < n_chunks)
        def _():
            _prep_ffn_weights(gcol_scr, *prep_in[1:], *prep_out)

    tm, d = x_ref.shape
    xv = x_ref[...]
    units = []
    if has_mixer:
        @pl.when(i == 0)
        def _():
            yb_scr[0] = yb0_ref[...]

        yb = yb_scr[slot]
        _conv_fill_window(nxt, tiles_per_seq, *u_refs, win_scr)
        units = [functools.partial(_conv_unit, rc, lt, win_scr, *conv_refs, yb_scr.at[1 - slot])
                 for rc, lt in _conv_units(tm)]
    if has_pool:
        gd = d // len(POOL_WINDOWS)

        @pl.when(i == 0)
        def _():
            pg_scr[0] = pg0_ref[...]
            gain = _gain_column(gcol_ref, pool_layer)
            for gi in range(len(POOL_WINDOWS)):
                pm_scr[gi] = (pm_ref[gi] * gain[gi * gd:(gi + 1) * gd, :]).astype(_BF16)

        def front(x_val, pg_val):
            ys = [jnp.dot(pg_val[:, gi * gd:(gi + 1) * gd], pm_scr[gi], preferred_element_type=_F32)
                  for gi in range(len(POOL_WINDOWS))]
            xf = x_val + jnp.concatenate(ys, axis=1) * ps_ref[...]
            return xf, _rms_scale(xf).astype(_BF16)

        @pl.when(i == 0)
        def _():
            xv_scr[0], hn_scr[0] = front(x_ref[...], pg0_ref[...])

        units = _pool_fill_units(nxt, tiles_per_seq, *xn_refs, hp_scr) + [
            functools.partial(_pool_unit, rc, gi, nxt, tiles_per_seq, hp_scr, pg_scr.at[1 - slot])
            for rc, gi in _pool_units(tm)]
    pending = iter(units)
    units_per_dot = len(units) // n_chunks

    def side_work(n_units):
        dep = None
        for _ in range(n_units):
            out = next(pending)()
            dep = dep if out is None else out
        return dep

    side_work(len(units) - n_chunks * units_per_dot)
    if has_mixer:
        yab = jnp.concatenate([ya_ref[...], yb], axis=1)
        xv = xv + jnp.dot(yab, wo_ref[...], preferred_element_type=_F32)
    if has_pool:
        h_scr = hn_scr.at[slot]
        xv = xv_scr[slot]
        hb = h_scr[...]
        h_tile = hb[:2 * SUBLANES, :LANES].astype(_F32)
    else:
        hb = _rms_scale(xv).astype(_BF16)

    gu_width = wgu_ref.shape[2]
    for c in range(n_chunks):
        dep = side_work(units_per_dot)
        if dep is not None:
            h_scr[:2 * SUBLANES, :LANES] = (h_tile + _derived_zero(dep)).astype(_BF16)
            hb = h_scr[...]
        gu = jnp.dot(hb, wgu_ref[c], preferred_element_type=_F32)
        for b0 in range(0, gu_width, 2 * LANES):
            half_gate = gu[:, b0:b0 + LANES]
            up = gu[:, b0 + LANES:b0 + 2 * LANES]
            f0 = (c * gu_width + b0) // 2
            act = (half_gate * up) * _one_plus_tanh(half_gate)
            act_scr[:, f0:f0 + LANES] = act.astype(_BF16)
    acc = xv + jnp.dot(act_scr[...], wd_ref[...], preferred_element_type=_F32)
    if has_final:
        acc = _rmsnorm(acc, fg_ref[...])
    o_ref[...] = acc
    if has_pool:
        xv_scr[1 - slot], hn_scr[1 - slot] = front(xn_refs[0][...], pg_scr[1 - slot])


def _ffn(x2d, wgu, wd, mixer=None, pool=None, final_g=None, prep=None, seq=None):
    tokens, d = x2d.shape
    n_chunks, _, _ = wgu.shape
    ff = wd.shape[0]
    tm = TOKEN_TILE
    tiles_per_seq = seq // tm
    n_tiles = tokens // tm
    has_mixer = mixer is not None
    has_pool = pool is not None
    has_final = final_g is not None
    has_prep = prep is not None
    assert n_tiles >= n_chunks
    args = [x2d]
    specs = [pl.BlockSpec((tm, d), lambda i: (i, 0))]
    out_specs = [pl.BlockSpec((tm, d), lambda i: (i, 0))]
    out_shape = [jax.ShapeDtypeStruct((tokens, d), _F32)]
    scratch = [pltpu.VMEM((tm, d), _BF16), pltpu.VMEM((tm, ff), _BF16)]
    if has_mixer:
        ya2d, yb0, u2d, conv_w, conv_b, ln_g, ln_b, w_out = mixer
        hb = tm // CONV_HALO
        n_halo = tokens // CONV_HALO
        nxt = lambda i: jnp.minimum(i + 1, n_tiles - 1)
        args += [ya2d, yb0, u2d, u2d, u2d, conv_w, conv_b, ln_g, ln_b, w_out]
        specs += [
            pl.BlockSpec((tm, A_WIDTH), lambda i: (i, 0)),
            _resident(yb0.shape),
            pl.BlockSpec((tm, B_WIDTH), lambda i: (nxt(i), 0)),
            pl.BlockSpec((CONV_HALO, B_WIDTH), lambda i: (jnp.maximum(nxt(i) * hb - 1, 0), 0)),
            pl.BlockSpec((CONV_HALO, B_WIDTH),
                         lambda i: (jnp.minimum((nxt(i) + 1) * hb, n_halo - 1), 0)),
            _resident_slab(conv_w.shape), _resident(conv_b.shape), _resident(ln_g.shape),
            _resident(ln_b.shape), _resident(w_out.shape),
        ]
        scratch += [pltpu.VMEM((2, tm, B_WIDTH), _BF16)] + _conv_scratch(tm)
    if has_pool:
        pg0, g_mix, pool_map, pool_scale = pool
        hb = tm // POOL_HALO
        n_halo = tokens // POOL_HALO
        nxt = lambda i: jnp.minimum(i + 1, n_tiles - 1)
        args += [pg0, x2d, x2d, x2d, g_mix[0], pool_map, pool_scale]
        specs += [
            _resident(pg0.shape),
            pl.BlockSpec((tm, d), lambda i: (nxt(i), 0)),
            pl.BlockSpec((POOL_HALO, d), lambda i: (jnp.maximum(nxt(i) * hb - 1, 0), 0)),
            pl.BlockSpec((POOL_HALO, d), lambda i: (jnp.minimum((nxt(i) + 1) * hb, n_halo - 1), 0)),
            g_mix[1], _resident(pool_map.shape), _resident(pool_scale.shape),
        ]
        scratch += [pltpu.VMEM((2, tm, d), _BF16), pltpu.VMEM((tm + 2 * POOL_HALO, d), _F32),
                    pltpu.VMEM(pool_map.shape, _BF16), pltpu.VMEM((2, tm, d), _F32),
                    pltpu.VMEM((2, tm, d), _BF16)]
    args += [wgu, wd]
    specs += [_resident(wgu.shape), _resident(wd.shape)]
    if has_prep:
        prep_in, prep_out, prep_shape = _prep_specs(
            *prep, lambda i: jnp.minimum(i, n_chunks - 1))
        args += [prep[0][0], *prep[1:4]]
        specs += prep_in
        out_specs += prep_out
        out_shape += prep_shape
        scratch.append(pltpu.VMEM((d, 1), _F32))
    if has_final:
        args.append(final_g)
        specs.append(_resident(final_g.shape))
    outs = pl.pallas_call(
        functools.partial(_ffn_kernel, has_mixer=has_mixer, has_pool=has_pool, has_final=has_final,
                          has_prep=has_prep, tiles_per_seq=tiles_per_seq, n_chunks=n_chunks,
                          prep_layer=prep[0][2] if has_prep else None,
                          pool_layer=pool[1][2] if has_pool else None),
        grid=(n_tiles,),
        in_specs=specs,
        out_specs=out_specs,
        out_shape=out_shape,
        scratch_shapes=scratch,
        compiler_params=pltpu.CompilerParams(
            dimension_semantics=("arbitrary",),
            vmem_limit_bytes=V7X_VMEM_LIMIT_BYTES),
        name="ffn_mixer" if has_mixer else "ffn_final",
    )(*args)
    return outs if has_prep else outs[0]


def kernel(x, norm_mix_g, norm_ffn_g, w_in_ab, fnet_map, conv_w, conv_b, conv_ln_g, conv_ln_b,
           w_out_ab, pool_map, pool_scale, ffn_w_gate, ffn_w_up, ffn_w_down, final_g):
    bsz, seq, d = x.shape
    tokens = bsz * seq
    row = lambda v: v.reshape(1, -1)

    ffn_weights = (ffn_w_gate, ffn_w_up, ffn_w_down)
    a_perm, u, w_out, wgu0, wd0 = _in_proj(x, _layer_row(norm_mix_g, 0), w_in_ab[0], w_out_ab[0],
                                           _layer_row(norm_ffn_g, 0), ffn_weights)
    ya = _fnet(a_perm, _dft_constants(seq), fnet_map[0])
    u2d = u.reshape(tokens, B_WIDTH)
    conv_p = (conv_w, row(conv_b[0]), row(conv_ln_g[0]), row(conv_ln_b[0]))
    mixer = (ya.reshape(tokens, A_WIDTH), _conv_first(u2d, *conv_p), u2d, *conv_p, w_out)
    x2, wgu1, wd1 = _ffn(x.reshape(tokens, d), wgu0, wd0, mixer=mixer,
                         prep=(_layer_row(norm_ffn_g, 1), *ffn_weights, 1), seq=seq)

    pool = (_pool_first(x2), _layer_row(norm_mix_g, 1), pool_map[0], row(pool_scale[0]))
    out = _ffn(x2, wgu1, wd1, pool=pool, final_g=row(final_g), seq=seq)
    return out.reshape(bsz, seq, d)
```
